```python
import math
import jax
import jax.numpy as jnp
from jax import lax
import numpy as np

D_MODEL = 1024
BATCH = 8
SEQ = 2048
DEPTH = 2

HEAD_DIM = 64
BRANCH_WIDTH = D_MODEL // 4
N_HEADS = BRANCH_WIDTH // HEAD_DIM
N_BRANCHES = 5
DIFF_QK_DIM = HEAD_DIM // 2
DIL_PATTERNS = ((128, 1), (512, 4), (2048, 16))
S5_GROUP = 16
S5_GROUPS = BRANCH_WIDTH // S5_GROUP
S5_STATE = 64
S5_DT_MIN = 1e-3
S5_DT_MAX = 1e-1
CMP_BLOCK = 32
CMP_STRIDE = 16
CMP_HIDDEN = 256
SLC_BLOCK = 64
N_SELECT = 16
WINDOW = 512
MEM_LEN = 256
MEM_HEADS = 4
ROPE_THETA = 10000.0
Q_BLOCK = 128
G_BLOCK = 64
RMS_EPS = 1e-6
NEG_INF = -1e30
FORCE_SCORE = 1e9
IN_SIZES = (BRANCH_WIDTH, BRANCH_WIDTH, BRANCH_WIDTH, BRANCH_WIDTH,
            BRANCH_WIDTH, BRANCH_WIDTH, BRANCH_WIDTH, BRANCH_WIDTH,
            BRANCH_WIDTH, BRANCH_WIDTH,
            BRANCH_WIDTH, HEAD_DIM, HEAD_DIM, HEAD_DIM, HEAD_DIM, HEAD_DIM, HEAD_DIM,
            3 * N_HEADS, BRANCH_WIDTH,
            MEM_HEADS * HEAD_DIM, MEM_HEADS * HEAD_DIM)
IN_COLS = sum(IN_SIZES)

kernel_name = 'hybrid_gated_parallel_mixer'


def rmsnorm(x, g):
    x32 = x.astype(jnp.float32)
    y = x32 * lax.rsqrt(jnp.mean(x32 * x32, axis=-1, keepdims=True) + RMS_EPS)
    return (y * g.astype(jnp.float32)).astype(x.dtype)


def rope(x):
    s_len, half = x.shape[1], x.shape[-1] // 2
    inv_freq = ROPE_THETA ** (-jnp.arange(half, dtype=jnp.float32) / half)
    ang = jnp.arange(s_len, dtype=jnp.float32)[:, None] * inv_freq[None, :]
    shape = (1, s_len) + (1,) * (x.ndim - 3) + (half,)
    cos = jnp.cos(ang).reshape(shape)
    sin = jnp.sin(ang).reshape(shape)
    x32 = x.astype(jnp.float32)
    x1, x2 = x32[..., :half], x32[..., half:]
    return jnp.concatenate([x1 * cos - x2 * sin, x2 * cos + x1 * sin], axis=-1).astype(x.dtype)


def masked_softmax(s, mask):
    s = jnp.where(mask, s, NEG_INF)
    m = jnp.max(s, axis=-1, keepdims=True)
    e = jnp.where(mask, jnp.exp(s - m), 0.0)
    return e / jnp.maximum(jnp.sum(e, axis=-1, keepdims=True), 1e-30)


def to_blocks(a, blk):
    b, s_len = a.shape[0], a.shape[1]
    return jnp.moveaxis(a.reshape((b, s_len // blk, blk) + a.shape[2:]), 1, 0)


def from_blocks(o):
    o = jnp.moveaxis(o, 0, 1)
    return o.reshape((o.shape[0], o.shape[1] * o.shape[2]) + o.shape[3:])


def diff_attention(q, k, v, lam, lam_init, subln_g):
    s_len, dqk = q.shape[1], q.shape[-1]
    kpos = jnp.arange(s_len)
    v32 = v.astype(jnp.float32)

    def block(args):
        qi, bi = args
        qpos = bi * Q_BLOCK + jnp.arange(Q_BLOCK)
        mask = kpos[None, :] <= qpos[:, None]
        s = jnp.einsum('bqhcd,bkhcd->bhcqk', qi, k).astype(jnp.float32) * dqk ** -0.5
        p = masked_softmax(s, mask)
        a = p[:, :, 0] - lam * p[:, :, 1]
        return jnp.einsum('bhqk,bkhd->bqhd', a, v32)

    o = from_blocks(lax.map(block, (to_blocks(q, Q_BLOCK), jnp.arange(s_len // Q_BLOCK))))
    return rmsnorm(o, subln_g) * (1.0 - lam_init)


def dilated_attention(q, k, v):
    s_len, hd = q.shape[1], q.shape[-1]

    def block(args):
        qi, bi = args
        qpos = bi * G_BLOCK + jnp.arange(G_BLOCK)
        outs, lses = [], []
        for window, dil in DIL_PATTERNS:
            offs = np.arange(window // dil + 1) * dil
            kidx = qpos[:, None] - offs[None, :]
            valid = (kidx >= 0)[None, None]
            kidx = jnp.maximum(kidx, 0)
            kg = k[:, kidx]
            vg = v[:, kidx].astype(jnp.float32)
            s = jnp.einsum('bqhd,bqkhd->bhqk', qi, kg).astype(jnp.float32) * hd ** -0.5
            s = jnp.where(valid, s, NEG_INF)
            m = jnp.max(s, axis=-1, keepdims=True)
            e = jnp.where(valid, jnp.exp(s - m), 0.0)
            den = jnp.sum(e, axis=-1, keepdims=True)
            outs.append(jnp.einsum('bhqk,bqkhd->bhqd', e, vg) / den)
            lses.append(m + jnp.log(den))
        w = jax.nn.softmax(jnp.stack(lses, 0), axis=0)
        o = jnp.sum(w * jnp.stack(outs, 0), axis=0)
        return jnp.swapaxes(o, 1, 2)

    return from_blocks(lax.map(block, (to_blocks(q, G_BLOCK), jnp.arange(s_len // G_BLOCK))))


def ssm_combine(e1, e2):
    a1r, a1i, b1r, b1i = e1
    a2r, a2i, b2r, b2i = e2
    return (a2r * a1r - a2i * a1i,
            a2r * a1i + a2i * a1r,
            a2r * b1r - a2i * b1i + b2r,
            a2r * b1i + a2i * b1r + b2i)


def s5_branch(u, lam_re, lam_im, log_dt, b_re, b_im, c_re, c_im, d_skip, w_glu, b_glu):
    bsz, s_len = u.shape[0], u.shape[1]
    f32 = jnp.float32
    u = u.astype(f32).reshape(bsz, s_len, S5_GROUPS, S5_GROUP)
    lr, li = lam_re.astype(f32), lam_im.astype(f32)
    dt = jnp.exp(log_dt.astype(f32))[:, None]
    mag = jnp.exp(lr * dt)
    a_re, a_im = mag * jnp.cos(li * dt), mag * jnp.sin(li * dt)
    den = lr * lr + li * li
    n_re, n_im = a_re - 1.0, a_im
    z_re = (n_re * lr + n_im * li) / den
    z_im = (n_im * lr - n_re * li) / den
    br, bi = b_re.astype(f32), b_im.astype(f32)
    bb_re = z_re[..., None] * br - z_im[..., None] * bi
    bb_im = z_re[..., None] * bi + z_im[..., None] * br
    bu_re = jnp.einsum('gnp,bsgp->bsgn', bb_re, u)
    bu_im = jnp.einsum('gnp,bsgp->bsgn', bb_im, u)
    a_re_t = jnp.broadcast_to(a_re, bu_re.shape)
    a_im_t = jnp.broadcast_to(a_im, bu_im.shape)
    _, _, x_re, x_im = lax.associative_scan(ssm_combine, (a_re_t, a_im_t, bu_re, bu_im), axis=1)
    y = (jnp.einsum('gpn,bsgn->bsgp', c_re.astype(f32), x_re)
         - jnp.einsum('gpn,bsgn->bsgp', c_im.astype(f32), x_im)
         + d_skip.astype(f32) * u)
    y = y.reshape(bsz, s_len, BRANCH_WIDTH)
    t = jax.nn.gelu(y) @ w_glu.astype(f32) + b_glu.astype(f32)
    return t[..., :BRANCH_WIDTH] * jax.nn.sigmoid(t[..., BRANCH_WIDTH:])


def nsa_attention(q, kc, vc, ks, vs, kw, vw, gates, pe, w1, w2):
    bsz, s_len, _, hd = q.shape
    f32 = jnp.float32
    scale = hd ** -0.5
    pos = jnp.arange(s_len)

    n_cmp = (s_len - CMP_BLOCK) // CMP_STRIDE + 1
    cidx = np.arange(n_cmp)[:, None] * CMP_STRIDE + np.arange(CMP_BLOCK)[None, :]

    def compress(t, pe_i, w1_i, w2_i):
        blk = t[:, cidx] + pe_i
        return jax.nn.gelu(blk.reshape(bsz, n_cmp, CMP_BLOCK * hd) @ w1_i) @ w2_i

    k_cmp = compress(kc, pe[0], w1[0], w2[0])
    v_cmp = compress(vc, pe[1], w1[1], w2[1]).astype(f32)
    cmask = cidx[:, -1][None, :] <= pos[:, None]
    p_cmp = masked_softmax(jnp.einsum('bshd,bnd->bhsn', q, k_cmp).astype(f32) * scale, cmask)
    o_cmp = jnp.einsum('bhsn,bnd->bshd', p_cmp, v_cmp)

    n_slc = s_len // SLC_BLOCK
    n_sel = min(N_SELECT, n_slc)
    c0 = np.arange(n_cmp)[:, None] * CMP_STRIDE
    s0 = np.arange(n_slc)[None, :] * SLC_BLOCK
    overlap = np.clip(np.minimum(c0 + CMP_BLOCK, s0 + SLC_BLOCK) - np.maximum(c0, s0), 0, None) / CMP_STRIDE
    importance = jnp.einsum('bhsn,nj->bsj', p_cmp, jnp.asarray(overlap, dtype=f32))
    qblk = (pos // SLC_BLOCK)[:, None]
    blk = jnp.arange(n_slc)[None, :]
    forced = (blk == 0) | (blk == qblk) | (blk == qblk - 1)
    score = jnp.where(blk <= qblk, jnp.where(forced, FORCE_SCORE, importance), NEG_INF)
    top_val, top_idx = lax.top_k(score, n_sel)
    top_ok = top_val > 0.5 * NEG_INF

    q_r = rope(q)
    ks_r = rope(ks)
    kw_r = rope(kw)
    vs32 = vs.astype(f32)
    gather = jax.vmap(lambda tb, ib: tb[ib])

    def sel_block(args):
        qi, ti, oki, bi = args
        qpos = bi * G_BLOCK + jnp.arange(G_BLOCK)
        tok = (ti[..., None] * SLC_BLOCK + jnp.arange(SLC_BLOCK)).reshape(bsz, G_BLOCK, n_sel * SLC_BLOCK)
        valid = jnp.repeat(oki, SLC_BLOCK, axis=-1) & (tok <= qpos[None, :, None])
        kg = gather(ks_r, tok)
        vg = gather(vs32, tok)
        s = jnp.einsum('bqhd,bqtd->bhqt', qi, kg).astype(f32) * scale
        p = masked_softmax(s, valid[:, None])
        return jnp.einsum('bhqt,bqtd->bqhd', p, vg)

    o_slc = from_blocks(lax.map(sel_block, (to_blocks(q_r, G_BLOCK), to_blocks(top_idx, G_BLOCK),
                                            to_blocks(top_ok, G_BLOCK), jnp.arange(s_len // G_BLOCK))))

    kp = jnp.pad(kw_r, ((0, 0), (WINDOW, 0), (0, 0)))
    vp = jnp.pad(vw.astype(f32), ((0, 0), (WINDOW, 0), (0, 0)))

    def win_block(args):
        qi, bi = args
        start = bi * Q_BLOCK
        kb = lax.dynamic_slice_in_dim(kp, start, Q_BLOCK + WINDOW, axis=1)
        vb = lax.dynamic_slice_in_dim(vp, start, Q_BLOCK + WINDOW, axis=1)
        kpos = start - WINDOW + jnp.arange(Q_BLOCK + WINDOW)
        dist = (start + jnp.arange(Q_BLOCK))[:, None] - kpos[None, :]
        mask = (dist >= 0) & (dist < WINDOW) & (kpos[None, :] >= 0)
        s = jnp.einsum('bqhd,bkd->bhqk', qi, kb).astype(f32) * scale
        p = masked_softmax(s, mask)
        return jnp.einsum('bhqk,bkd->bqhd', p, vb)

    o_win = from_blocks(lax.map(win_block, (to_blocks(q_r, Q_BLOCK), jnp.arange(s_len // Q_BLOCK))))

    g = jax.nn.sigmoid(gates.astype(f32))
    return g[..., 0:1] * o_cmp + g[..., 1:2] * o_slc + g[..., 2:3] * o_win


def memory_attention(q, mem, g, w_kv):
    bsz, m_len = mem.shape[0], mem.shape[1]
    width = MEM_HEADS * HEAD_DIM
    kv = rmsnorm(mem, g) @ w_kv
    k = kv[..., :width].reshape(bsz, m_len, MEM_HEADS, HEAD_DIM)
    v = kv[..., width:].reshape(bsz, m_len, MEM_HEADS, HEAD_DIM).astype(jnp.float32)
    s = jnp.einsum('bshd,bmhd->bhsm', q, k).astype(jnp.float32) * HEAD_DIM ** -0.5
    p = jax.nn.softmax(s, axis=-1)
    return jnp.einsum('bhsm,bmhd->bshd', p, v)


def setup_inputs(seed: int = 0) -> dict:
    key = jax.random.key(seed)
    ks = jax.random.split(key, 26)

    def nrm(k, shape, scale):
        return scale * jax.random.normal(k, shape, jnp.float32)

    n_ids = jnp.arange(S5_STATE, dtype=jnp.float32)
    s5_shape = (DEPTH, S5_GROUPS, S5_STATE)
    return {
        'x': nrm(ks[0], (BATCH, SEQ, D_MODEL), 1.0),
        'mem': nrm(ks[1], (BATCH, MEM_LEN, D_MODEL), 1.0),
        'norm_g': 1.0 + nrm(ks[2], (DEPTH, D_MODEL), 0.02),
        'w_in': nrm(ks[3], (DEPTH, D_MODEL, IN_COLS), D_MODEL ** -0.5),
        'diff_lambda': nrm(ks[4], (DEPTH, 4, DIFF_QK_DIM), 0.1),
        'diff_subln_g': 1.0 + nrm(ks[5], (DEPTH, HEAD_DIM), 0.02),
        's5_lambda_re': -0.5 + nrm(ks[6], s5_shape, 0.01),
        's5_lambda_im': math.pi * n_ids + nrm(ks[7], s5_shape, 0.01),
        's5_log_dt': jax.random.uniform(ks[8], (DEPTH, S5_GROUPS), jnp.float32,
                                        math.log(S5_DT_MIN), math.log(S5_DT_MAX)),
        's5_b_re': nrm(ks[9], (DEPTH, S5_GROUPS, S5_STATE, S5_GROUP), (2 * S5_GROUP) ** -0.5),
        's5_b_im': nrm(ks[10], (DEPTH, S5_GROUPS, S5_STATE, S5_GROUP), (2 * S5_GROUP) ** -0.5),
        's5_c_re': nrm(ks[11], (DEPTH, S5_GROUPS, S5_GROUP, S5_STATE), S5_STATE ** -0.5),
        's5_c_im': nrm(ks[12], (DEPTH, S5_GROUPS, S5_GROUP, S5_STATE), S5_STATE ** -0.5),
        's5_d': nrm(ks[13], (DEPTH, S5_GROUPS, S5_GROUP), 1.0),
        'w_glu': nrm(ks[14], (DEPTH, BRANCH_WIDTH, 2 * BRANCH_WIDTH), BRANCH_WIDTH ** -0.5),
        'b_glu': nrm(ks[15], (DEPTH, 2 * BRANCH_WIDTH), 0.01),
        'nsa_pe': nrm(ks[16], (DEPTH, 2, CMP_BLOCK, HEAD_DIM), 0.02),
        'nsa_w1': nrm(ks[17], (DEPTH, 2, CMP_BLOCK * HEAD_DIM, CMP_HIDDEN), (CMP_BLOCK * HEAD_DIM) ** -0.5),
        'nsa_w2': nrm(ks[18], (DEPTH, 2, CMP_HIDDEN, HEAD_DIM), CMP_HIDDEN ** -0.5),
        'mem_norm_g': 1.0 + nrm(ks[19], (DEPTH, D_MODEL), 0.02),
        'w_mem_kv': nrm(ks[20], (DEPTH, D_MODEL, 2 * MEM_HEADS * HEAD_DIM), D_MODEL ** -0.5),
        'w_merge': nrm(ks[21], (DEPTH, D_MODEL, N_BRANCHES * D_MODEL), D_MODEL ** -0.5),
        'b_merge': nrm(ks[22], (DEPTH, N_BRANCHES * D_MODEL), 0.01),
        'w_branch': nrm(ks[23], (DEPTH, N_BRANCHES, BRANCH_WIDTH, D_MODEL), BRANCH_WIDTH ** -0.5),
        'w_out': nrm(ks[24], (DEPTH, D_MODEL, D_MODEL), D_MODEL ** -0.5),
        'final_g': 1.0 + nrm(ks[25], (D_MODEL,), 0.02),
    }


def reference(x, mem, norm_g, w_in, diff_lambda, diff_subln_g, s5_lambda_re, s5_lambda_im,
              s5_log_dt, s5_b_re, s5_b_im, s5_c_re, s5_c_im, s5_d, w_glu, b_glu, nsa_pe,
              nsa_w1, nsa_w2, mem_norm_g, w_mem_kv, w_merge, b_merge, w_branch, w_out, final_g):
    bsz, s_len = x.shape[0], x.shape[1]
    splits = np.cumsum(np.array(IN_SIZES))[:-1].tolist()
    for l in range(DEPTH):
        h = rmsnorm(x, norm_g[l])
        proj = h @ w_in[l]
        (a_q, a_k, a_v, a_z, b_q, b_k, b_v, b_z, c_u, c_z,
         d_q, d_kc, d_vc, d_ks, d_vs, d_kw, d_vw, d_g, d_z, e_q, e_z) = jnp.split(proj, splits, axis=-1)

        qa = rope(a_q.reshape(bsz, s_len, N_HEADS, 2, DIFF_QK_DIM))
        ka = rope(a_k.reshape(bsz, s_len, N_HEADS, 2, DIFF_QK_DIM))
        va = a_v.reshape(bsz, s_len, N_HEADS, HEAD_DIM)
        dl = diff_lambda[l].astype(jnp.float32)
        lam_init = 0.8 - 0.6 * math.exp(-0.3 * l)
        lam = jnp.exp(jnp.sum(dl[0] * dl[1])) - jnp.exp(jnp.sum(dl[2] * dl[3])) + lam_init
        o_a = diff_attention(qa, ka, va, lam, lam_init, diff_subln_g[l]).reshape(bsz, s_len, BRANCH_WIDTH)

        qb = rope(b_q.reshape(bsz, s_len, N_HEADS, HEAD_DIM))
        kb = rope(b_k.reshape(bsz, s_len, N_HEADS, HEAD_DIM))
        vb = b_v.reshape(bsz, s_len, N_HEADS, HEAD_DIM)
        o_b = dilated_attention(qb, kb, vb).reshape(bsz, s_len, BRANCH_WIDTH)

        o_c = s5_branch(c_u, s5_lambda_re[l], s5_lambda_im[l], s5_log_dt[l], s5_b_re[l], s5_b_im[l],
                        s5_c_re[l], s5_c_im[l], s5_d[l], w_glu[l], b_glu[l])

        o_d = nsa_attention(d_q.reshape(bsz, s_len, N_HEADS, HEAD_DIM), d_kc, d_vc, d_ks, d_vs, d_kw, d_vw,
                            d_g.reshape(bsz, s_len, N_HEADS, 3), nsa_pe[l], nsa_w1[l], nsa_w2[l])
        o_d = o_d.reshape(bsz, s_len, BRANCH_WIDTH)

        o_e = memory_attention(e_q.reshape(bsz, s_len, MEM_HEADS, HEAD_DIM), mem, mem_norm_g[l], w_mem_kv[l])
        o_e = o_e.reshape(bsz, s_len, MEM_HEADS * HEAD_DIM)

        branches = jnp.stack([o_a * jax.nn.silu(a_z), o_b * jax.nn.silu(b_z), o_c * jax.nn.silu(c_z),
                              o_d * jax.nn.silu(d_z), o_e * jax.nn.silu(e_z)], axis=2)
        y = jnp.einsum('bsnc,ncd->bsnd', branches, w_branch[l])
        gate = jax.nn.sigmoid(h @ w_merge[l] + b_merge[l]).reshape(bsz, s_len, N_BRANCHES, D_MODEL)
        mixed = jnp.einsum('bsnd,bsnd->bsd', gate, y)
        x = x + (mixed @ w_out[l]).astype(x.dtype)
    return rmsnorm(x, final_g)
```

```python
import functools
import math

import numpy as np
import jax
import jax.numpy as jnp
from jax import lax
from jax.experimental import pallas as pl
from jax.experimental.pallas import tpu as pltpu

F32 = jnp.float32
BF16 = jnp.bfloat16

HEAD_DIM = 64
BRANCH_WIDTH = 256
N_HEADS = 4
N_BRANCHES = 5
DIFF_QK_DIM = 32
DIL_PATTERNS = ((128, 1), (512, 4), (2048, 16))
S5_GROUP = 16
S5_GROUPS = 16
S5_STATE = 64
CMP_BLOCK = 32
CMP_STRIDE = 16
SLC_BLOCK = 64
N_SELECT = 16
WINDOW = 512
ROPE_THETA = 10000.0
RMS_EPS = 1e-6
NEG_INF = -1e30
FORCE_SCORE = 1e9

V7X_VMEM_BYTES = 64 * 1024 * 1024
BIG_VMEM_LIMIT = V7X_VMEM_BYTES - 8 * 1024 * 1024

ATT_TILE = 256
ROW_TILE = 256
S5_CHUNK = 128

_NT = (((1,), (1,)), ((), ()))


def _rms(x, g):
    return x * lax.rsqrt(jnp.mean(x * x, axis=-1, keepdims=True) + RMS_EPS) * g


def _dot(a, b):
    return jnp.dot(a, b, preferred_element_type=F32)


def _dot_nt(a, b):
    return lax.dot_general(a, b, _NT, preferred_element_type=F32)


def _split_hi_lo(x):
    hi = x.astype(BF16)
    lo = (x - hi.astype(F32)).astype(BF16)
    return hi, lo


def _dot_hilo(x, w):
    hi, lo = _split_hi_lo(x)
    return _dot(hi, w) + _dot(lo, w)


def _const_spec(shape):
    n = len(shape)
    return pl.BlockSpec(shape, lambda *_: (0,) * n, pipeline_mode=pl.Buffered(1))


def _head_mask(h, dtype):
    lane = lax.broadcasted_iota(jnp.int32, (1, BRANCH_WIDTH), 1)
    return jnp.where((lane >= h * HEAD_DIM) & (lane < (h + 1) * HEAD_DIM), 1.0, 0.0).astype(dtype)


_IN_PLAN = (
    ("qa", "rope", "A", 256, BF16), ("ka", "rope", "A", 256, BF16), ("va", "plain", None, 256, BF16),
    ("qb", "rope", "B", 256, BF16), ("kb", "rope", "B", 256, BF16), ("vb", "plain", None, 256, BF16),
    ("cu", "plain", None, 256, F32),
    ("dq", "plain", None, 256, BF16), ("dqr", "rope", "B", 256, BF16),
    ("kvc", "plain", None, 128, F32),
    ("ksr", "rope", "B", 256, BF16), ("vsr", "plain", None, 256, BF16),
    ("kwr", "rope", "B", 256, BF16), ("vwr", "plain", None, 256, BF16),
    ("dg", "plain", None, 128, F32),
    ("eq", "plain", None, 256, BF16),
    ("zs", "silu", None, 1280, F32),
)


def _in_proj_kernel(x_ref, g_ref, w_ref, cosa_ref, sina_ref, cosb_ref, sinb_ref, *out_refs):
    h = _rms(x_ref[...], g_ref[...]).astype(BF16)
    col = 0
    for (_, kind, table, width, _), o_ref in zip(_IN_PLAN, out_refs):
        if kind == "rope":
            y = _dot(h, w_ref[:, col:col + 2 * width])
            cos_ref, sin_ref = (cosa_ref, sina_ref) if table == "A" else (cosb_ref, sinb_ref)
            y = y[:, :width] * cos_ref[...] + y[:, width:] * sin_ref[...]
            col += 2 * width
        else:
            y = _dot(h, w_ref[:, col:col + width])
            if kind == "silu":
                y = y * jax.nn.sigmoid(y)
            col += width
        o_ref[...] = y.astype(o_ref.dtype)


def _rot_cols(w, group):
    d, n = w.shape
    w4 = w.reshape(d, n // group, 2, group // 2)
    return jnp.concatenate([-w4[:, :, 1], w4[:, :, 0]], axis=2).reshape(d, n)


def _in_weights(w):
    sizes = (256,) * 10 + (256, 64, 64, 64, 64, 64, 64, 12, 256, 256, 256)
    offs = np.cumsum((0,) + sizes)
    (a_q, a_k, a_v, a_z, b_q, b_k, b_v, b_z, c_u, c_z, d_q, d_kc, d_vc, d_ks, d_vs, d_kw, d_vw,
     d_g, d_z, e_q, e_z) = [w[:, offs[i]:offs[i + 1]] for i in range(len(sizes))]
    scale = HEAD_DIM ** -0.5
    rep = lambda t: jnp.tile(t, (1, N_HEADS))
    rope = lambda t, g: [t, _rot_cols(t, g)]
    pad = lambda t, n: jnp.pad(t, ((0, 0), (0, n - t.shape[1])))
    cols = (rope(a_q, DIFF_QK_DIM) + rope(a_k, DIFF_QK_DIM) + [a_v]
            + rope(b_q * scale, HEAD_DIM) + rope(b_k, HEAD_DIM) + [b_v]
            + [c_u, d_q * scale] + rope(d_q * scale, HEAD_DIM)
            + [jnp.concatenate([d_kc, d_vc], axis=1)]
            + rope(rep(d_ks), HEAD_DIM) + [rep(d_vs)] + rope(rep(d_kw), HEAD_DIM) + [rep(d_vw)]
            + [pad(d_g, 128), e_q * scale]
            + [a_z, b_z, c_z, d_z, e_z])
    return jnp.concatenate(cols, axis=1).astype(BF16)


def _rope_tables(s_len, group):
    half = group // 2
    inv_freq = ROPE_THETA ** (-jnp.arange(half, dtype=F32) / half)
    ang = jnp.arange(s_len, dtype=F32)[:, None] * inv_freq[None, :]
    reps = BRANCH_WIDTH // half
    return jnp.tile(jnp.cos(ang), (1, reps)), jnp.tile(jnp.sin(ang), (1, reps))


def _in_proj(x2, g, wcat, tables, s_len):
    t, d = x2.shape
    tm = ROW_TILE
    nsb = s_len // tm
    row = lambda w: pl.BlockSpec((tm, w), lambda i: (i, 0))
    tab = pl.BlockSpec((tm, BRANCH_WIDTH), lambda i: (i % nsb, 0))
    return pl.pallas_call(
        _in_proj_kernel,
        grid=(t // tm,),
        in_specs=[row(d), _const_spec((1, d)), _const_spec(wcat.shape), tab, tab, tab, tab],
        out_specs=[row(p[3]) for p in _IN_PLAN],
        out_shape=[jax.ShapeDtypeStruct((t, p[3]), p[4]) for p in _IN_PLAN],
        compiler_params=pltpu.CompilerParams(dimension_semantics=("arbitrary",),
                                             vmem_limit_bytes=BIG_VMEM_LIMIT),
        name="in_proj",
    )(x2, g.reshape(1, d), wcat, *tables)


def _strip_attention(qm, k_ref, v_ref, kb_lo, kb_hi, bias_fn, last_bias, scratch, exp_scale=None):
    s_ref, m_ref, l_ref, acc_ref = scratch
    tq = qm.shape[0]

    def scores(kb, extra):
        kblk = k_ref[pl.ds(pl.multiple_of(kb * tq, tq), tq), :]
        s = _dot_nt(qm, kblk)
        if bias_fn is not None:
            s = s + bias_fn(kb)
        if extra is not None:
            s = s + extra
        return s

    m_ref[...] = jnp.full(m_ref.shape, NEG_INF, F32)

    def pass1(kb, carry):
        s = scores(kb, None)
        s_ref[kb] = s
        m_ref[...] = jnp.maximum(m_ref[...], s)
        return carry

    lax.fori_loop(kb_lo, kb_hi - 1, pass1, 0)
    s_last = scores(kb_hi - 1, last_bias)
    s_ref[kb_hi - 1] = s_last
    m = jnp.max(jnp.maximum(m_ref[...], s_last), axis=-1, keepdims=True)

    l_ref[...] = jnp.zeros(l_ref.shape, F32)
    acc_ref[...] = jnp.zeros(acc_ref.shape, F32)

    def pass2(kb, carry):
        z = s_ref[kb] - m
        if exp_scale is not None:
            z = z * exp_scale
        p = jnp.exp(z)
        l_ref[...] += p
        vblk = v_ref[pl.ds(pl.multiple_of(kb * tq, tq), tq), :]
        acc_ref[...] += _dot(p.astype(BF16), vblk)
        return carry

    lax.fori_loop(kb_lo, kb_hi, pass2, 0)
    return acc_ref[...] / jnp.sum(l_ref[...], axis=-1, keepdims=True)


def _strip_scratch(tq, s_len):
    return [pltpu.VMEM((s_len // tq, tq, tq), F32), pltpu.VMEM((tq, tq), F32),
            pltpu.VMEM((tq, tq), F32), pltpu.VMEM((tq, BRANCH_WIDTH), F32)]


def _causal_bias(tq):
    r = lax.broadcasted_iota(jnp.int32, (tq, tq), 0)
    c = lax.broadcasted_iota(jnp.int32, (tq, tq), 1)
    return jnp.where(c <= r, 0.0, NEG_INF).astype(F32)


def _att_specs(tq, s_len):
    q_spec = pl.BlockSpec((None, tq, BRANCH_WIDTH), lambda b, i: (b, i, 0))
    kv_spec = pl.BlockSpec((None, s_len, BRANCH_WIDTH), lambda b, i: (b, 0, 0))
    return q_spec, kv_spec


def _diff_kernel(lam_ref, q_ref, k_ref, v_ref, g_ref, hm_ref, o_ref, *scratch, tq, out_scale):
    qi = pl.program_id(1)
    q = q_ref[...]
    lane = lax.broadcasted_iota(jnp.int32, (1, BRANCH_WIDTH), 1)
    causal = _causal_bias(tq)
    lam = lam_ref[0]
    o = jnp.zeros((tq, BRANCH_WIDTH), F32)
    for h in range(N_HEADS):
        parts = []
        for c in range(2):
            lo = h * HEAD_DIM + c * DIFF_QK_DIM
            cmask = jnp.where((lane >= lo) & (lane < lo + DIFF_QK_DIM), 1.0, 0.0).astype(BF16)
            parts.append(_strip_attention(q * cmask, k_ref, v_ref, 0, qi + 1, None, causal, scratch,
                                          exp_scale=DIFF_QK_DIM ** -0.5))
        o = jnp.where(_head_mask(h, F32) > 0.5, parts[0] - lam * parts[1], o)
    ms = _dot_hilo(o * o, hm_ref[...])
    o_ref[...] = o * lax.rsqrt(ms + RMS_EPS) * g_ref[...] * out_scale


def _diff_attention(q, k, v, lam, subln_g, lam_init):
    b, s_len, w = q.shape
    tq = ATT_TILE
    q_spec, kv_spec = _att_specs(tq, s_len)
    head = np.arange(w) // HEAD_DIM
    hm = jnp.asarray((head[:, None] == head[None, :]) / HEAD_DIM, dtype=BF16)
    g = jnp.tile(subln_g.astype(F32), N_HEADS).reshape(1, w)
    return pl.pallas_call(
        functools.partial(_diff_kernel, tq=tq, out_scale=1.0 - lam_init),
        grid=(b, s_len // tq),
        in_specs=[pl.BlockSpec(memory_space=pltpu.SMEM), q_spec, kv_spec, kv_spec,
                  _const_spec((1, w)), _const_spec((w, w))],
        out_specs=q_spec,
        out_shape=jax.ShapeDtypeStruct((b, s_len, w), F32),
        scratch_shapes=_strip_scratch(tq, s_len),
        compiler_params=pltpu.CompilerParams(dimension_semantics=("arbitrary", "arbitrary")),
        name="diff_attention",
    )(lam.reshape(1), q, k, v, g, hm)


def _dil_kernel(q_ref, k_ref, v_ref, bias_ref, o_ref, *scratch, tq):
    qi = pl.program_id(1)
    q = q_ref[...]
    o = jnp.zeros((tq, BRANCH_WIDTH), F32)
    for h in range(N_HEADS):
        oh = _strip_attention(q * _head_mask(h, BF16), k_ref, v_ref, 0, qi + 1,
                              lambda kb: bias_ref[qi - kb], None, scratch)
        o = jnp.where(_head_mask(h, F32) > 0.5, oh, o)
    o_ref[...] = o


def _dilated_bias(tq, s_len):
    nq = s_len // tq
    d = (np.arange(nq)[:, None, None] * tq + np.arange(tq)[None, :, None] - np.arange(tq)[None, None, :])
    count = np.zeros(d.shape, np.float64)
    for window, dil in DIL_PATTERNS:
        count += (d >= 0) & (d <= window) & (d % dil == 0)
    return np.where(count > 0, np.log(np.maximum(count, 1.0)), NEG_INF).astype(np.float32)


def _dilated_attention(q, k, v):
    b, s_len, w = q.shape
    tq = ATT_TILE
    q_spec, kv_spec = _att_specs(tq, s_len)
    bias = jnp.asarray(_dilated_bias(tq, s_len))
    return pl.pallas_call(
        functools.partial(_dil_kernel, tq=tq),
        grid=(b, s_len // tq),
        in_specs=[q_spec, kv_spec, kv_spec, _const_spec(bias.shape)],
        out_specs=q_spec,
        out_shape=jax.ShapeDtypeStruct((b, s_len, w), F32),
        scratch_shapes=_strip_scratch(tq, s_len),
        compiler_params=pltpu.CompilerParams(dimension_semantics=("arbitrary", "arbitrary")),
        name="dilated_attention",
    )(q, k, v, bias)


def _s5_kernel(u_ref, bm_ref, cm_ref, are_ref, aim_ref, d_ref, wg_ref, bg_ref, o_ref, xs_ref, st_ref,
               *, ts, nb):
    n = S5_GROUPS * S5_STATE

    @pl.when(pl.program_id(0) == 0)
    def _():
        st_ref[...] = jnp.zeros(st_ref.shape, F32)

    u = u_ref[...]
    xs_ref[...] = _dot(u.astype(BF16), bm_ref[...])
    a_re = jnp.broadcast_to(are_ref[...], (nb, n))
    a_im = jnp.broadcast_to(aim_ref[...], (nb, n))

    def step(t, carry):
        x_re, x_im = carry
        r0 = pl.multiple_of(t * nb, nb)
        n_re = a_re * x_re - a_im * x_im + xs_ref[pl.ds(r0, nb), 0:n]
        n_im = a_re * x_im + a_im * x_re + xs_ref[pl.ds(r0, nb), n:2 * n]
        xs_ref[pl.ds(r0, nb), 0:n] = n_re
        xs_ref[pl.ds(r0, nb), n:2 * n] = n_im
        return n_re, n_im

    x_re, x_im = lax.fori_loop(0, ts, step, (st_ref[:, 0:n], st_ref[:, n:2 * n]))
    st_ref[:, 0:n] = x_re
    st_ref[:, n:2 * n] = x_im

    y = _dot(xs_ref[...].astype(BF16), cm_ref[...]) + d_ref[...] * u
    t = _dot(jax.nn.gelu(y).astype(BF16), wg_ref[...]) + bg_ref[...]
    o_ref[...] = t[:, :BRANCH_WIDTH] * jax.nn.sigmoid(t[:, BRANCH_WIDTH:])


def _s5_params(lam_re, lam_im, log_dt, b_re, b_im, c_re, c_im):
    g, n, p = S5_GROUPS, S5_STATE, S5_GROUP
    lr, li = lam_re.astype(F32), lam_im.astype(F32)
    dt = jnp.exp(log_dt.astype(F32))[:, None]
    mag = jnp.exp(lr * dt)
    a_re, a_im = mag * jnp.cos(li * dt), mag * jnp.sin(li * dt)
    den = lr * lr + li * li
    n_re, n_im = a_re - 1.0, a_im
    z_re = (n_re * lr + n_im * li) / den
    z_im = (n_im * lr - n_re * li) / den
    br, bi = b_re.astype(F32), b_im.astype(F32)
    bb_re = z_re[..., None] * br - z_im[..., None] * bi
    bb_im = z_re[..., None] * bi + z_im[..., None] * br
    eye = jnp.eye(g, dtype=F32)
    blockdiag_in = lambda t: jnp.einsum("gnp,gh->gphn", t, eye).reshape(g * p, g * n)
    blockdiag_out = lambda t: jnp.einsum("gpn,gh->gnhp", t, eye).reshape(g * n, g * p)
    bm = jnp.concatenate([blockdiag_in(bb_re), blockdiag_in(bb_im)], axis=1)
    cm = jnp.concatenate([blockdiag_out(c_re.astype(F32)), -blockdiag_out(c_im.astype(F32))], axis=0)
    return bm.astype(BF16), cm.astype(BF16), a_re.reshape(1, g * n), a_im.reshape(1, g * n)


def _s5_branch(u_tb, nb, params, d_skip, w_glu, b_glu):
    rows, w = u_tb.shape
    ts = S5_CHUNK
    bm, cm, a_re, a_im = params
    n2 = bm.shape[1]
    blk = pl.BlockSpec((ts * nb, w), lambda i: (i, 0))
    return pl.pallas_call(
        functools.partial(_s5_kernel, ts=ts, nb=nb),
        grid=(rows // (ts * nb),),
        in_specs=[blk, _const_spec(bm.shape), _const_spec(cm.shape), _const_spec(a_re.shape),
                  _const_spec(a_im.shape), _const_spec((1, w)), _const_spec(w_glu.shape),
                  _const_spec((1, 2 * w))],
        out_specs=blk,
        out_shape=jax.ShapeDtypeStruct((rows, w), F32),
        scratch_shapes=[pltpu.VMEM((ts * nb, n2), F32), pltpu.VMEM((nb, n2), F32)],
        compiler_params=pltpu.CompilerParams(dimension_semantics=("arbitrary",),
                                             vmem_limit_bytes=BIG_VMEM_LIMIT),
        name="s5_scan",
    )(u_tb, bm, cm, a_re, a_im, d_skip.astype(F32).reshape(1, w), w_glu.astype(BF16),
      b_glu.astype(F32).reshape(1, 2 * w))


def _compress_kernel(r_ref, pe_top_ref, pe_bot_ref, w_top_ref, w_bot_ref, w2_ref, k_ref, v_ref):
    r = r_ref[...]
    top = (r + pe_top_ref[...]).astype(BF16)
    nxt = pltpu.roll(r, r.shape[0] - 1, 0)
    bot = (nxt + pe_bot_ref[...]).astype(BF16)
    hid = jax.nn.gelu(_dot(top, w_top_ref[...]) + _dot(bot, w_bot_ref[...]))
    kv = _dot(hid.astype(BF16), w2_ref[...])
    k_ref[...] = kv[:, :BRANCH_WIDTH].astype(BF16)
    v_ref[...] = kv[:, BRANCH_WIDTH:].astype(BF16)


def _compress(kvc, pe, w1, w2):
    b, s_len, _ = kvc.shape
    nr = s_len // CMP_STRIDE
    per = CMP_BLOCK // CMP_STRIDE
    assert per == 2
    hid = w1.shape[-1]
    r = kvc.reshape(b, nr, CMP_STRIDE * 2 * HEAD_DIM)
    w1r = w1.astype(F32).reshape(2, per, CMP_STRIDE, HEAD_DIM, hid)
    per_r = pe.astype(F32).reshape(2, per, CMP_STRIDE, HEAD_DIM)

    def expand(j):
        wk = jnp.pad(w1r[0, j], ((0, 0), (0, HEAD_DIM), (0, hid)))
        wv = jnp.pad(w1r[1, j], ((0, 0), (HEAD_DIM, 0), (hid, 0)))
        return (wk + wv).reshape(CMP_STRIDE * 2 * HEAD_DIM, 2 * hid).astype(BF16)

    pe_rows = [jnp.concatenate([per_r[0, j], per_r[1, j]], axis=-1).reshape(1, -1) for j in range(per)]
    z = jnp.zeros((hid, BRANCH_WIDTH), F32)
    w2f = w2.astype(F32)
    w2x = jnp.concatenate([jnp.concatenate([jnp.tile(w2f[0], (1, N_HEADS)), z], axis=1),
                           jnp.concatenate([z, jnp.tile(w2f[1], (1, N_HEADS))], axis=1)], axis=0).astype(BF16)
    w_top, w_bot = expand(0), expand(1)
    blk = pl.BlockSpec((None, nr, r.shape[-1]), lambda i: (i, 0, 0))
    out = pl.BlockSpec((None, nr, BRANCH_WIDTH), lambda i: (i, 0, 0))
    return pl.pallas_call(
        _compress_kernel,
        grid=(b,),
        in_specs=[blk, _const_spec(pe_rows[0].shape), _const_spec(pe_rows[1].shape),
                  _const_spec(w_top.shape), _const_spec(w_bot.shape), _const_spec(w2x.shape)],
        out_specs=[out, out],
        out_shape=[jax.ShapeDtypeStruct((b, nr, BRANCH_WIDTH), BF16)] * 2,
        compiler_params=pltpu.CompilerParams(dimension_semantics=("arbitrary",)),
        name="nsa_compress",
    )(r, pe_rows[0], pe_rows[1], w_top, w_bot, w2x)


def _nsa_kernel(dq_ref, dqr_ref, dg_ref, kc_ref, vc_ref, ks_ref, vs_ref, kw_ref, vw_ref,
                ovt_ref, exp_ref, gsel_ref, wbias_ref, o_ref, *scratch, tq, n_cmp, n_slc, n_sel):
    qi = pl.program_id(1)
    t0 = qi * tq
    nr = kc_ref.shape[0]

    dq = dq_ref[...]
    tpos = t0 + lax.broadcasted_iota(jnp.int32, (tq, nr), 0)
    ci = lax.broadcasted_iota(jnp.int32, (tq, nr), 1)
    cmask = (ci * CMP_STRIDE + (CMP_BLOCK - 1) <= tpos) & (ci < n_cmp)
    kc = kc_ref[...]
    vc = vc_ref[...]
    o_cmp = jnp.zeros((tq, BRANCH_WIDTH), F32)
    p_sum = jnp.zeros((tq, nr), F32)
    for h in range(N_HEADS):
        s = jnp.where(cmask, _dot_nt(dq * _head_mask(h, BF16), kc), NEG_INF)
        m = jnp.max(s, axis=-1, keepdims=True)
        e = jnp.where(cmask, jnp.exp(s - m), 0.0)
        p = e / jnp.maximum(jnp.sum(e, axis=-1, keepdims=True), 1e-30)
        p_sum = p_sum + p
        o_cmp = jnp.where(_head_mask(h, F32) > 0.5, _dot(p.astype(BF16), vc), o_cmp)

    p_hi, p_lo = _split_hi_lo(p_sum)
    ovt = ovt_ref[...]
    imp_t = _dot_nt(ovt, p_hi) + _dot_nt(ovt, p_lo)
    blk = lax.broadcasted_iota(jnp.int32, (n_slc, tq), 0)
    qblk = lax.shift_right_arithmetic(t0 + lax.broadcasted_iota(jnp.int32, (n_slc, tq), 1),
                                      int(math.log2(SLC_BLOCK)))
    forced = jnp.where(blk == 0, 1, jnp.where(blk == qblk, 1, jnp.where(blk == qblk - 1, 1, 0)))
    score = jnp.where(blk <= qblk, jnp.where(forced > 0, FORCE_SCORE, imp_t), NEG_INF)
    rank = jnp.zeros((n_slc, tq), F32)
    for i in range(n_slc):
        si = score[i:i + 1, :]
        tie = jnp.where(blk > i, 1.0, 0.0)
        rank = rank + jnp.where(si > score, 1.0, jnp.where(si == score, tie, 0.0))
    sel_bias_t = jnp.where(rank < n_sel, jnp.where(score > 0.5 * NEG_INF, 0.0, NEG_INF), NEG_INF)
    pad_rows = exp_ref.shape[1] - n_slc
    sel_bias = jnp.concatenate([sel_bias_t, jnp.full((pad_rows, tq), NEG_INF, F32)], axis=0).T.astype(BF16)

    causal = _causal_bias(tq)
    dqr = dqr_ref[...]
    o_slc = jnp.zeros((tq, BRANCH_WIDTH), F32)
    o_win = jnp.zeros((tq, BRANCH_WIDTH), F32)
    win_lo = jnp.maximum(qi - (wbias_ref.shape[0] - 1), 0)
    for h in range(N_HEADS):
        qm = dqr * _head_mask(h, BF16)
        keep = _head_mask(h, F32) > 0.5
        oh = _strip_attention(qm, ks_ref, vs_ref, 0, qi + 1, lambda kb: _dot(sel_bias, exp_ref[kb]),
                              causal, scratch)
        o_slc = jnp.where(keep, oh, o_slc)
        oh = _strip_attention(qm, kw_ref, vw_ref, win_lo, qi + 1, lambda kb: wbias_ref[qi - kb],
                              None, scratch)
        o_win = jnp.where(keep, oh, o_win)

    gates = _dot_hilo(jax.nn.sigmoid(dg_ref[...]), gsel_ref[...])
    w = BRANCH_WIDTH
    o_ref[...] = gates[:, :w] * o_cmp + gates[:, w:2 * w] * o_slc + gates[:, 2 * w:] * o_win


def _nsa_constants(tq, s_len):
    n_cmp = (s_len - CMP_BLOCK) // CMP_STRIDE + 1
    n_slc = s_len // SLC_BLOCK
    nr = s_len // CMP_STRIDE
    c0 = np.arange(n_cmp)[:, None] * CMP_STRIDE
    s0 = np.arange(n_slc)[None, :] * SLC_BLOCK
    overlap = np.clip(np.minimum(c0 + CMP_BLOCK, s0 + SLC_BLOCK) - np.maximum(c0, s0), 0, None) / CMP_STRIDE
    ovt = np.zeros((n_slc, nr), np.float32)
    ovt[:, :n_cmp] = overlap.T
    rows = -(-n_slc // 128) * 128
    expand = np.zeros((s_len // tq, rows, tq), np.float32)
    tok = np.arange(s_len)
    expand[tok // tq, tok // SLC_BLOCK, tok % tq] = 1.0
    gsel = np.zeros((128, 3 * BRANCH_WIDTH), np.float32)
    for h in range(N_HEADS):
        for j in range(3):
            gsel[h * 3 + j, j * BRANCH_WIDTH + h * HEAD_DIM:j * BRANCH_WIDTH + (h + 1) * HEAD_DIM] = 1.0
    nwin = -(-WINDOW // tq) + 1
    d = np.arange(nwin)[:, None, None] * tq + np.arange(tq)[None, :, None] - np.arange(tq)[None, None, :]
    wbias = np.where((d >= 0) & (d < WINDOW), 0.0, NEG_INF).astype(np.float32)
    return n_cmp, n_slc, ovt, expand, gsel, wbias


def _nsa_attention(dq, dqr, dg, k_cmp, v_cmp, ksr, vsr, kwr, vwr):
    b, s_len, w = dq.shape
    tq = ATT_TILE
    n_cmp, n_slc, ovt, expand, gsel, wbias = _nsa_constants(tq, s_len)
    q_spec, kv_spec = _att_specs(tq, s_len)
    nr = k_cmp.shape[1]
    cmp_spec = pl.BlockSpec((None, nr, w), lambda bi, i: (bi, 0, 0))
    g_spec = pl.BlockSpec((None, tq, dg.shape[-1]), lambda bi, i: (bi, i, 0))
    consts = [jnp.asarray(ovt, BF16), jnp.asarray(expand, BF16), jnp.asarray(gsel, BF16), jnp.asarray(wbias)]
    return pl.pallas_call(
        functools.partial(_nsa_kernel, tq=tq, n_cmp=n_cmp, n_slc=n_slc, n_sel=min(N_SELECT, n_slc)),
        grid=(b, s_len // tq),
        in_specs=[q_spec, q_spec, g_spec, cmp_spec, cmp_spec, kv_spec, kv_spec, kv_spec, kv_spec]
        + [_const_spec(c.shape) for c in consts],
        out_specs=q_spec,
        out_shape=jax.ShapeDtypeStruct((b, s_len, w), F32),
        scratch_shapes=_strip_scratch(tq, s_len),
        compiler_params=pltpu.CompilerParams(dimension_semantics=("arbitrary", "arbitrary"),
                                             vmem_limit_bytes=BIG_VMEM_LIMIT),
        name="nsa_attention",
    )(dq, dqr, dg, k_cmp, v_cmp, ksr, vsr, kwr, vwr, *consts)


def _mem_kv_kernel(m_ref, g_ref, w_ref, k_ref, v_ref):
    kv = _dot(_rms(m_ref[...], g_ref[...]).astype(BF16), w_ref[...])
    k_ref[...] = kv[:, :BRANCH_WIDTH].astype(BF16)
    v_ref[...] = kv[:, BRANCH_WIDTH:].astype(BF16)


def _mem_kv(mem2, g, w_kv):
    rows, d = mem2.shape
    tm = ROW_TILE
    out = pl.BlockSpec((tm, BRANCH_WIDTH), lambda i: (i, 0))
    return pl.pallas_call(
        _mem_kv_kernel,
        grid=(rows // tm,),
        in_specs=[pl.BlockSpec((tm, d), lambda i: (i, 0)), _const_spec((1, d)), _const_spec(w_kv.shape)],
        out_specs=[out, out],
        out_shape=[jax.ShapeDtypeStruct((rows, BRANCH_WIDTH), BF16)] * 2,
        compiler_params=pltpu.CompilerParams(dimension_semantics=("arbitrary",)),
        name="mem_kv",
    )(mem2, g.astype(F32).reshape(1, d), w_kv.astype(BF16))


def _mem_attn_kernel(q_ref, k_ref, v_ref, o_ref):
    q = q_ref[...]
    k = k_ref[...]
    v = v_ref[...]
    o = jnp.zeros(o_ref.shape, F32)
    for h in range(N_HEADS):
        s = _dot_nt(q * _head_mask(h, BF16), k)
        e = jnp.exp(s - jnp.max(s, axis=-1, keepdims=True))
        oh = _dot(e.astype(BF16), v) / jnp.sum(e, axis=-1, keepdims=True)
        o = jnp.where(_head_mask(h, F32) > 0.5, oh, o)
    o_ref[...] = o


def _mem_attention(q, k, v):
    b, s_len, w = q.shape
    tq = ATT_TILE
    q_spec = pl.BlockSpec((None, tq, w), lambda bi, i: (bi, i, 0))
    kv_spec = pl.BlockSpec((None, k.shape[1], w), lambda bi, i: (bi, 0, 0))
    return pl.pallas_call(
        _mem_attn_kernel,
        grid=(b, s_len // tq),
        in_specs=[q_spec, kv_spec, kv_spec],
        out_specs=q_spec,
        out_shape=jax.ShapeDtypeStruct((b, s_len, w), F32),
        compiler_params=pltpu.CompilerParams(dimension_semantics=("arbitrary", "arbitrary")),
        name="mem_attention",
    )(q, k, v)


def _merge_kernel(x_ref, g_ref, oa_ref, ob_ref, oc_ref, od_ref, oe_ref, zs_ref, wm_ref, bm_ref, wb_ref,
                  wo_ref, fg_ref, o_ref, *, final_norm):
    x = x_ref[...]
    d = x.shape[-1]
    w = BRANCH_WIDTH
    h = _rms(x, g_ref[...]).astype(BF16)
    mixed = jnp.zeros(x.shape, F32)
    for n, br_ref in enumerate((oa_ref, ob_ref, oc_ref, od_ref, oe_ref)):
        br = (br_ref[...] * zs_ref[:, n * w:(n + 1) * w]).astype(BF16)
        y = _dot(br, wb_ref[n])
        gate = jax.nn.sigmoid(_dot(h, wm_ref[:, n * d:(n + 1) * d]) + bm_ref[:, n * d:(n + 1) * d])
        mixed = mixed + gate * y
    out = x + _dot(mixed.astype(BF16), wo_ref[...])
    if final_norm:
        out = _rms(out, fg_ref[...])
    o_ref[...] = out


def _merge(x2, g, branches, zs, w_merge, b_merge, w_branch, w_out, final_g, final_norm):
    t, d = x2.shape
    tm = ROW_TILE
    row = lambda wd: pl.BlockSpec((tm, wd), lambda i: (i, 0))
    return pl.pallas_call(
        functools.partial(_merge_kernel, final_norm=final_norm),
        grid=(t // tm,),
        in_specs=[row(d), _const_spec((1, d))] + [row(BRANCH_WIDTH)] * N_BRANCHES + [row(zs.shape[1])]
        + [_const_spec(w_merge.shape), _const_spec((1, N_BRANCHES * d)), _const_spec(w_branch.shape),
           _const_spec(w_out.shape), _const_spec((1, d))],
        out_specs=row(d),
        out_shape=jax.ShapeDtypeStruct((t, d), F32),
        compiler_params=pltpu.CompilerParams(dimension_semantics=("arbitrary",),
                                             vmem_limit_bytes=BIG_VMEM_LIMIT),
        name="merge",
    )(x2, g.astype(F32).reshape(1, d), *branches, zs, w_merge.astype(BF16),
      b_merge.astype(F32).reshape(1, -1), w_branch.astype(BF16), w_out.astype(BF16),
      final_g.astype(F32).reshape(1, d))


def kernel(x, mem, norm_g, w_in, diff_lambda, diff_subln_g, s5_lambda_re, s5_lambda_im, s5_log_dt,
           s5_b_re, s5_b_im, s5_c_re, s5_c_im, s5_d, w_glu, b_glu, nsa_pe, nsa_w1, nsa_w2, mem_norm_g,
           w_mem_kv, w_merge, b_merge, w_branch, w_out, final_g):
    bsz, s_len, d = x.shape
    depth = w_in.shape[0]
    t = bsz * s_len
    w = BRANCH_WIDTH
    tables = _rope_tables(s_len, DIFF_QK_DIM) + _rope_tables(s_len, HEAD_DIM)
    x2 = x.astype(F32).reshape(t, d)
    mem2 = mem.astype(F32).reshape(-1, d)
    for l in range(depth):
        proj = dict(zip([p[0] for p in _IN_PLAN],
                        _in_proj(x2, norm_g[l].astype(F32), _in_weights(w_in[l].astype(F32)), tables, s_len)))
        seq = lambda name: proj[name].reshape(bsz, s_len, -1)

        dl = diff_lambda[l].astype(F32)
        lam_init = 0.8 - 0.6 * math.exp(-0.3 * l)
        lam = jnp.exp(jnp.sum(dl[0] * dl[1])) - jnp.exp(jnp.sum(dl[2] * dl[3])) + lam_init
        o_a = _diff_attention(seq("qa"), seq("ka"), seq("va"), lam, diff_subln_g[l], lam_init)

        o_b = _dilated_attention(seq("qb"), seq("kb"), seq("vb"))

        u_tb = jnp.swapaxes(seq("cu"), 0, 1).reshape(t, w)
        s5p = _s5_params(s5_lambda_re[l], s5_lambda_im[l], s5_log_dt[l], s5_b_re[l], s5_b_im[l],
                         s5_c_re[l], s5_c_im[l])
        o_c = _s5_branch(u_tb, bsz, s5p, s5_d[l], w_glu[l], b_glu[l])
        o_c = jnp.swapaxes(o_c.reshape(s_len, bsz, w), 0, 1)

        k_cmp, v_cmp = _compress(seq("kvc"), nsa_pe[l], nsa_w1[l], nsa_w2[l])
        o_d = _nsa_attention(seq("dq"), seq("dqr"), seq("dg"), k_cmp, v_cmp,
                             seq("ksr"), seq("vsr"), seq("kwr"), seq("vwr"))

        k_mem, v_mem = _mem_kv(mem2, mem_norm_g[l], w_mem_kv[l])
        o_e = _mem_attention(seq("eq"), k_mem.reshape(bsz, -1, w), v_mem.reshape(bsz, -1, w))

        branches = [o.reshape(t, w) for o in (o_a, o_b, o_c, o_d, o_e)]
        x2 = _merge(x2, norm_g[l], branches, proj["zs"], w_merge[l], b_merge[l], w_branch[l], w_out[l],
                    final_g, final_norm=(l == depth - 1))
    return x2.reshape(bsz, s_len, d).astype(x.dtype)
```

```python
import functools
import math

import numpy as np
import jax
import jax.numpy as jnp
from jax import lax
from jax.experimental import pallas as pl
from jax.experimental.pallas import tpu as pltpu

F32 = jnp.float32
BF16 = jnp.bfloat16

HEAD_DIM = 64
BRANCH_WIDTH = 256
N_HEADS = 4
N_BRANCHES = 5
DIFF_QK_DIM = 32
DIL_PATTERNS = ((128, 1), (512, 4), (2048, 16))
S5_GROUP = 16
S5_GROUPS = 16
S5_STATE = 64
CMP_BLOCK = 32
CMP_STRIDE = 16
SLC_BLOCK = 64
N_SELECT = 16
WINDOW = 512
ROPE_THETA = 10000.0
RMS_EPS = 1e-6
NEG_INF = -1e30
FORCE_SCORE = 1e9

V7X_VMEM_BYTES = 64 * 1024 * 1024
BIG_VMEM_LIMIT = V7X_VMEM_BYTES - 8 * 1024 * 1024

LANES = 128
ATT_TILE = 256
ROW_TILE = 256
S5_CHUNK = 128

_NT = (((1,), (1,)), ((), ()))


def _rms(x, g):
    return x * lax.rsqrt(jnp.mean(x * x, axis=-1, keepdims=True) + RMS_EPS) * g


def _dot(a, b):
    return jnp.dot(a, b, preferred_element_type=F32)


def _dot_nt(a, b):
    return lax.dot_general(a, b, _NT, preferred_element_type=F32)


def _split_hi_lo(x):
    hi = x.astype(BF16)
    lo = (x - hi.astype(F32)).astype(BF16)
    return hi, lo


def _dot_hilo(x, w):
    hi, lo = _split_hi_lo(x)
    return _dot(hi, w) + _dot(lo, w)


def _const_spec(shape):
    n = len(shape)
    return pl.BlockSpec(shape, lambda *_: (0,) * n, pipeline_mode=pl.Buffered(1))


def _head_mask(h, dtype):
    lane = lax.broadcasted_iota(jnp.int32, (1, BRANCH_WIDTH), 1)
    return jnp.where((lane >= h * HEAD_DIM) & (lane < (h + 1) * HEAD_DIM), 1.0, 0.0).astype(dtype)


_IN_PLAN = (
    ("qa", "rope", "A", 256, BF16), ("ka", "rope", "A", 256, BF16), ("va", "plain", None, 256, BF16),
    ("qb", "rope", "B", 256, BF16), ("kb", "rope", "B", 256, BF16), ("vb", "plain", None, 256, BF16),
    ("cu", "plain", None, 256, F32),
    ("dq", "plain", None, 256, BF16), ("dqr", "rope", "B", 256, BF16),
    ("kvc", "plain", None, 128, F32),
    ("ksr", "rope", "B", 256, BF16), ("vsr", "plain", None, 256, BF16),
    ("kwr", "rope", "B", 256, BF16), ("vwr", "plain", None, 256, BF16),
    ("dg", "plain", None, 128, F32),
    ("eq", "plain", None, 256, BF16),
    ("zs", "silu", None, 1280, F32),
)


def _in_proj_kernel(x_ref, g_ref, w_ref, cosa_ref, sina_ref, cosb_ref, sinb_ref, *out_refs):
    h = _rms(x_ref[...], g_ref[...]).astype(BF16)
    col = 0
    for (_, kind, table, width, _), o_ref in zip(_IN_PLAN, out_refs):
        if kind == "rope":
            y = _dot(h, w_ref[:, col:col + 2 * width])
            cos_ref, sin_ref = (cosa_ref, sina_ref) if table == "A" else (cosb_ref, sinb_ref)
            y = y[:, :width] * cos_ref[...] + y[:, width:] * sin_ref[...]
            col += 2 * width
        else:
            y = _dot(h, w_ref[:, col:col + width])
            if kind == "silu":
                y = y * jax.nn.sigmoid(y)
            col += width
        o_ref[...] = y.astype(o_ref.dtype)


def _rot_cols(w, group):
    d, n = w.shape
    w4 = w.reshape(d, n // group, 2, group // 2)
    return jnp.concatenate([-w4[:, :, 1], w4[:, :, 0]], axis=2).reshape(d, n)


def _in_weights(w):
    sizes = (256,) * 10 + (256, 64, 64, 64, 64, 64, 64, 12, 256, 256, 256)
    offs = np.cumsum((0,) + sizes)
    (a_q, a_k, a_v, a_z, b_q, b_k, b_v, b_z, c_u, c_z, d_q, d_kc, d_vc, d_ks, d_vs, d_kw, d_vw,
     d_g, d_z, e_q, e_z) = [w[:, offs[i]:offs[i + 1]] for i in range(len(sizes))]
    scale = HEAD_DIM ** -0.5
    rep = lambda t: jnp.tile(t, (1, N_HEADS))
    rope = lambda t, g: [t, _rot_cols(t, g)]
    pad = lambda t, n: jnp.pad(t, ((0, 0), (0, n - t.shape[1])))
    cols = (rope(a_q, DIFF_QK_DIM) + rope(a_k, DIFF_QK_DIM) + [a_v]
            + rope(b_q * scale, HEAD_DIM) + rope(b_k, HEAD_DIM) + [b_v]
            + [c_u, d_q * scale] + rope(d_q * scale, HEAD_DIM)
            + [jnp.concatenate([d_kc, d_vc], axis=1)]
            + rope(rep(d_ks), HEAD_DIM) + [rep(d_vs)] + rope(rep(d_kw), HEAD_DIM) + [rep(d_vw)]
            + [pad(d_g, 128), e_q * scale]
            + [a_z, b_z, c_z, d_z, e_z])
    return jnp.concatenate(cols, axis=1).astype(BF16)


def _rope_tables(s_len, group):
    half = group // 2
    inv_freq = ROPE_THETA ** (-jnp.arange(half, dtype=F32) / half)
    ang = jnp.arange(s_len, dtype=F32)[:, None] * inv_freq[None, :]
    reps = BRANCH_WIDTH // half
    return jnp.tile(jnp.cos(ang), (1, reps)), jnp.tile(jnp.sin(ang), (1, reps))


def _in_proj(x2, g, wcat, tables, s_len):
    t, d = x2.shape
    tm = ROW_TILE
    nsb = s_len // tm
    row = lambda w: pl.BlockSpec((tm, w), lambda i: (i, 0))
    tab = pl.BlockSpec((tm, BRANCH_WIDTH), lambda i: (i % nsb, 0))
    return pl.pallas_call(
        _in_proj_kernel,
        grid=(t // tm,),
        in_specs=[row(d), _const_spec((1, d)), _const_spec(wcat.shape), tab, tab, tab, tab],
        out_specs=[row(p[3]) for p in _IN_PLAN],
        out_shape=[jax.ShapeDtypeStruct((t, p[3]), p[4]) for p in _IN_PLAN],
        compiler_params=pltpu.CompilerParams(dimension_semantics=("arbitrary",),
                                             vmem_limit_bytes=BIG_VMEM_LIMIT),
        name="in_proj",
    )(x2, g.reshape(1, d), wcat, *tables)


def _lane_fold(x, op):
    parts = [x[:, j * LANES:(j + 1) * LANES] for j in range(x.shape[1] // LANES)]
    return functools.reduce(op, parts)


def _strip_attention(qms, k_ref, v_ref, kb_lo, kb_hi, bias_fn, last_bias, scratch, exp_scale=1.0):
    s_ref, m_ref, l_ref, acc_ref = scratch
    n = len(qms)
    tq = qms[0].shape[0]
    reps = tq // LANES

    def scores(kb, extra):
        kblk = k_ref[pl.ds(pl.multiple_of(kb * tq, tq), tq), :]
        bias = bias_fn(kb) if bias_fn is not None else None
        if extra is not None:
            bias = extra if bias is None else bias + extra
        out = []
        for qm in qms:
            s = _dot_nt(qm, kblk)
            out.append(s if bias is None else s + bias)
        return out

    m_ref[0:n] = jnp.full((n,) + m_ref.shape[1:], NEG_INF, F32)

    def pass1(kb, carry):
        for i, s in enumerate(scores(kb, None)):
            s_ref[i, kb] = s
            m_ref[i] = jnp.maximum(m_ref[i], _lane_fold(s, jnp.maximum))
        return carry

    lax.fori_loop(kb_lo, kb_hi - 1, pass1, 0)
    for i, s in enumerate(scores(kb_hi - 1, last_bias)):
        s_ref[i, kb_hi - 1] = s
        m = jnp.max(jnp.maximum(m_ref[i], _lane_fold(s, jnp.maximum)), axis=-1, keepdims=True)
        m_ref[i] = jnp.broadcast_to(m, m_ref.shape[1:])
        l_ref[i] = jnp.zeros(l_ref.shape[1:], F32)
        acc_ref[i] = jnp.zeros(acc_ref.shape[1:], F32)

    c1 = exp_scale * math.log2(math.e)

    def pass2(kb, carry):
        vblk = v_ref[pl.ds(pl.multiple_of(kb * tq, tq), tq), :]
        for i in range(n):
            m = jnp.concatenate([m_ref[i]] * reps, axis=1)
            p = jnp.exp2((s_ref[i, kb] - m) * c1)
            l_ref[i] += _lane_fold(p, jnp.add)
            acc_ref[i] += _dot(p.astype(BF16), vblk)
        return carry

    lax.fori_loop(kb_lo, kb_hi, pass2, 0)
    return [acc_ref[i] / jnp.sum(l_ref[i], axis=-1, keepdims=True) for i in range(n)]


def _strip_scratch(n, tq, s_len):
    return [pltpu.VMEM((n, s_len // tq, tq, tq), F32), pltpu.VMEM((n, tq, LANES), F32),
            pltpu.VMEM((n, tq, LANES), F32), pltpu.VMEM((n, tq, BRANCH_WIDTH), F32)]


def _causal_bias(tq):
    r = lax.broadcasted_iota(jnp.int32, (tq, tq), 0)
    c = lax.broadcasted_iota(jnp.int32, (tq, tq), 1)
    return jnp.where(c <= r, 0.0, NEG_INF).astype(F32)


def _att_specs(tq, s_len):
    q_spec = pl.BlockSpec((None, tq, BRANCH_WIDTH), lambda b, i: (b, i, 0))
    kv_spec = pl.BlockSpec((None, s_len, BRANCH_WIDTH), lambda b, i: (b, 0, 0))
    return q_spec, kv_spec


def _diff_kernel(lam_ref, q_ref, k_ref, v_ref, g_ref, hm_ref, o_ref, *scratch, tq, out_scale):
    qi = pl.program_id(1)
    q = q_ref[...]
    lane = lax.broadcasted_iota(jnp.int32, (1, BRANCH_WIDTH), 1)
    causal = _causal_bias(tq)
    lam = lam_ref[0]
    qms = []
    for hc in range(2 * N_HEADS):
        lo = hc * DIFF_QK_DIM
        cmask = jnp.where((lane >= lo) & (lane < lo + DIFF_QK_DIM), 1.0, 0.0).astype(BF16)
        qms.append(q * cmask)
    parts = _strip_attention(qms, k_ref, v_ref, 0, qi + 1, None, causal, scratch,
                             exp_scale=DIFF_QK_DIM ** -0.5)
    o = jnp.zeros((tq, BRANCH_WIDTH), F32)
    for h in range(N_HEADS):
        o = jnp.where(_head_mask(h, F32) > 0.5, parts[2 * h] - lam * parts[2 * h + 1], o)
    ms = _dot_hilo(o * o, hm_ref[...])
    o_ref[...] = o * lax.rsqrt(ms + RMS_EPS) * g_ref[...] * out_scale


def _diff_attention(q, k, v, lam, subln_g, lam_init):
    b, s_len, w = q.shape
    tq = ATT_TILE
    q_spec, kv_spec = _att_specs(tq, s_len)
    head = np.arange(w) // HEAD_DIM
    hm = jnp.asarray((head[:, None] == head[None, :]) / HEAD_DIM, dtype=BF16)
    g = jnp.tile(subln_g.astype(F32), N_HEADS).reshape(1, w)
    return pl.pallas_call(
        functools.partial(_diff_kernel, tq=tq, out_scale=1.0 - lam_init),
        grid=(b, s_len // tq),
        in_specs=[pl.BlockSpec(memory_space=pltpu.SMEM), q_spec, kv_spec, kv_spec,
                  _const_spec((1, w)), _const_spec((w, w))],
        out_specs=q_spec,
        out_shape=jax.ShapeDtypeStruct((b, s_len, w), F32),
        scratch_shapes=_strip_scratch(2 * N_HEADS, tq, s_len),
        compiler_params=pltpu.CompilerParams(dimension_semantics=("arbitrary", "arbitrary"),
                                             vmem_limit_bytes=BIG_VMEM_LIMIT),
        name="diff_attention",
    )(lam.reshape(1), q, k, v, g, hm)


def _dil_kernel(q_ref, k_ref, v_ref, bias_ref, o_ref, *scratch, tq):
    qi = pl.program_id(1)
    q = q_ref[...]
    outs = _strip_attention([q * _head_mask(h, BF16) for h in range(N_HEADS)], k_ref, v_ref, 0, qi + 1,
                            lambda kb: bias_ref[qi - kb], None, scratch)
    o = jnp.zeros((tq, BRANCH_WIDTH), F32)
    for h in range(N_HEADS):
        o = jnp.where(_head_mask(h, F32) > 0.5, outs[h], o)
    o_ref[...] = o


def _dilated_bias(tq, s_len):
    nq = s_len // tq
    d = (np.arange(nq)[:, None, None] * tq + np.arange(tq)[None, :, None] - np.arange(tq)[None, None, :])
    count = np.zeros(d.shape, np.float64)
    for window, dil in DIL_PATTERNS:
        count += (d >= 0) & (d <= window) & (d % dil == 0)
    return np.where(count > 0, np.log(np.maximum(count, 1.0)), NEG_INF).astype(np.float32)


def _dilated_attention(q, k, v):
    b, s_len, w = q.shape
    tq = ATT_TILE
    q_spec, kv_spec = _att_specs(tq, s_len)
    bias = jnp.asarray(_dilated_bias(tq, s_len))
    return pl.pallas_call(
        functools.partial(_dil_kernel, tq=tq),
        grid=(b, s_len // tq),
        in_specs=[q_spec, kv_spec, kv_spec, _const_spec(bias.shape)],
        out_specs=q_spec,
        out_shape=jax.ShapeDtypeStruct((b, s_len, w), F32),
        scratch_shapes=_strip_scratch(N_HEADS, tq, s_len),
        compiler_params=pltpu.CompilerParams(dimension_semantics=("arbitrary", "arbitrary"),
                                             vmem_limit_bytes=BIG_VMEM_LIMIT),
        name="dilated_attention",
    )(q, k, v, bias)


def _s5_kernel(u_ref, bm_ref, cm_ref, are_ref, aim_ref, d_ref, wg_ref, bg_ref, o_ref, xs_ref, st_ref,
               *, ts, nb):
    n = S5_GROUPS * S5_STATE

    @pl.when(pl.program_id(0) == 0)
    def _():
        st_ref[...] = jnp.zeros(st_ref.shape, F32)

    u = u_ref[...]
    xs_ref[...] = _dot(u.astype(BF16), bm_ref[...])
    a_re = jnp.broadcast_to(are_ref[...], (nb, n))
    a_im = jnp.broadcast_to(aim_ref[...], (nb, n))

    def step(t, carry):
        x_re, x_im = carry
        r0 = pl.multiple_of(t * nb, nb)
        n_re = a_re * x_re - a_im * x_im + xs_ref[pl.ds(r0, nb), 0:n]
        n_im = a_re * x_im + a_im * x_re + xs_ref[pl.ds(r0, nb), n:2 * n]
        xs_ref[pl.ds(r0, nb), 0:n] = n_re
        xs_ref[pl.ds(r0, nb), n:2 * n] = n_im
        return n_re, n_im

    x_re, x_im = lax.fori_loop(0, ts, step, (st_ref[:, 0:n], st_ref[:, n:2 * n]))
    st_ref[:, 0:n] = x_re
    st_ref[:, n:2 * n] = x_im

    y = _dot(xs_ref[...].astype(BF16), cm_ref[...]) + d_ref[...] * u
    t = _dot(jax.nn.gelu(y).astype(BF16), wg_ref[...]) + bg_ref[...]
    o_ref[...] = t[:, :BRANCH_WIDTH] * jax.nn.sigmoid(t[:, BRANCH_WIDTH:])


def _s5_params(lam_re, lam_im, log_dt, b_re, b_im, c_re, c_im):
    g, n, p = S5_GROUPS, S5_STATE, S5_GROUP
    lr, li = lam_re.astype(F32), lam_im.astype(F32)
    dt = jnp.exp(log_dt.astype(F32))[:, None]
    mag = jnp.exp(lr * dt)
    a_re, a_im = mag * jnp.cos(li * dt), mag * jnp.sin(li * dt)
    den = lr * lr + li * li
    n_re, n_im = a_re - 1.0, a_im
    z_re = (n_re * lr + n_im * li) / den
    z_im = (n_im * lr - n_re * li) / den
    br, bi = b_re.astype(F32), b_im.astype(F32)
    bb_re = z_re[..., None] * br - z_im[..., None] * bi
    bb_im = z_re[..., None] * bi + z_im[..., None] * br
    eye = jnp.eye(g, dtype=F32)
    blockdiag_in = lambda t: jnp.einsum("gnp,gh->gphn", t, eye).reshape(g * p, g * n)
    blockdiag_out = lambda t: jnp.einsum("gpn,gh->gnhp", t, eye).reshape(g * n, g * p)
    bm = jnp.concatenate([blockdiag_in(bb_re), blockdiag_in(bb_im)], axis=1)
    cm = jnp.concatenate([blockdiag_out(c_re.astype(F32)), -blockdiag_out(c_im.astype(F32))], axis=0)
    return bm.astype(BF16), cm.astype(BF16), a_re.reshape(1, g * n), a_im.reshape(1, g * n)


def _s5_branch(u_tb, nb, params, d_skip, w_glu, b_glu):
    rows, w = u_tb.shape
    ts = S5_CHUNK
    bm, cm, a_re, a_im = params
    n2 = bm.shape[1]
    blk = pl.BlockSpec((ts * nb, w), lambda i: (i, 0))
    return pl.pallas_call(
        functools.partial(_s5_kernel, ts=ts, nb=nb),
        grid=(rows // (ts * nb),),
        in_specs=[blk, _const_spec(bm.shape), _const_spec(cm.shape), _const_spec(a_re.shape),
                  _const_spec(a_im.shape), _const_spec((1, w)), _const_spec(w_glu.shape),
                  _const_spec((1, 2 * w))],
        out_specs=blk,
        out_shape=jax.ShapeDtypeStruct((rows, w), F32),
        scratch_shapes=[pltpu.VMEM((ts * nb, n2), F32), pltpu.VMEM((nb, n2), F32)],
        compiler_params=pltpu.CompilerParams(dimension_semantics=("arbitrary",),
                                             vmem_limit_bytes=BIG_VMEM_LIMIT),
        name="s5_scan",
    )(u_tb, bm, cm, a_re, a_im, d_skip.astype(F32).reshape(1, w), w_glu.astype(BF16),
      b_glu.astype(F32).reshape(1, 2 * w))


def _compress_kernel(r_ref, pe_top_ref, pe_bot_ref, w_top_ref, w_bot_ref, w2_ref, k_ref, v_ref):
    r = r_ref[...]
    top = (r + pe_top_ref[...]).astype(BF16)
    nxt = pltpu.roll(r, r.shape[0] - 1, 0)
    bot = (nxt + pe_bot_ref[...]).astype(BF16)
    hid = jax.nn.gelu(_dot(top, w_top_ref[...]) + _dot(bot, w_bot_ref[...]))
    kv = _dot(hid.astype(BF16), w2_ref[...])
    k_ref[...] = kv[:, :BRANCH_WIDTH].astype(BF16)
    v_ref[...] = kv[:, BRANCH_WIDTH:].astype(BF16)


def _compress(kvc, pe, w1, w2):
    b, s_len, _ = kvc.shape
    nr = s_len // CMP_STRIDE
    per = CMP_BLOCK // CMP_STRIDE
    assert per == 2
    hid = w1.shape[-1]
    r = kvc.reshape(b, nr, CMP_STRIDE * 2 * HEAD_DIM)
    w1r = w1.astype(F32).reshape(2, per, CMP_STRIDE, HEAD_DIM, hid)
    per_r = pe.astype(F32).reshape(2, per, CMP_STRIDE, HEAD_DIM)

    def expand(j):
        wk = jnp.pad(w1r[0, j], ((0, 0), (0, HEAD_DIM), (0, hid)))
        wv = jnp.pad(w1r[1, j], ((0, 0), (HEAD_DIM, 0), (hid, 0)))
        return (wk + wv).reshape(CMP_STRIDE * 2 * HEAD_DIM, 2 * hid).astype(BF16)

    pe_rows = [jnp.concatenate([per_r[0, j], per_r[1, j]], axis=-1).reshape(1, -1) for j in range(per)]
    z = jnp.zeros((hid, BRANCH_WIDTH), F32)
    w2f = w2.astype(F32)
    w2x = jnp.concatenate([jnp.concatenate([jnp.tile(w2f[0], (1, N_HEADS)), z], axis=1),
                           jnp.concatenate([z, jnp.tile(w2f[1], (1, N_HEADS))], axis=1)], axis=0).astype(BF16)
    w_top, w_bot = expand(0), expand(1)
    blk = pl.BlockSpec((None, nr, r.shape[-1]), lambda i: (i, 0, 0))
    out = pl.BlockSpec((None, nr, BRANCH_WIDTH), lambda i: (i, 0, 0))
    return pl.pallas_call(
        _compress_kernel,
        grid=(b,),
        in_specs=[blk, _const_spec(pe_rows[0].shape), _const_spec(pe_rows[1].shape),
                  _const_spec(w_top.shape), _const_spec(w_bot.shape), _const_spec(w2x.shape)],
        out_specs=[out, out],
        out_shape=[jax.ShapeDtypeStruct((b, nr, BRANCH_WIDTH), BF16)] * 2,
        compiler_params=pltpu.CompilerParams(dimension_semantics=("arbitrary",)),
        name="nsa_compress",
    )(r, pe_rows[0], pe_rows[1], w_top, w_bot, w2x)


def _nsa_kernel(dq_ref, dqr_ref, dg_ref, kc_ref, vc_ref, ks_ref, vs_ref, kw_ref, vw_ref,
                ovt_ref, exp_ref, gsel_ref, wbias_ref, o_ref, *scratch, tq, n_cmp, n_slc, n_sel):
    qi = pl.program_id(1)
    t0 = qi * tq
    nr = kc_ref.shape[0]

    dq = dq_ref[...]
    tpos = t0 + lax.broadcasted_iota(jnp.int32, (tq, nr), 0)
    ci = lax.broadcasted_iota(jnp.int32, (tq, nr), 1)
    cmask = (ci * CMP_STRIDE + (CMP_BLOCK - 1) <= tpos) & (ci < n_cmp)
    kc = kc_ref[...]
    vc = vc_ref[...]
    o_cmp = jnp.zeros((tq, BRANCH_WIDTH), F32)
    p_sum = jnp.zeros((tq, nr), F32)
    for h in range(N_HEADS):
        s = jnp.where(cmask, _dot_nt(dq * _head_mask(h, BF16), kc), NEG_INF)
        m = jnp.max(s, axis=-1, keepdims=True)
        e = jnp.where(cmask, jnp.exp(s - m), 0.0)
        p = e / jnp.maximum(jnp.sum(e, axis=-1, keepdims=True), 1e-30)
        p_sum = p_sum + p
        o_cmp = jnp.where(_head_mask(h, F32) > 0.5, _dot(p.astype(BF16), vc), o_cmp)

    p_hi, p_lo = _split_hi_lo(p_sum)
    ovt = ovt_ref[...]
    imp_t = _dot_nt(ovt, p_hi) + _dot_nt(ovt, p_lo)
    blk = lax.broadcasted_iota(jnp.int32, (n_slc, tq), 0)
    qblk = lax.shift_right_arithmetic(t0 + lax.broadcasted_iota(jnp.int32, (n_slc, tq), 1),
                                      int(math.log2(SLC_BLOCK)))
    forced = jnp.where(blk == 0, 1, jnp.where(blk == qblk, 1, jnp.where(blk == qblk - 1, 1, 0)))
    score = jnp.where(blk <= qblk, jnp.where(forced > 0, FORCE_SCORE, imp_t), NEG_INF)
    rank = jnp.zeros((n_slc, tq), F32)
    for i in range(n_slc):
        si = score[i:i + 1, :]
        tie = jnp.where(blk > i, 1.0, 0.0)
        rank = rank + jnp.where(si > score, 1.0, jnp.where(si == score, tie, 0.0))
    sel_bias_t = jnp.where(rank < n_sel, jnp.where(score > 0.5 * NEG_INF, 0.0, NEG_INF), NEG_INF)
    pad_rows = exp_ref.shape[1] - n_slc
    sel_bias = jnp.concatenate([sel_bias_t, jnp.full((pad_rows, tq), NEG_INF, F32)], axis=0).T.astype(BF16)

    causal = _causal_bias(tq)
    dqr = dqr_ref[...]
    o_slc = jnp.zeros((tq, BRANCH_WIDTH), F32)
    o_win = jnp.zeros((tq, BRANCH_WIDTH), F32)
    win_lo = jnp.maximum(qi - (wbias_ref.shape[0] - 1), 0)
    qms = [dqr * _head_mask(h, BF16) for h in range(N_HEADS)]
    slc = _strip_attention(qms, ks_ref, vs_ref, 0, qi + 1, lambda kb: _dot(sel_bias, exp_ref[kb]),
                           causal, scratch)
    for h in range(N_HEADS):
        o_slc = jnp.where(_head_mask(h, F32) > 0.5, slc[h], o_slc)
    win = _strip_attention(qms, kw_ref, vw_ref, win_lo, qi + 1, lambda kb: wbias_ref[qi - kb],
                           None, scratch)
    for h in range(N_HEADS):
        o_win = jnp.where(_head_mask(h, F32) > 0.5, win[h], o_win)

    gates = _dot_hilo(jax.nn.sigmoid(dg_ref[...]), gsel_ref[...])
    w = BRANCH_WIDTH
    o_ref[...] = gates[:, :w] * o_cmp + gates[:, w:2 * w] * o_slc + gates[:, 2 * w:] * o_win


def _nsa_constants(tq, s_len):
    n_cmp = (s_len - CMP_BLOCK) // CMP_STRIDE + 1
    n_slc = s_len // SLC_BLOCK
    nr = s_len // CMP_STRIDE
    c0 = np.arange(n_cmp)[:, None] * CMP_STRIDE
    s0 = np.arange(n_slc)[None, :] * SLC_BLOCK
    overlap = np.clip(np.minimum(c0 + CMP_BLOCK, s0 + SLC_BLOCK) - np.maximum(c0, s0), 0, None) / CMP_STRIDE
    ovt = np.zeros((n_slc, nr), np.float32)
    ovt[:, :n_cmp] = overlap.T
    rows = -(-n_slc // 128) * 128
    expand = np.zeros((s_len // tq, rows, tq), np.float32)
    tok = np.arange(s_len)
    expand[tok // tq, tok // SLC_BLOCK, tok % tq] = 1.0
    gsel = np.zeros((128, 3 * BRANCH_WIDTH), np.float32)
    for h in range(N_HEADS):
        for j in range(3):
            gsel[h * 3 + j, j * BRANCH_WIDTH + h * HEAD_DIM:j * BRANCH_WIDTH + (h + 1) * HEAD_DIM] = 1.0
    nwin = -(-WINDOW // tq) + 1
    d = np.arange(nwin)[:, None, None] * tq + np.arange(tq)[None, :, None] - np.arange(tq)[None, None, :]
    wbias = np.where((d >= 0) & (d < WINDOW), 0.0, NEG_INF).astype(np.float32)
    return n_cmp, n_slc, ovt, expand, gsel, wbias


def _nsa_attention(dq, dqr, dg, k_cmp, v_cmp, ksr, vsr, kwr, vwr):
    b, s_len, w = dq.shape
    tq = ATT_TILE
    n_cmp, n_slc, ovt, expand, gsel, wbias = _nsa_constants(tq, s_len)
    q_spec, kv_spec = _att_specs(tq, s_len)
    nr = k_cmp.shape[1]
    cmp_spec = pl.BlockSpec((None, nr, w), lambda bi, i: (bi, 0, 0))
    g_spec = pl.BlockSpec((None, tq, dg.shape[-1]), lambda bi, i: (bi, i, 0))
    consts = [jnp.asarray(ovt, BF16), jnp.asarray(expand, BF16), jnp.asarray(gsel, BF16), jnp.asarray(wbias)]
    return pl.pallas_call(
        functools.partial(_nsa_kernel, tq=tq, n_cmp=n_cmp, n_slc=n_slc, n_sel=min(N_SELECT, n_slc)),
        grid=(b, s_len // tq),
        in_specs=[q_spec, q_spec, g_spec, cmp_spec, cmp_spec, kv_spec, kv_spec, kv_spec, kv_spec]
        + [_const_spec(c.shape) for c in consts],
        out_specs=q_spec,
        out_shape=jax.ShapeDtypeStruct((b, s_len, w), F32),
        scratch_shapes=_strip_scratch(N_HEADS, tq, s_len),
        compiler_params=pltpu.CompilerParams(dimension_semantics=("arbitrary", "arbitrary"),
                                             vmem_limit_bytes=BIG_VMEM_LIMIT),
        name="nsa_attention",
    )(dq, dqr, dg, k_cmp, v_cmp, ksr, vsr, kwr, vwr, *consts)


def _mem_kv_kernel(m_ref, g_ref, w_ref, k_ref, v_ref):
    kv = _dot(_rms(m_ref[...], g_ref[...]).astype(BF16), w_ref[...])
    k_ref[...] = kv[:, :BRANCH_WIDTH].astype(BF16)
    v_ref[...] = kv[:, BRANCH_WIDTH:].astype(BF16)


def _mem_kv(mem2, g, w_kv):
    rows, d = mem2.shape
    tm = ROW_TILE
    out = pl.BlockSpec((tm, BRANCH_WIDTH), lambda i: (i, 0))
    return pl.pallas_call(
        _mem_kv_kernel,
        grid=(rows // tm,),
        in_specs=[pl.BlockSpec((tm, d), lambda i: (i, 0)), _const_spec((1, d)), _const_spec(w_kv.shape)],
        out_specs=[out, out],
        out_shape=[jax.ShapeDtypeStruct((rows, BRANCH_WIDTH), BF16)] * 2,
        compiler_params=pltpu.CompilerParams(dimension_semantics=("arbitrary",)),
        name="mem_kv",
    )(mem2, g.astype(F32).reshape(1, d), w_kv.astype(BF16))


def _mem_attn_kernel(q_ref, k_ref, v_ref, o_ref):
    q = q_ref[...]
    k = k_ref[...]
    v = v_ref[...]
    o = jnp.zeros(o_ref.shape, F32)
    for h in range(N_HEADS):
        s = _dot_nt(q * _head_mask(h, BF16), k)
        e = jnp.exp(s - jnp.max(s, axis=-1, keepdims=True))
        oh = _dot(e.astype(BF16), v) / jnp.sum(e, axis=-1, keepdims=True)
        o = jnp.where(_head_mask(h, F32) > 0.5, oh, o)
    o_ref[...] = o


def _mem_attention(q, k, v):
    b, s_len, w = q.shape
    tq = ATT_TILE
    q_spec = pl.BlockSpec((None, tq, w), lambda bi, i: (bi, i, 0))
    kv_spec = pl.BlockSpec((None, k.shape[1], w), lambda bi, i: (bi, 0, 0))
    return pl.pallas_call(
        _mem_attn_kernel,
        grid=(b, s_len // tq),
        in_specs=[q_spec, kv_spec, kv_spec],
        out_specs=q_spec,
        out_shape=jax.ShapeDtypeStruct((b, s_len, w), F32),
        compiler_params=pltpu.CompilerParams(dimension_semantics=("arbitrary", "arbitrary")),
        name="mem_attention",
    )(q, k, v)


def _merge_kernel(x_ref, g_ref, oa_ref, ob_ref, oc_ref, od_ref, oe_ref, zs_ref, wm_ref, bm_ref, wb_ref,
                  wo_ref, fg_ref, o_ref, *, final_norm):
    x = x_ref[...]
    d = x.shape[-1]
    w = BRANCH_WIDTH
    h = _rms(x, g_ref[...]).astype(BF16)
    mixed = jnp.zeros(x.shape, F32)
    for n, br_ref in enumerate((oa_ref, ob_ref, oc_ref, od_ref, oe_ref)):
        br = (br_ref[...] * zs_ref[:, n * w:(n + 1) * w]).astype(BF16)
        y = _dot(br, wb_ref[n])
        gate = jax.nn.sigmoid(_dot(h, wm_ref[:, n * d:(n + 1) * d]) + bm_ref[:, n * d:(n + 1) * d])
        mixed = mixed + gate * y
    out = x + _dot(mixed.astype(BF16), wo_ref[...])
    if final_norm:
        out = _rms(out, fg_ref[...])
    o_ref[...] = out


def _merge(x2, g, branches, zs, w_merge, b_merge, w_branch, w_out, final_g, final_norm):
    t, d = x2.shape
    tm = ROW_TILE
    row = lambda wd: pl.BlockSpec((tm, wd), lambda i: (i, 0))
    return pl.pallas_call(
        functools.partial(_merge_kernel, final_norm=final_norm),
        grid=(t // tm,),
        in_specs=[row(d), _const_spec((1, d))] + [row(BRANCH_WIDTH)] * N_BRANCHES + [row(zs.shape[1])]
        + [_const_spec(w_merge.shape), _const_spec((1, N_BRANCHES * d)), _const_spec(w_branch.shape),
           _const_spec(w_out.shape), _const_spec((1, d))],
        out_specs=row(d),
        out_shape=jax.ShapeDtypeStruct((t, d), F32),
        compiler_params=pltpu.CompilerParams(dimension_semantics=("arbitrary",),
                                             vmem_limit_bytes=BIG_VMEM_LIMIT),
        name="merge",
    )(x2, g.astype(F32).reshape(1, d), *branches, zs, w_merge.astype(BF16),
      b_merge.astype(F32).reshape(1, -1), w_branch.astype(BF16), w_out.astype(BF16),
      final_g.astype(F32).reshape(1, d))


def kernel(x, mem, norm_g, w_in, diff_lambda, diff_subln_g, s5_lambda_re, s5_lambda_im, s5_log_dt,
           s5_b_re, s5_b_im, s5_c_re, s5_c_im, s5_d, w_glu, b_glu, nsa_pe, nsa_w1, nsa_w2, mem_norm_g,
           w_mem_kv, w_merge, b_merge, w_branch, w_out, final_g):
    bsz, s_len, d = x.shape
    depth = w_in.shape[0]
    t = bsz * s_len
    w = BRANCH_WIDTH
    tables = _rope_tables(s_len, DIFF_QK_DIM) + _rope_tables(s_len, HEAD_DIM)
    x2 = x.astype(F32).reshape(t, d)
    mem2 = mem.astype(F32).reshape(-1, d)
    for l in range(depth):
        proj = dict(zip([p[0] for p in _IN_PLAN],
                        _in_proj(x2, norm_g[l].astype(F32), _in_weights(w_in[l].astype(F32)), tables, s_len)))
        seq = lambda name: proj[name].reshape(bsz, s_len, -1)

        dl = diff_lambda[l].astype(F32)
        lam_init = 0.8 - 0.6 * math.exp(-0.3 * l)
        lam = jnp.exp(jnp.sum(dl[0] * dl[1])) - jnp.exp(jnp.sum(dl[2] * dl[3])) + lam_init
        o_a = _diff_attention(seq("qa"), seq("ka"), seq("va"), lam, diff_subln_g[l], lam_init)

        o_b = _dilated_attention(seq("qb"), seq("kb"), seq("vb"))

        u_tb = jnp.swapaxes(seq("cu"), 0, 1).reshape(t, w)
        s5p = _s5_params(s5_lambda_re[l], s5_lambda_im[l], s5_log_dt[l], s5_b_re[l], s5_b_im[l],
                         s5_c_re[l], s5_c_im[l])
        o_c = _s5_branch(u_tb, bsz, s5p, s5_d[l], w_glu[l], b_glu[l])
        o_c = jnp.swapaxes(o_c.reshape(s_len, bsz, w), 0, 1)

        k_cmp, v_cmp = _compress(seq("kvc"), nsa_pe[l], nsa_w1[l], nsa_w2[l])
        o_d = _nsa_attention(seq("dq"), seq("dqr"), seq("dg"), k_cmp, v_cmp,
                             seq("ksr"), seq("vsr"), seq("kwr"), seq("vwr"))

        k_mem, v_mem = _mem_kv(mem2, mem_norm_g[l], w_mem_kv[l])
        o_e = _mem_attention(seq("eq"), k_mem.reshape(bsz, -1, w), v_mem.reshape(bsz, -1, w))

        branches = [o.reshape(t, w) for o in (o_a, o_b, o_c, o_d, o_e)]
        x2 = _merge(x2, norm_g[l], branches, proj["zs"], w_merge[l], b_merge[l], w_branch[l], w_out[l],
                    final_g, final_norm=(l == depth - 1))
    return x2.reshape(bsz, s_len, d).astype(x.dtype)
```

```python
import functools
import math

import numpy as np
import jax
import jax.numpy as jnp
from jax import lax
from jax.experimental import pallas as pl
from jax.experimental.pallas import tpu as pltpu

F32 = jnp.float32
BF16 = jnp.bfloat16

HEAD_DIM = 64
BRANCH_WIDTH = 256
N_HEADS = 4
N_BRANCHES = 5
DIFF_QK_DIM = 32
DIL_PATTERNS = ((128, 1), (512, 4), (2048, 16))
S5_GROUP = 16
S5_GROUPS = 16
S5_STATE = 64
CMP_BLOCK = 32
CMP_STRIDE = 16
SLC_BLOCK = 64
N_SELECT = 16
WINDOW = 512
ROPE_THETA = 10000.0
RMS_EPS = 1e-6
NEG_INF = -1e30
FORCE_SCORE = 1e9

V7X_VMEM_BYTES = 64 * 1024 * 1024
BIG_VMEM_LIMIT = V7X_VMEM_BYTES - 8 * 1024 * 1024

LANES = 128
ATT_TILE = 256
ROW_TILE = 256
S5_CHUNK = 128

_NT = (((1,), (1,)), ((), ()))


def _rms(x, g):
    return x * lax.rsqrt(jnp.mean(x * x, axis=-1, keepdims=True) + RMS_EPS) * g


def _dot(a, b):
    return jnp.dot(a, b, preferred_element_type=F32)


def _dot_nt(a, b):
    return lax.dot_general(a, b, _NT, preferred_element_type=F32)


def _split_hi_lo(x):
    hi = x.astype(BF16)
    lo = (x - hi.astype(F32)).astype(BF16)
    return hi, lo


def _dot_hilo(x, w):
    hi, lo = _split_hi_lo(x)
    return _dot(hi, w) + _dot(lo, w)


def _const_spec(shape):
    n = len(shape)
    return pl.BlockSpec(shape, lambda *_: (0,) * n, pipeline_mode=pl.Buffered(1))


def _head_mask(h, dtype):
    lane = lax.broadcasted_iota(jnp.int32, (1, BRANCH_WIDTH), 1)
    return jnp.where((lane >= h * HEAD_DIM) & (lane < (h + 1) * HEAD_DIM), 1.0, 0.0).astype(dtype)


def _low_half():
    return lax.broadcasted_iota(jnp.int32, (1, LANES), 1) < HEAD_DIM


def _swap_halves(x):
    return pltpu.roll(x, HEAD_DIM, 1)


def _augment_heads(v):
    low = _low_half()
    parts = []
    for j in range(v.shape[1] // LANES):
        pair = v[:, j * LANES:(j + 1) * LANES]
        parts += [jnp.where(low, pair, 1.0), jnp.where(low, 1.0, pair)]
    return jnp.concatenate(parts, axis=1)


def _shared_kv_variants(kv):
    low = _low_half()
    sw = _swap_halves(kv)
    k2 = jnp.concatenate([jnp.where(low, kv, 0.0), jnp.where(low, 0.0, sw)], axis=1)
    v2 = jnp.concatenate([jnp.where(low, sw, 1.0), jnp.where(low, 1.0, kv)], axis=1)
    return k2, v2


def _normalise_pair(acc_even, acc_odd):
    return jnp.where(_low_half(), acc_even / _swap_halves(acc_even), acc_odd / _swap_halves(acc_odd))


_IN_COLS = (("qa", 256), ("ka", 256), ("va", 256), ("qb", 256), ("kb", 256), ("vb", 256), ("cu", 256),
            ("dq", 256), ("kvc", 128), ("kvs", 128), ("kvw", 128), ("dg", 128), ("eq", 256), ("zs", 1280))
_IN_OFFS = dict(zip([n for n, _ in _IN_COLS], np.cumsum([0] + [w for _, w in _IN_COLS])[:-1].tolist()))
_IN_WIDTH = dict(_IN_COLS)
_IN_OUTS = (("qa", 256, BF16), ("ka", 256, BF16), ("va", 512, BF16),
            ("qb", 256, BF16), ("kb", 256, BF16), ("vb", 512, BF16),
            ("cu", 256, F32), ("dq", 256, BF16), ("dqr", 256, BF16), ("kvc", 128, F32),
            ("ks", 256, BF16), ("vs", 256, BF16), ("kw", 256, BF16), ("vw", 256, BF16),
            ("dg", 128, F32), ("eq", 256, BF16), ("zs", 1280, F32))


def _rotate_half(y, group):
    half = group // 2
    lane = lax.broadcasted_iota(jnp.int32, (1, LANES), 1)
    first = (lane & (group - 1)) < half
    parts = []
    for j in range(y.shape[1] // LANES):
        v = y[:, j * LANES:(j + 1) * LANES]
        parts.append(jnp.where(first, pltpu.roll(v, LANES - half, 1), pltpu.roll(v, half, 1)))
    return jnp.concatenate(parts, axis=1)


def _in_proj_kernel(x_ref, g_ref, w_ref, cosa_ref, sina_ref, cosb_ref, sinb_ref, cosk_ref, sink_ref, *out_refs):
    out = dict(zip([n for n, _, _ in _IN_OUTS], out_refs))
    h = _rms(x_ref[...], g_ref[...]).astype(BF16)

    def proj(name):
        off = _IN_OFFS[name]
        return _dot(h, w_ref[:, off:off + _IN_WIDTH[name]])

    def rope(y, cos_ref, sin_ref, group):
        return y * cos_ref[...] + _rotate_half(y, group) * sin_ref[...]

    def put(name, y):
        out[name][...] = y.astype(out[name].dtype)

    put("qa", rope(proj("qa"), cosa_ref, sina_ref, DIFF_QK_DIM))
    put("ka", rope(proj("ka"), cosa_ref, sina_ref, DIFF_QK_DIM))
    put("va", _augment_heads(proj("va")))
    put("qb", rope(proj("qb"), cosb_ref, sinb_ref, HEAD_DIM))
    put("kb", rope(proj("kb"), cosb_ref, sinb_ref, HEAD_DIM))
    put("vb", _augment_heads(proj("vb")))
    put("cu", proj("cu"))
    dq = proj("dq")
    put("dq", dq)
    put("dqr", rope(dq, cosb_ref, sinb_ref, HEAD_DIM))
    put("kvc", proj("kvc"))
    for kv_name, k_name, v_name in (("kvs", "ks", "vs"), ("kvw", "kw", "vw")):
        k2, v2 = _shared_kv_variants(rope(proj(kv_name), cosk_ref, sink_ref, HEAD_DIM))
        put(k_name, k2)
        put(v_name, v2)
    put("dg", proj("dg"))
    put("eq", proj("eq"))
    z = proj("zs")
    put("zs", z * jax.nn.sigmoid(z))


def _in_weights(w):
    sizes = (256,) * 10 + (256, 64, 64, 64, 64, 64, 64, 12, 256, 256, 256)
    offs = np.cumsum((0,) + sizes)
    (a_q, a_k, a_v, a_z, b_q, b_k, b_v, b_z, c_u, c_z, d_q, d_kc, d_vc, d_ks, d_vs, d_kw, d_vw,
     d_g, d_z, e_q, e_z) = [w[:, offs[i]:offs[i + 1]] for i in range(len(sizes))]
    scale = HEAD_DIM ** -0.5
    cat = lambda *t: jnp.concatenate(t, axis=1)
    cols = [a_q, a_k, a_v, b_q * scale, b_k, b_v, c_u, d_q * scale, cat(d_kc, d_vc), cat(d_ks, d_vs),
            cat(d_kw, d_vw), jnp.pad(d_g, ((0, 0), (0, 128 - d_g.shape[1]))), e_q * scale,
            a_z, b_z, c_z, d_z, e_z]
    return cat(*cols).astype(BF16)


def _rope_tables(s_len):
    def table(group, width):
        half = group // 2
        inv_freq = ROPE_THETA ** (-jnp.arange(half, dtype=F32) / half)
        ang = jnp.arange(s_len, dtype=F32)[:, None] * inv_freq[None, :]
        cos = jnp.tile(jnp.cos(ang), (1, width // half))
        sin = jnp.tile(jnp.concatenate([-jnp.sin(ang), jnp.sin(ang)], axis=1), (1, width // group))
        return cos, sin
    cos_a, sin_a = table(DIFF_QK_DIM, BRANCH_WIDTH)
    cos_b, sin_b = table(HEAD_DIM, BRANCH_WIDTH)
    cos_k = jnp.concatenate([cos_b[:, :HEAD_DIM], jnp.ones((s_len, HEAD_DIM), F32)], axis=1)
    sin_k = jnp.concatenate([sin_b[:, :HEAD_DIM], jnp.zeros((s_len, HEAD_DIM), F32)], axis=1)
    return cos_a, sin_a, cos_b, sin_b, cos_k, sin_k


def _in_proj(x2, g, wcat, tables, s_len):
    t, d = x2.shape
    tm = ROW_TILE
    nsb = s_len // tm
    row = lambda w: pl.BlockSpec((tm, w), lambda i: (i, 0))
    tab = lambda a: pl.BlockSpec((tm, a.shape[1]), lambda i: (i % nsb, 0))
    return pl.pallas_call(
        _in_proj_kernel,
        grid=(t // tm,),
        in_specs=[row(d), _const_spec((1, d)), _const_spec(wcat.shape)] + [tab(a) for a in tables],
        out_specs=[row(w) for _, w, _ in _IN_OUTS],
        out_shape=[jax.ShapeDtypeStruct((t, w), dt) for _, w, dt in _IN_OUTS],
        compiler_params=pltpu.CompilerParams(dimension_semantics=("arbitrary",),
                                             vmem_limit_bytes=BIG_VMEM_LIMIT),
        name="in_proj",
    )(x2, g.reshape(1, d), wcat, *tables)


def _lane_fold(x, op):
    parts = [x[:, j * LANES:(j + 1) * LANES] for j in range(x.shape[1] // LANES)]
    return functools.reduce(op, parts)


def _strip_attention(chains, tq, kb_lo, kb_hi, bias_fn, last_bias, scratch, exp_scale=1.0):
    s_ref, m_ref, acc_ref = scratch
    n = len(chains)
    reps = tq // LANES

    def tile(cache, ref, col, width, kb):
        key = (id(ref), col, width)
        if key not in cache:
            cache[key] = ref[pl.ds(pl.multiple_of(kb * tq, tq), tq), col:col + width]
        return cache[key]

    def scores(kb, extra):
        bias = bias_fn(kb) if bias_fn is not None else None
        if extra is not None:
            bias = extra if bias is None else bias + extra
        cache, out = {}, []
        for q, (k_ref, k_col), _ in chains:
            s = _dot_nt(q, tile(cache, k_ref, k_col, q.shape[1], kb))
            out.append(s if bias is None else s + bias)
        return out

    m_ref[0:n] = jnp.full((n,) + m_ref.shape[1:], NEG_INF, F32)

    def pass1(kb, carry):
        for i, s in enumerate(scores(kb, None)):
            s_ref[i, kb] = s
            m_ref[i] = jnp.maximum(m_ref[i], _lane_fold(s, jnp.maximum))
        return carry

    lax.fori_loop(kb_lo, kb_hi - 1, pass1, 0)
    for i, s in enumerate(scores(kb_hi - 1, last_bias)):
        s_ref[i, kb_hi - 1] = s
        m = jnp.max(jnp.maximum(m_ref[i], _lane_fold(s, jnp.maximum)), axis=-1, keepdims=True)
        m_ref[i] = jnp.broadcast_to(m, m_ref.shape[1:])
        acc_ref[i] = jnp.zeros(acc_ref.shape[1:], F32)

    c1 = exp_scale * math.log2(math.e)

    def pass2(kb, carry):
        cache = {}
        for i, (_, _, (v_ref, v_col)) in enumerate(chains):
            m = jnp.concatenate([m_ref[i]] * reps, axis=1)
            p = jnp.exp2((s_ref[i, kb] - m) * c1)
            acc_ref[i] += _dot(p.astype(BF16), tile(cache, v_ref, v_col, LANES, kb))
        return carry

    lax.fori_loop(kb_lo, kb_hi, pass2, 0)
    return [acc_ref[i] for i in range(n)]


def _strip_scratch(n, tq, s_len):
    return [pltpu.VMEM((n, s_len // tq, tq, tq), F32), pltpu.VMEM((n, tq, LANES), F32),
            pltpu.VMEM((n, tq, LANES), F32)]


def _causal_bias(tq):
    r = lax.broadcasted_iota(jnp.int32, (tq, tq), 0)
    c = lax.broadcasted_iota(jnp.int32, (tq, tq), 1)
    return jnp.where(c <= r, 0.0, NEG_INF).astype(F32)


def _att_specs(tq, s_len, k_width, v_width):
    q_spec = pl.BlockSpec((None, tq, BRANCH_WIDTH), lambda b, i: (b, i, 0))
    kv = lambda w: pl.BlockSpec((None, s_len, w), lambda b, i: (b, 0, 0))
    return q_spec, kv(k_width), kv(v_width)


def _diff_kernel(lam_ref, q_ref, k_ref, v_ref, g_ref, hm_ref, o_ref, *scratch, tq, out_scale):
    qi = pl.program_id(1)
    q = q_ref[...]
    lane = lax.broadcasted_iota(jnp.int32, (1, BRANCH_WIDTH), 1)
    lam = lam_ref[0]
    chains = []
    for hc in range(2 * N_HEADS):
        lo = hc * DIFF_QK_DIM
        cmask = jnp.where((lane >= lo) & (lane < lo + DIFF_QK_DIM), 1.0, 0.0).astype(BF16)
        chains.append((q * cmask, (k_ref, 0), (v_ref, (hc // 2) * LANES)))
    acc = _strip_attention(chains, tq, 0, qi + 1, None, _causal_bias(tq), scratch,
                           exp_scale=DIFF_QK_DIM ** -0.5)
    halves = []
    for pair in range(N_HEADS // 2):
        even, odd = 4 * pair, 4 * pair + 2
        halves.append(_normalise_pair(acc[even], acc[odd]) - lam * _normalise_pair(acc[even + 1], acc[odd + 1]))
    o = jnp.concatenate(halves, axis=1)
    ms = _dot_hilo(o * o, hm_ref[...])
    o_ref[...] = o * lax.rsqrt(ms + RMS_EPS) * g_ref[...] * out_scale


def _diff_attention(q, k, v_aug, lam, subln_g, lam_init):
    b, s_len, w = q.shape
    tq = ATT_TILE
    q_spec, k_spec, v_spec = _att_specs(tq, s_len, w, v_aug.shape[-1])
    head = np.arange(w) // HEAD_DIM
    hm = jnp.asarray((head[:, None] == head[None, :]) / HEAD_DIM, dtype=BF16)
    g = jnp.tile(subln_g.astype(F32), N_HEADS).reshape(1, w)
    return pl.pallas_call(
        functools.partial(_diff_kernel, tq=tq, out_scale=1.0 - lam_init),
        grid=(b, s_len // tq),
        in_specs=[pl.BlockSpec(memory_space=pltpu.SMEM), q_spec, k_spec, v_spec,
                  _const_spec((1, w)), _const_spec((w, w))],
        out_specs=q_spec,
        out_shape=jax.ShapeDtypeStruct((b, s_len, w), F32),
        scratch_shapes=_strip_scratch(2 * N_HEADS, tq, s_len),
        compiler_params=pltpu.CompilerParams(dimension_semantics=("arbitrary", "arbitrary"),
                                             vmem_limit_bytes=BIG_VMEM_LIMIT),
        name="diff_attention",
    )(lam.reshape(1), q, k, v_aug, g, hm)


def _dil_kernel(q_ref, k_ref, v_ref, bias_ref, o_ref, *scratch, tq):
    qi = pl.program_id(1)
    q = q_ref[...]
    chains = [(q * _head_mask(h, BF16), (k_ref, 0), (v_ref, h * LANES)) for h in range(N_HEADS)]
    acc = _strip_attention(chains, tq, 0, qi + 1, lambda kb: bias_ref[qi - kb], None, scratch)
    o_ref[...] = jnp.concatenate([_normalise_pair(acc[0], acc[1]), _normalise_pair(acc[2], acc[3])], axis=1)


def _dilated_bias(tq, s_len):
    nq = s_len // tq
    d = (np.arange(nq)[:, None, None] * tq + np.arange(tq)[None, :, None] - np.arange(tq)[None, None, :])
    count = np.zeros(d.shape, np.float64)
    for window, dil in DIL_PATTERNS:
        count += (d >= 0) & (d <= window) & (d % dil == 0)
    return np.where(count > 0, np.log(np.maximum(count, 1.0)), NEG_INF).astype(np.float32)


def _dilated_attention(q, k, v_aug):
    b, s_len, w = q.shape
    tq = ATT_TILE
    q_spec, k_spec, v_spec = _att_specs(tq, s_len, w, v_aug.shape[-1])
    bias = jnp.asarray(_dilated_bias(tq, s_len))
    return pl.pallas_call(
        functools.partial(_dil_kernel, tq=tq),
        grid=(b, s_len // tq),
        in_specs=[q_spec, k_spec, v_spec, _const_spec(bias.shape)],
        out_specs=q_spec,
        out_shape=jax.ShapeDtypeStruct((b, s_len, w), F32),
        scratch_shapes=_strip_scratch(N_HEADS, tq, s_len),
        compiler_params=pltpu.CompilerParams(dimension_semantics=("arbitrary", "arbitrary"),
                                             vmem_limit_bytes=BIG_VMEM_LIMIT),
        name="dilated_attention",
    )(q, k, v_aug, bias)


def _s5_kernel(u_ref, bm_ref, cm_ref, are_ref, aim_ref, d_ref, wg_ref, bg_ref, o_ref, xs_ref, st_ref,
               *, ts, nb):
    n = S5_GROUPS * S5_STATE

    @pl.when(pl.program_id(0) == 0)
    def _():
        st_ref[...] = jnp.zeros(st_ref.shape, F32)

    u = u_ref[...]
    xs_ref[...] = _dot(u.astype(BF16), bm_ref[...])
    a_re = jnp.broadcast_to(are_ref[...], (nb, n))
    a_im = jnp.broadcast_to(aim_ref[...], (nb, n))

    def step(t, carry):
        x_re, x_im = carry
        r0 = pl.multiple_of(t * nb, nb)
        n_re = a_re * x_re - a_im * x_im + xs_ref[pl.ds(r0, nb), 0:n]
        n_im = a_re * x_im + a_im * x_re + xs_ref[pl.ds(r0, nb), n:2 * n]
        xs_ref[pl.ds(r0, nb), 0:n] = n_re
        xs_ref[pl.ds(r0, nb), n:2 * n] = n_im
        return n_re, n_im

    x_re, x_im = lax.fori_loop(0, ts, step, (st_ref[:, 0:n], st_ref[:, n:2 * n]))
    st_ref[:, 0:n] = x_re
    st_ref[:, n:2 * n] = x_im

    y = _dot(xs_ref[...].astype(BF16), cm_ref[...]) + d_ref[...] * u
    t = _dot(jax.nn.gelu(y).astype(BF16), wg_ref[...]) + bg_ref[...]
    o_ref[...] = t[:, :BRANCH_WIDTH] * jax.nn.sigmoid(t[:, BRANCH_WIDTH:])


def _s5_params(lam_re, lam_im, log_dt, b_re, b_im, c_re, c_im):
    g, n, p = S5_GROUPS, S5_STATE, S5_GROUP
    lr, li = lam_re.astype(F32), lam_im.astype(F32)
    dt = jnp.exp(log_dt.astype(F32))[:, None]
    mag = jnp.exp(lr * dt)
    a_re, a_im = mag * jnp.cos(li * dt), mag * jnp.sin(li * dt)
    den = lr * lr + li * li
    n_re, n_im = a_re - 1.0, a_im
    z_re = (n_re * lr + n_im * li) / den
    z_im = (n_im * lr - n_re * li) / den
    br, bi = b_re.astype(F32), b_im.astype(F32)
    bb_re = z_re[..., None] * br - z_im[..., None] * bi
    bb_im = z_re[..., None] * bi + z_im[..., None] * br
    eye = jnp.eye(g, dtype=F32)
    blockdiag_in = lambda t: jnp.einsum("gnp,gh->gphn", t, eye).reshape(g * p, g * n)
    blockdiag_out = lambda t: jnp.einsum("gpn,gh->gnhp", t, eye).reshape(g * n, g * p)
    bm = jnp.concatenate([blockdiag_in(bb_re), blockdiag_in(bb_im)], axis=1)
    cm = jnp.concatenate([blockdiag_out(c_re.astype(F32)), -blockdiag_out(c_im.astype(F32))], axis=0)
    return bm.astype(BF16), cm.astype(BF16), a_re.reshape(1, g * n), a_im.reshape(1, g * n)


def _s5_branch(u_tb, nb, params, d_skip, w_glu, b_glu):
    rows, w = u_tb.shape
    ts = S5_CHUNK
    bm, cm, a_re, a_im = params
    n2 = bm.shape[1]
    blk = pl.BlockSpec((ts * nb, w), lambda i: (i, 0))
    return pl.pallas_call(
        functools.partial(_s5_kernel, ts=ts, nb=nb),
        grid=(rows // (ts * nb),),
        in_specs=[blk, _const_spec(bm.shape), _const_spec(cm.shape), _const_spec(a_re.shape),
                  _const_spec(a_im.shape), _const_spec((1, w)), _const_spec(w_glu.shape),
                  _const_spec((1, 2 * w))],
        out_specs=blk,
        out_shape=jax.ShapeDtypeStruct((rows, w), F32),
        scratch_shapes=[pltpu.VMEM((ts * nb, n2), F32), pltpu.VMEM((nb, n2), F32)],
        compiler_params=pltpu.CompilerParams(dimension_semantics=("arbitrary",),
                                             vmem_limit_bytes=BIG_VMEM_LIMIT),
        name="s5_scan",
    )(u_tb, bm, cm, a_re, a_im, d_skip.astype(F32).reshape(1, w), w_glu.astype(BF16),
      b_glu.astype(F32).reshape(1, 2 * w))


def _compress_kernel(r_ref, pe_top_ref, pe_bot_ref, w_top_ref, w_bot_ref, w2_ref, k_ref, v_ref):
    r = r_ref[...]
    top = (r + pe_top_ref[...]).astype(BF16)
    nxt = pltpu.roll(r, r.shape[0] - 1, 0)
    bot = (nxt + pe_bot_ref[...]).astype(BF16)
    hid = jax.nn.gelu(_dot(top, w_top_ref[...]) + _dot(bot, w_bot_ref[...]))
    k2, v2 = _shared_kv_variants(_dot(hid.astype(BF16), w2_ref[...]))
    k_ref[...] = k2.astype(BF16)
    v_ref[...] = v2.astype(BF16)


def _compress(kvc, pe, w1, w2):
    b, s_len, _ = kvc.shape
    nr = s_len // CMP_STRIDE
    per = CMP_BLOCK // CMP_STRIDE
    assert per == 2
    hid = w1.shape[-1]
    r = kvc.reshape(b, nr, CMP_STRIDE * 2 * HEAD_DIM)
    w1r = w1.astype(F32).reshape(2, per, CMP_STRIDE, HEAD_DIM, hid)
    per_r = pe.astype(F32).reshape(2, per, CMP_STRIDE, HEAD_DIM)

    def expand(j):
        wk = jnp.pad(w1r[0, j], ((0, 0), (0, HEAD_DIM), (0, hid)))
        wv = jnp.pad(w1r[1, j], ((0, 0), (HEAD_DIM, 0), (hid, 0)))
        return (wk + wv).reshape(CMP_STRIDE * 2 * HEAD_DIM, 2 * hid).astype(BF16)

    pe_rows = [jnp.concatenate([per_r[0, j], per_r[1, j]], axis=-1).reshape(1, -1) for j in range(per)]
    z = jnp.zeros((hid, HEAD_DIM), F32)
    w2f = w2.astype(F32)
    w2x = jnp.concatenate([jnp.concatenate([w2f[0], z], axis=1),
                           jnp.concatenate([z, w2f[1]], axis=1)], axis=0).astype(BF16)
    w_top, w_bot = expand(0), expand(1)
    blk = pl.BlockSpec((None, nr, r.shape[-1]), lambda i: (i, 0, 0))
    out = pl.BlockSpec((None, nr, BRANCH_WIDTH), lambda i: (i, 0, 0))
    return pl.pallas_call(
        _compress_kernel,
        grid=(b,),
        in_specs=[blk, _const_spec(pe_rows[0].shape), _const_spec(pe_rows[1].shape),
                  _const_spec(w_top.shape), _const_spec(w_bot.shape), _const_spec(w2x.shape)],
        out_specs=[out, out],
        out_shape=[jax.ShapeDtypeStruct((b, nr, BRANCH_WIDTH), BF16)] * 2,
        compiler_params=pltpu.CompilerParams(dimension_semantics=("arbitrary",)),
        name="nsa_compress",
    )(r, pe_rows[0], pe_rows[1], w_top, w_bot, w2x)


def _nsa_kernel(dq_ref, dqr_ref, dg_ref, kc_ref, vc_ref, ks_ref, vs_ref, kw_ref, vw_ref,
                ovt_ref, exp_ref, gsel_ref, wbias_ref, o_ref, *scratch, tq, n_cmp, n_slc, n_sel):
    qi = pl.program_id(1)
    t0 = qi * tq
    nr = kc_ref.shape[0]
    low = _low_half()
    q_cols = lambda h: slice((h // 2) * LANES, (h // 2 + 1) * LANES)
    kv_col = lambda h: (h % 2) * LANES

    dq = dq_ref[...]
    tpos = t0 + lax.broadcasted_iota(jnp.int32, (tq, nr), 0)
    ci = lax.broadcasted_iota(jnp.int32, (tq, nr), 1)
    cmask = (ci * CMP_STRIDE + (CMP_BLOCK - 1) <= tpos) & (ci < n_cmp)
    p_sum = jnp.zeros((tq, nr), F32)
    o_heads = []
    for h in range(N_HEADS):
        s = jnp.where(cmask, _dot_nt(dq[:, q_cols(h)], kc_ref[:, kv_col(h):kv_col(h) + LANES]), NEG_INF)
        m = jnp.max(s, axis=-1, keepdims=True)
        e = jnp.where(cmask, jnp.exp(s - m), 0.0)
        p = e / jnp.maximum(jnp.sum(e, axis=-1, keepdims=True), 1e-30)
        p_sum = p_sum + p
        o_heads.append(_dot(p.astype(BF16), vc_ref[:, kv_col(h):kv_col(h) + LANES]))
    o_cmp = jnp.concatenate([jnp.where(low, o_heads[0], o_heads[1]), jnp.where(low, o_heads[2], o_heads[3])],
                            axis=1)

    p_hi, p_lo = _split_hi_lo(p_sum)
    ovt = ovt_ref[...]
    imp_t = _dot_nt(ovt, p_hi) + _dot_nt(ovt, p_lo)
    blk = lax.broadcasted_iota(jnp.int32, (n_slc, tq), 0)
    qblk = lax.shift_right_arithmetic(t0 + lax.broadcasted_iota(jnp.int32, (n_slc, tq), 1),
                                      int(math.log2(SLC_BLOCK)))
    forced = jnp.where(blk == 0, 1, jnp.where(blk == qblk, 1, jnp.where(blk == qblk - 1, 1, 0)))
    score = jnp.where(blk <= qblk, jnp.where(forced > 0, FORCE_SCORE, imp_t), NEG_INF)
    rank = jnp.zeros((n_slc, tq), F32)
    for i in range(n_slc):
        si = score[i:i + 1, :]
        tie = jnp.where(blk > i, 1.0, 0.0)
        rank = rank + jnp.where(si > score, 1.0, jnp.where(si == score, tie, 0.0))
    sel_bias_t = jnp.where(rank < n_sel, jnp.where(score > 0.5 * NEG_INF, 0.0, NEG_INF), NEG_INF)
    pad_rows = exp_ref.shape[1] - n_slc
    sel_bias = jnp.concatenate([sel_bias_t, jnp.full((pad_rows, tq), NEG_INF, F32)], axis=0).T.astype(BF16)

    dqr = dqr_ref[...]
    pairs = lambda acc: jnp.concatenate([_normalise_pair(acc[0], acc[1]), _normalise_pair(acc[2], acc[3])], axis=1)
    chains = [(dqr[:, q_cols(h)], (ks_ref, kv_col(h)), (vs_ref, kv_col(h))) for h in range(N_HEADS)]
    o_slc = pairs(_strip_attention(chains, tq, 0, qi + 1, lambda kb: _dot(sel_bias, exp_ref[kb]),
                                   _causal_bias(tq), scratch))
    chains = [(dqr[:, q_cols(h)], (kw_ref, kv_col(h)), (vw_ref, kv_col(h))) for h in range(N_HEADS)]
    win_lo = jnp.maximum(qi - (wbias_ref.shape[0] - 1), 0)
    o_win = pairs(_strip_attention(chains, tq, win_lo, qi + 1, lambda kb: wbias_ref[qi - kb], None, scratch))

    gates = _dot_hilo(jax.nn.sigmoid(dg_ref[...]), gsel_ref[...])
    w = BRANCH_WIDTH
    o_ref[...] = gates[:, :w] * o_cmp + gates[:, w:2 * w] * o_slc + gates[:, 2 * w:] * o_win


def _nsa_constants(tq, s_len):
    n_cmp = (s_len - CMP_BLOCK) // CMP_STRIDE + 1
    n_slc = s_len // SLC_BLOCK
    nr = s_len // CMP_STRIDE
    c0 = np.arange(n_cmp)[:, None] * CMP_STRIDE
    s0 = np.arange(n_slc)[None, :] * SLC_BLOCK
    overlap = np.clip(np.minimum(c0 + CMP_BLOCK, s0 + SLC_BLOCK) - np.maximum(c0, s0), 0, None) / CMP_STRIDE
    ovt = np.zeros((n_slc, nr), np.float32)
    ovt[:, :n_cmp] = overlap.T
    rows = -(-n_slc // 128) * 128
    expand = np.zeros((s_len // tq, rows, tq), np.float32)
    tok = np.arange(s_len)
    expand[tok // tq, tok // SLC_BLOCK, tok % tq] = 1.0
    gsel = np.zeros((128, 3 * BRANCH_WIDTH), np.float32)
    for h in range(N_HEADS):
        for j in range(3):
            gsel[h * 3 + j, j * BRANCH_WIDTH + h * HEAD_DIM:j * BRANCH_WIDTH + (h + 1) * HEAD_DIM] = 1.0
    nwin = -(-WINDOW // tq) + 1
    d = np.arange(nwin)[:, None, None] * tq + np.arange(tq)[None, :, None] - np.arange(tq)[None, None, :]
    wbias = np.where((d >= 0) & (d < WINDOW), 0.0, NEG_INF).astype(np.float32)
    return n_cmp, n_slc, ovt, expand, gsel, wbias


def _nsa_attention(dq, dqr, dg, k_cmp, v_cmp, ks, vs, kw, vw):
    b, s_len, w = dq.shape
    tq = ATT_TILE
    n_cmp, n_slc, ovt, expand, gsel, wbias = _nsa_constants(tq, s_len)
    q_spec, kv_spec, _ = _att_specs(tq, s_len, w, w)
    nr = k_cmp.shape[1]
    cmp_spec = pl.BlockSpec((None, nr, w), lambda bi, i: (bi, 0, 0))
    g_spec = pl.BlockSpec((None, tq, dg.shape[-1]), lambda bi, i: (bi, i, 0))
    consts = [jnp.asarray(ovt, BF16), jnp.asarray(expand, BF16), jnp.asarray(gsel, BF16), jnp.asarray(wbias)]
    return pl.pallas_call(
        functools.partial(_nsa_kernel, tq=tq, n_cmp=n_cmp, n_slc=n_slc, n_sel=min(N_SELECT, n_slc)),
        grid=(b, s_len // tq),
        in_specs=[q_spec, q_spec, g_spec, cmp_spec, cmp_spec, kv_spec, kv_spec, kv_spec, kv_spec]
        + [_const_spec(c.shape) for c in consts],
        out_specs=q_spec,
        out_shape=jax.ShapeDtypeStruct((b, s_len, w), F32),
        scratch_shapes=_strip_scratch(N_HEADS, tq, s_len),
        compiler_params=pltpu.CompilerParams(dimension_semantics=("arbitrary", "arbitrary"),
                                             vmem_limit_bytes=BIG_VMEM_LIMIT),
        name="nsa_attention",
    )(dq, dqr, dg, k_cmp, v_cmp, ks, vs, kw, vw, *consts)


def _mem_kv_kernel(m_ref, g_ref, w_ref, k_ref, v_ref):
    kv = _dot(_rms(m_ref[...], g_ref[...]).astype(BF16), w_ref[...])
    k_ref[...] = kv[:, :BRANCH_WIDTH].astype(BF16)
    v_ref[...] = kv[:, BRANCH_WIDTH:].astype(BF16)


def _mem_kv(mem2, g, w_kv):
    rows, d = mem2.shape
    tm = ROW_TILE
    out = pl.BlockSpec((tm, BRANCH_WIDTH), lambda i: (i, 0))
    return pl.pallas_call(
        _mem_kv_kernel,
        grid=(rows // tm,),
        in_specs=[pl.BlockSpec((tm, d), lambda i: (i, 0)), _const_spec((1, d)), _const_spec(w_kv.shape)],
        out_specs=[out, out],
        out_shape=[jax.ShapeDtypeStruct((rows, BRANCH_WIDTH), BF16)] * 2,
        compiler_params=pltpu.CompilerParams(dimension_semantics=("arbitrary",)),
        name="mem_kv",
    )(mem2, g.astype(F32).reshape(1, d), w_kv.astype(BF16))


def _mem_attn_kernel(q_ref, k_ref, v_ref, o_ref):
    q = q_ref[...]
    k = k_ref[...]
    v = v_ref[...]
    o = jnp.zeros(o_ref.shape, F32)
    for h in range(N_HEADS):
        s = _dot_nt(q * _head_mask(h, BF16), k)
        e = jnp.exp(s - jnp.max(s, axis=-1, keepdims=True))
        oh = _dot(e.astype(BF16), v) / jnp.sum(e, axis=-1, keepdims=True)
        o = jnp.where(_head_mask(h, F32) > 0.5, oh, o)
    o_ref[...] = o


def _mem_attention(q, k, v):
    b, s_len, w = q.shape
    tq = ATT_TILE
    q_spec = pl.BlockSpec((None, tq, w), lambda bi, i: (bi, i, 0))
    kv_spec = pl.BlockSpec((None, k.shape[1], w), lambda bi, i: (bi, 0, 0))
    return pl.pallas_call(
        _mem_attn_kernel,
        grid=(b, s_len // tq),
        in_specs=[q_spec, kv_spec, kv_spec],
        out_specs=q_spec,
        out_shape=jax.ShapeDtypeStruct((b, s_len, w), F32),
        compiler_params=pltpu.CompilerParams(dimension_semantics=("arbitrary", "arbitrary")),
        name="mem_attention",
    )(q, k, v)


def _merge_kernel(x_ref, g_ref, oa_ref, ob_ref, oc_ref, od_ref, oe_ref, zs_ref, wm_ref, bm_ref, wb_ref,
                  wo_ref, fg_ref, o_ref, *, final_norm):
    x = x_ref[...]
    d = x.shape[-1]
    w = BRANCH_WIDTH
    h = _rms(x, g_ref[...]).astype(BF16)
    mixed = jnp.zeros(x.shape, F32)
    for n, br_ref in enumerate((oa_ref, ob_ref, oc_ref, od_ref, oe_ref)):
        br = (br_ref[...] * zs_ref[:, n * w:(n + 1) * w]).astype(BF16)
        y = _dot(br, wb_ref[n])
        gate = jax.nn.sigmoid(_dot(h, wm_ref[:, n * d:(n + 1) * d]) + bm_ref[:, n * d:(n + 1) * d])
        mixed = mixed + gate * y
    out = x + _dot(mixed.astype(BF16), wo_ref[...])
    if final_norm:
        out = _rms(out, fg_ref[...])
    o_ref[...] = out


def _merge(x2, g, branches, zs, w_merge, b_merge, w_branch, w_out, final_g, final_norm):
    t, d = x2.shape
    tm = ROW_TILE
    row = lambda wd: pl.BlockSpec((tm, wd), lambda i: (i, 0))
    return pl.pallas_call(
        functools.partial(_merge_kernel, final_norm=final_norm),
        grid=(t // tm,),
        in_specs=[row(d), _const_spec((1, d))] + [row(BRANCH_WIDTH)] * N_BRANCHES + [row(zs.shape[1])]
        + [_const_spec(w_merge.shape), _const_spec((1, N_BRANCHES * d)), _const_spec(w_branch.shape),
           _const_spec(w_out.shape), _const_spec((1, d))],
        out_specs=row(d),
        out_shape=jax.ShapeDtypeStruct((t, d), F32),
        compiler_params=pltpu.CompilerParams(dimension_semantics=("arbitrary",),
                                             vmem_limit_bytes=BIG_VMEM_LIMIT),
        name="merge",
    )(x2, g.astype(F32).reshape(1, d), *branches, zs, w_merge.astype(BF16),
      b_merge.astype(F32).reshape(1, -1), w_branch.astype(BF16), w_out.astype(BF16),
      final_g.astype(F32).reshape(1, d))


def kernel(x, mem, norm_g, w_in, diff_lambda, diff_subln_g, s5_lambda_re, s5_lambda_im, s5_log_dt,
           s5_b_re, s5_b_im, s5_c_re, s5_c_im, s5_d, w_glu, b_glu, nsa_pe, nsa_w1, nsa_w2, mem_norm_g,
           w_mem_kv, w_merge, b_merge, w_branch, w_out, final_g):
    bsz, s_len, d = x.shape
    depth = w_in.shape[0]
    t = bsz * s_len
    w = BRANCH_WIDTH
    tables = _rope_tables(s_len)
    x2 = x.astype(F32).reshape(t, d)
    mem2 = mem.astype(F32).reshape(-1, d)
    for l in range(depth):
        proj = dict(zip([n for n, _, _ in _IN_OUTS],
                        _in_proj(x2, norm_g[l].astype(F32), _in_weights(w_in[l].astype(F32)), tables, s_len)))
        seq = lambda name: proj[name].reshape(bsz, s_len, -1)

        dl = diff_lambda[l].astype(F32)
        lam_init = 0.8 - 0.6 * math.exp(-0.3 * l)
        lam = jnp.exp(jnp.sum(dl[0] * dl[1])) - jnp.exp(jnp.sum(dl[2] * dl[3])) + lam_init
        o_a = _diff_attention(seq("qa"), seq("ka"), seq("va"), lam, diff_subln_g[l], lam_init)

        o_b = _dilated_attention(seq("qb"), seq("kb"), seq("vb"))

        u_tb = jnp.swapaxes(seq("cu"), 0, 1).reshape(t, w)
        s5p = _s5_params(s5_lambda_re[l], s5_lambda_im[l], s5_log_dt[l], s5_b_re[l], s5_b_im[l],
                         s5_c_re[l], s5_c_im[l])
        o_c = _s5_branch(u_tb, bsz, s5p, s5_d[l], w_glu[l], b_glu[l])
        o_c = jnp.swapaxes(o_c.reshape(s_len, bsz, w), 0, 1)

        k_cmp, v_cmp = _compress(seq("kvc"), nsa_pe[l], nsa_w1[l], nsa_w2[l])
        o_d = _nsa_attention(seq("dq"), seq("dqr"), seq("dg"), k_cmp, v_cmp,
                             seq("ks"), seq("vs"), seq("kw"), seq("vw"))

        k_mem, v_mem = _mem_kv(mem2, mem_norm_g[l], w_mem_kv[l])
        o_e = _mem_attention(seq("eq"), k_mem.reshape(bsz, -1, w), v_mem.reshape(bsz, -1, w))

        branches = [o.reshape(t, w) for o in (o_a, o_b, o_c, o_d, o_e)]
        x2 = _merge(x2, norm_g[l], branches, proj["zs"], w_merge[l], b_merge[l], w_branch[l], w_out[l],
                    final_g, final_norm=(l == depth - 1))
    return x2.reshape(bsz, s_len, d).astype(x.dtype)
```

```python
import functools
import math

import numpy as np
import jax
import jax.numpy as jnp
from jax import lax
from jax.experimental import pallas as pl
from jax.experimental.pallas import tpu as pltpu

F32 = jnp.float32
BF16 = jnp.bfloat16

HEAD_DIM = 64
BRANCH_WIDTH = 256
N_HEADS = 4
N_BRANCHES = 5
DIFF_QK_DIM = 32
DIL_PATTERNS = ((128, 1), (512, 4), (2048, 16))
S5_GROUP = 16
S5_GROUPS = 16
S5_STATE = 64
CMP_BLOCK = 32
CMP_STRIDE = 16
SLC_BLOCK = 64
N_SELECT = 16
WINDOW = 512
ROPE_THETA = 10000.0
RMS_EPS = 1e-6
NEG_INF = -1e30
FORCE_SCORE = 1e9

V7X_VMEM_BYTES = 64 * 1024 * 1024
BIG_VMEM_LIMIT = V7X_VMEM_BYTES - 8 * 1024 * 1024

LANES = 128
ATT_TILE = 256
ROW_TILE = 256
MEM_ATT_TILE = 1024
S5_CHUNK = 128

_NT = (((1,), (1,)), ((), ()))


def _rms(x, g):
    return x * lax.rsqrt(jnp.mean(x * x, axis=-1, keepdims=True) + RMS_EPS) * g


def _dot(a, b):
    return jnp.dot(a, b, preferred_element_type=F32)


def _dot_nt(a, b):
    return lax.dot_general(a, b, _NT, preferred_element_type=F32)


def _split_hi_lo(x):
    hi = x.astype(BF16)
    lo = (x - hi.astype(F32)).astype(BF16)
    return hi, lo


def _dot_hilo(x, w):
    hi, lo = _split_hi_lo(x)
    return _dot(hi, w) + _dot(lo, w)


def _const_spec(shape):
    n = len(shape)
    return pl.BlockSpec(shape, lambda *_: (0,) * n, pipeline_mode=pl.Buffered(1))


def _head_mask(h, dtype):
    lane = lax.broadcasted_iota(jnp.int32, (1, BRANCH_WIDTH), 1)
    return jnp.where((lane >= h * HEAD_DIM) & (lane < (h + 1) * HEAD_DIM), 1.0, 0.0).astype(dtype)


def _low_half():
    return lax.broadcasted_iota(jnp.int32, (1, LANES), 1) < HEAD_DIM


def _swap_halves(x):
    return pltpu.roll(x, HEAD_DIM, 1)


def _augment_heads(v):
    low = _low_half()
    parts = []
    for j in range(v.shape[1] // LANES):
        pair = v[:, j * LANES:(j + 1) * LANES]
        parts += [jnp.where(low, pair, 1.0), jnp.where(low, 1.0, pair)]
    return jnp.concatenate(parts, axis=1)


def _shared_kv_variants(kv):
    low = _low_half()
    sw = _swap_halves(kv)
    k2 = jnp.concatenate([jnp.where(low, kv, 0.0), jnp.where(low, 0.0, sw)], axis=1)
    v2 = jnp.concatenate([jnp.where(low, sw, 1.0), jnp.where(low, 1.0, kv)], axis=1)
    return k2, v2


def _normalise_pair(acc_even, acc_odd):
    return jnp.where(_low_half(), acc_even / _swap_halves(acc_even), acc_odd / _swap_halves(acc_odd))


_IN_COLS = (("qa", 256), ("ka", 256), ("va", 256), ("qb", 256), ("kb", 256), ("vb", 256), ("cu", 256),
            ("dq", 256), ("kvc", 128), ("kvs", 128), ("kvw", 128), ("dg", 128), ("eq", 256), ("zs", 1280))
_IN_OFFS = dict(zip([n for n, _ in _IN_COLS], np.cumsum([0] + [w for _, w in _IN_COLS])[:-1].tolist()))
_IN_WIDTH = dict(_IN_COLS)
_IN_OUTS = (("qa", 256, BF16), ("ka", 256, BF16), ("va", 512, BF16),
            ("qb", 256, BF16), ("kb", 256, BF16), ("vb", 512, BF16),
            ("cu", 256, F32), ("dq", 256, BF16), ("dqr", 256, BF16), ("kvc", 128, F32),
            ("ks", 256, BF16), ("vs", 256, BF16), ("kw", 256, BF16), ("vw", 256, BF16),
            ("dg", 128, F32), ("eq", 256, BF16), ("zs", 1280, F32))


def _rotate_half(y, group):
    half = group // 2
    lane = lax.broadcasted_iota(jnp.int32, (1, LANES), 1)
    first = (lane & (group - 1)) < half
    parts = []
    for j in range(y.shape[1] // LANES):
        v = y[:, j * LANES:(j + 1) * LANES]
        parts.append(jnp.where(first, pltpu.roll(v, LANES - half, 1), pltpu.roll(v, half, 1)))
    return jnp.concatenate(parts, axis=1)


def _in_proj_kernel(x_ref, g_ref, w_ref, cosa_ref, sina_ref, cosb_ref, sinb_ref, cosk_ref, sink_ref, *out_refs):
    out = dict(zip([n for n, _, _ in _IN_OUTS], out_refs))
    h = _rms(x_ref[...], g_ref[...]).astype(BF16)

    def proj(name):
        off = _IN_OFFS[name]
        return _dot(h, w_ref[:, off:off + _IN_WIDTH[name]])

    def rope(y, cos_ref, sin_ref, group):
        return y * cos_ref[...] + _rotate_half(y, group) * sin_ref[...]

    def put(name, y):
        out[name][...] = y.astype(out[name].dtype)

    put("qa", rope(proj("qa"), cosa_ref, sina_ref, DIFF_QK_DIM))
    put("ka", rope(proj("ka"), cosa_ref, sina_ref, DIFF_QK_DIM))
    put("va", _augment_heads(proj("va")))
    put("qb", rope(proj("qb"), cosb_ref, sinb_ref, HEAD_DIM))
    put("kb", rope(proj("kb"), cosb_ref, sinb_ref, HEAD_DIM))
    put("vb", _augment_heads(proj("vb")))
    put("cu", proj("cu"))
    dq = proj("dq")
    put("dq", dq)
    put("dqr", rope(dq, cosb_ref, sinb_ref, HEAD_DIM))
    put("kvc", proj("kvc"))
    for kv_name, k_name, v_name in (("kvs", "ks", "vs"), ("kvw", "kw", "vw")):
        k2, v2 = _shared_kv_variants(rope(proj(kv_name), cosk_ref, sink_ref, HEAD_DIM))
        put(k_name, k2)
        put(v_name, v2)
    put("dg", proj("dg"))
    put("eq", proj("eq"))
    z = proj("zs")
    put("zs", z * jax.nn.sigmoid(z))


def _in_weights(w):
    sizes = (256,) * 10 + (256, 64, 64, 64, 64, 64, 64, 12, 256, 256, 256)
    offs = np.cumsum((0,) + sizes)
    (a_q, a_k, a_v, a_z, b_q, b_k, b_v, b_z, c_u, c_z, d_q, d_kc, d_vc, d_ks, d_vs, d_kw, d_vw,
     d_g, d_z, e_q, e_z) = [w[:, offs[i]:offs[i + 1]] for i in range(len(sizes))]
    scale = HEAD_DIM ** -0.5
    cat = lambda *t: jnp.concatenate(t, axis=1)
    cols = [a_q, a_k, a_v, b_q * scale, b_k, b_v, c_u, d_q * scale, cat(d_kc, d_vc), cat(d_ks, d_vs),
            cat(d_kw, d_vw), jnp.pad(d_g, ((0, 0), (0, 128 - d_g.shape[1]))), e_q * scale,
            a_z, b_z, c_z, d_z, e_z]
    return cat(*cols).astype(BF16)


def _rope_tables(s_len):
    def table(group, width):
        half = group // 2
        inv_freq = ROPE_THETA ** (-jnp.arange(half, dtype=F32) / half)
        ang = jnp.arange(s_len, dtype=F32)[:, None] * inv_freq[None, :]
        cos = jnp.tile(jnp.cos(ang), (1, width // half))
        sin = jnp.tile(jnp.concatenate([-jnp.sin(ang), jnp.sin(ang)], axis=1), (1, width // group))
        return cos, sin
    cos_a, sin_a = table(DIFF_QK_DIM, BRANCH_WIDTH)
    cos_b, sin_b = table(HEAD_DIM, BRANCH_WIDTH)
    cos_k = jnp.concatenate([cos_b[:, :HEAD_DIM], jnp.ones((s_len, HEAD_DIM), F32)], axis=1)
    sin_k = jnp.concatenate([sin_b[:, :HEAD_DIM], jnp.zeros((s_len, HEAD_DIM), F32)], axis=1)
    return cos_a, sin_a, cos_b, sin_b, cos_k, sin_k


def _in_proj(x2, g, wcat, tables, s_len):
    t, d = x2.shape
    tm = ROW_TILE
    nsb = s_len // tm
    row = lambda w: pl.BlockSpec((tm, w), lambda i: (i, 0))
    tab = lambda a: pl.BlockSpec((tm, a.shape[1]), lambda i: (i % nsb, 0))
    return pl.pallas_call(
        _in_proj_kernel,
        grid=(t // tm,),
        in_specs=[row(d), _const_spec((1, d)), _const_spec(wcat.shape)] + [tab(a) for a in tables],
        out_specs=[row(w) for _, w, _ in _IN_OUTS],
        out_shape=[jax.ShapeDtypeStruct((t, w), dt) for _, w, dt in _IN_OUTS],
        compiler_params=pltpu.CompilerParams(dimension_semantics=("arbitrary",),
                                             vmem_limit_bytes=BIG_VMEM_LIMIT),
        name="in_proj",
    )(x2, g.reshape(1, d), wcat, *tables)


def _lane_fold(x, op):
    parts = [x[:, j * LANES:(j + 1) * LANES] for j in range(x.shape[1] // LANES)]
    return functools.reduce(op, parts)


def _strip_attention(chains, tq, kb_lo, kb_hi, bias_fn, last_bias, scratch, exp_scale=1.0):
    s_ref, m_ref, acc_ref = scratch
    n = len(chains)
    reps = tq // LANES

    def tile(cache, ref, col, width, kb, ntiles=1):
        key = (id(ref), col, width)
        if key not in cache:
            cache[key] = ref[pl.ds(pl.multiple_of(kb * tq, tq), ntiles * tq), col:col + width]
        return cache[key]

    def scores(kb, extra):
        bias = bias_fn(kb) if bias_fn is not None else None
        if extra is not None:
            bias = extra if bias is None else bias + extra
        cache, out = {}, []
        for q, (k_ref, k_col), _ in chains:
            s = _dot_nt(q, tile(cache, k_ref, k_col, q.shape[1], kb))
            out.append(s if bias is None else s + bias)
        return out

    def tile_pairs(lo, hi, step):
        def two(j, carry):
            step(lo + 2 * j, 2)
            return carry
        cnt = hi - lo
        lax.fori_loop(0, lax.shift_right_arithmetic(cnt, 1), two, 0)

        @pl.when((cnt & 1) == 1)
        def _():
            step(hi - 1, 1)

    m_ref[0:n] = jnp.full((n,) + m_ref.shape[1:], NEG_INF, F32)

    def pass1(kb, ntiles):
        tiles = [scores(kb + j, None) for j in range(ntiles)]
        for i in range(n):
            m = m_ref[i]
            for j in range(ntiles):
                s_ref[i, kb + j] = tiles[j][i]
                m = jnp.maximum(m, _lane_fold(tiles[j][i], jnp.maximum))
            m_ref[i] = m

    tile_pairs(kb_lo, kb_hi - 1, pass1)
    for i, s in enumerate(scores(kb_hi - 1, last_bias)):
        s_ref[i, kb_hi - 1] = s
        m = jnp.max(jnp.maximum(m_ref[i], _lane_fold(s, jnp.maximum)), axis=-1, keepdims=True)
        m_ref[i] = jnp.broadcast_to(m, m_ref.shape[1:])
        acc_ref[i] = jnp.zeros(acc_ref.shape[1:], F32)

    c1 = exp_scale * math.log2(math.e)

    def pass2(kb, ntiles):
        cache = {}
        for i, (_, _, (v_ref, v_col)) in enumerate(chains):
            m = jnp.concatenate([m_ref[i]] * reps, axis=1)
            p = [jnp.exp2((s_ref[i, kb + j] - m) * c1).astype(BF16) for j in range(ntiles)]
            p = p[0] if ntiles == 1 else jnp.concatenate(p, axis=1)
            acc_ref[i] += _dot(p, tile(cache, v_ref, v_col, LANES, kb, ntiles))

    tile_pairs(kb_lo, kb_hi, pass2)
    return [acc_ref[i] for i in range(n)]


def _strip_scratch(n, tq, s_len):
    return [pltpu.VMEM((n, s_len // tq, tq, tq), F32), pltpu.VMEM((n, tq, LANES), F32),
            pltpu.VMEM((n, tq, LANES), F32)]


def _causal_bias(tq):
    r = lax.broadcasted_iota(jnp.int32, (tq, tq), 0)
    c = lax.broadcasted_iota(jnp.int32, (tq, tq), 1)
    return jnp.where(c <= r, 0.0, NEG_INF).astype(F32)


def _att_specs(tq, s_len, k_width, v_width):
    q_spec = pl.BlockSpec((None, tq, BRANCH_WIDTH), lambda b, i: (b, i, 0))
    kv = lambda w: pl.BlockSpec((None, s_len, w), lambda b, i: (b, 0, 0))
    return q_spec, kv(k_width), kv(v_width)


def _diff_kernel(lam_ref, q_ref, k_ref, v_ref, g_ref, hm_ref, o_ref, *scratch, tq, out_scale):
    qi = pl.program_id(1)
    q = q_ref[...]
    lane = lax.broadcasted_iota(jnp.int32, (1, BRANCH_WIDTH), 1)
    lam = lam_ref[0]
    chains = []
    for hc in range(2 * N_HEADS):
        lo = hc * DIFF_QK_DIM
        cmask = jnp.where((lane >= lo) & (lane < lo + DIFF_QK_DIM), 1.0, 0.0).astype(BF16)
        chains.append((q * cmask, (k_ref, 0), (v_ref, (hc // 2) * LANES)))
    acc = _strip_attention(chains, tq, 0, qi + 1, None, _causal_bias(tq), scratch,
                           exp_scale=DIFF_QK_DIM ** -0.5)
    halves = []
    for pair in range(N_HEADS // 2):
        even, odd = 4 * pair, 4 * pair + 2
        halves.append(_normalise_pair(acc[even], acc[odd]) - lam * _normalise_pair(acc[even + 1], acc[odd + 1]))
    o = jnp.concatenate(halves, axis=1)
    ms = _dot_hilo(o * o, hm_ref[...])
    o_ref[...] = o * lax.rsqrt(ms + RMS_EPS) * g_ref[...] * out_scale


def _diff_attention(q, k, v_aug, lam, subln_g, lam_init):
    b, s_len, w = q.shape
    tq = ATT_TILE
    q_spec, k_spec, v_spec = _att_specs(tq, s_len, w, v_aug.shape[-1])
    head = np.arange(w) // HEAD_DIM
    hm = jnp.asarray((head[:, None] == head[None, :]) / HEAD_DIM, dtype=BF16)
    g = jnp.tile(subln_g.astype(F32), N_HEADS).reshape(1, w)
    return pl.pallas_call(
        functools.partial(_diff_kernel, tq=tq, out_scale=1.0 - lam_init),
        grid=(b, s_len // tq),
        in_specs=[pl.BlockSpec(memory_space=pltpu.SMEM), q_spec, k_spec, v_spec,
                  _const_spec((1, w)), _const_spec((w, w))],
        out_specs=q_spec,
        out_shape=jax.ShapeDtypeStruct((b, s_len, w), F32),
        scratch_shapes=_strip_scratch(2 * N_HEADS, tq, s_len),
        compiler_params=pltpu.CompilerParams(dimension_semantics=("arbitrary", "arbitrary"),
                                             vmem_limit_bytes=BIG_VMEM_LIMIT),
        name="diff_attention",
    )(lam.reshape(1), q, k, v_aug, g, hm)


def _dil_kernel(q_ref, k_ref, v_ref, bias_ref, o_ref, *scratch, tq):
    qi = pl.program_id(1)
    q = q_ref[...]
    chains = [(q * _head_mask(h, BF16), (k_ref, 0), (v_ref, h * LANES)) for h in range(N_HEADS)]
    acc = _strip_attention(chains, tq, 0, qi + 1, lambda kb: bias_ref[qi - kb], None, scratch)
    o_ref[...] = jnp.concatenate([_normalise_pair(acc[0], acc[1]), _normalise_pair(acc[2], acc[3])], axis=1)


def _dilated_bias(tq, s_len):
    nq = s_len // tq
    d = (np.arange(nq)[:, None, None] * tq + np.arange(tq)[None, :, None] - np.arange(tq)[None, None, :])
    count = np.zeros(d.shape, np.float64)
    for window, dil in DIL_PATTERNS:
        count += (d >= 0) & (d <= window) & (d % dil == 0)
    return np.where(count > 0, np.log(np.maximum(count, 1.0)), NEG_INF).astype(np.float32)


def _dilated_attention(q, k, v_aug):
    b, s_len, w = q.shape
    tq = ATT_TILE
    q_spec, k_spec, v_spec = _att_specs(tq, s_len, w, v_aug.shape[-1])
    bias = jnp.asarray(_dilated_bias(tq, s_len))
    return pl.pallas_call(
        functools.partial(_dil_kernel, tq=tq),
        grid=(b, s_len // tq),
        in_specs=[q_spec, k_spec, v_spec, _const_spec(bias.shape)],
        out_specs=q_spec,
        out_shape=jax.ShapeDtypeStruct((b, s_len, w), F32),
        scratch_shapes=_strip_scratch(N_HEADS, tq, s_len),
        compiler_params=pltpu.CompilerParams(dimension_semantics=("arbitrary", "arbitrary"),
                                             vmem_limit_bytes=BIG_VMEM_LIMIT),
        name="dilated_attention",
    )(q, k, v_aug, bias)


def _s5_kernel(u_ref, bm_ref, cm_ref, are_ref, aim_ref, d_ref, wg_ref, bg_ref, o_ref, xs_ref, st_ref,
               *, ts, nb):
    n = S5_GROUPS * S5_STATE

    @pl.when(pl.program_id(0) == 0)
    def _():
        st_ref[...] = jnp.zeros(st_ref.shape, F32)

    u = u_ref[...]
    xs_ref[...] = _dot(u.astype(BF16), bm_ref[...])
    a_re = jnp.broadcast_to(are_ref[...], (nb, n))
    a_im = jnp.broadcast_to(aim_ref[...], (nb, n))

    def step(t, carry):
        x_re, x_im = carry
        r0 = pl.multiple_of(t * nb, nb)
        n_re = a_re * x_re - a_im * x_im + xs_ref[pl.ds(r0, nb), 0:n]
        n_im = a_re * x_im + a_im * x_re + xs_ref[pl.ds(r0, nb), n:2 * n]
        xs_ref[pl.ds(r0, nb), 0:n] = n_re
        xs_ref[pl.ds(r0, nb), n:2 * n] = n_im
        return n_re, n_im

    x_re, x_im = lax.fori_loop(0, ts, step, (st_ref[:, 0:n], st_ref[:, n:2 * n]))
    st_ref[:, 0:n] = x_re
    st_ref[:, n:2 * n] = x_im

    y = _dot(xs_ref[...].astype(BF16), cm_ref[...]) + d_ref[...] * u
    t = _dot(jax.nn.gelu(y).astype(BF16), wg_ref[...]) + bg_ref[...]
    o_ref[...] = t[:, :BRANCH_WIDTH] * jax.nn.sigmoid(t[:, BRANCH_WIDTH:])


def _s5_params(lam_re, lam_im, log_dt, b_re, b_im, c_re, c_im):
    g, n, p = S5_GROUPS, S5_STATE, S5_GROUP
    lr, li = lam_re.astype(F32), lam_im.astype(F32)
    dt = jnp.exp(log_dt.astype(F32))[:, None]
    mag = jnp.exp(lr * dt)
    a_re, a_im = mag * jnp.cos(li * dt), mag * jnp.sin(li * dt)
    den = lr * lr + li * li
    n_re, n_im = a_re - 1.0, a_im
    z_re = (n_re * lr + n_im * li) / den
    z_im = (n_im * lr - n_re * li) / den
    br, bi = b_re.astype(F32), b_im.astype(F32)
    bb_re = z_re[..., None] * br - z_im[..., None] * bi
    bb_im = z_re[..., None] * bi + z_im[..., None] * br
    eye = jnp.eye(g, dtype=F32)
    blockdiag_in = lambda t: jnp.einsum("gnp,gh->gphn", t, eye).reshape(g * p, g * n)
    blockdiag_out = lambda t: jnp.einsum("gpn,gh->gnhp", t, eye).reshape(g * n, g * p)
    bm = jnp.concatenate([blockdiag_in(bb_re), blockdiag_in(bb_im)], axis=1)
    cm = jnp.concatenate([blockdiag_out(c_re.astype(F32)), -blockdiag_out(c_im.astype(F32))], axis=0)
    return bm.astype(BF16), cm.astype(BF16), a_re.reshape(1, g * n), a_im.reshape(1, g * n)


def _s5_branch(u_tb, nb, params, d_skip, w_glu, b_glu):
    rows, w = u_tb.shape
    ts = S5_CHUNK
    bm, cm, a_re, a_im = params
    n2 = bm.shape[1]
    blk = pl.BlockSpec((ts * nb, w), lambda i: (i, 0))
    return pl.pallas_call(
        functools.partial(_s5_kernel, ts=ts, nb=nb),
        grid=(rows // (ts * nb),),
        in_specs=[blk, _const_spec(bm.shape), _const_spec(cm.shape), _const_spec(a_re.shape),
                  _const_spec(a_im.shape), _const_spec((1, w)), _const_spec(w_glu.shape),
                  _const_spec((1, 2 * w))],
        out_specs=blk,
        out_shape=jax.ShapeDtypeStruct((rows, w), F32),
        scratch_shapes=[pltpu.VMEM((ts * nb, n2), F32), pltpu.VMEM((nb, n2), F32)],
        compiler_params=pltpu.CompilerParams(dimension_semantics=("arbitrary",),
                                             vmem_limit_bytes=BIG_VMEM_LIMIT),
        name="s5_scan",
    )(u_tb, bm, cm, a_re, a_im, d_skip.astype(F32).reshape(1, w), w_glu.astype(BF16),
      b_glu.astype(F32).reshape(1, 2 * w))


def _compress_kernel(r_ref, pe_top_ref, pe_bot_ref, w_top_ref, w_bot_ref, w2_ref, k_ref, v_ref):
    r = r_ref[...]
    top = (r + pe_top_ref[...]).astype(BF16)
    nxt = pltpu.roll(r, r.shape[0] - 1, 0)
    bot = (nxt + pe_bot_ref[...]).astype(BF16)
    hid = jax.nn.gelu(_dot(top, w_top_ref[...]) + _dot(bot, w_bot_ref[...]))
    k2, v2 = _shared_kv_variants(_dot(hid.astype(BF16), w2_ref[...]))
    k_ref[...] = k2.astype(BF16)
    v_ref[...] = v2.astype(BF16)


def _compress(kvc, pe, w1, w2):
    b, s_len, _ = kvc.shape
    nr = s_len // CMP_STRIDE
    per = CMP_BLOCK // CMP_STRIDE
    assert per == 2
    hid = w1.shape[-1]
    r = kvc.reshape(b, nr, CMP_STRIDE * 2 * HEAD_DIM)
    w1r = w1.astype(F32).reshape(2, per, CMP_STRIDE, HEAD_DIM, hid)
    per_r = pe.astype(F32).reshape(2, per, CMP_STRIDE, HEAD_DIM)

    def expand(j):
        wk = jnp.pad(w1r[0, j], ((0, 0), (0, HEAD_DIM), (0, hid)))
        wv = jnp.pad(w1r[1, j], ((0, 0), (HEAD_DIM, 0), (hid, 0)))
        return (wk + wv).reshape(CMP_STRIDE * 2 * HEAD_DIM, 2 * hid).astype(BF16)

    pe_rows = [jnp.concatenate([per_r[0, j], per_r[1, j]], axis=-1).reshape(1, -1) for j in range(per)]
    z = jnp.zeros((hid, HEAD_DIM), F32)
    w2f = w2.astype(F32)
    w2x = jnp.concatenate([jnp.concatenate([w2f[0], z], axis=1),
                           jnp.concatenate([z, w2f[1]], axis=1)], axis=0).astype(BF16)
    w_top, w_bot = expand(0), expand(1)
    blk = pl.BlockSpec((None, nr, r.shape[-1]), lambda i: (i, 0, 0))
    out = pl.BlockSpec((None, nr, BRANCH_WIDTH), lambda i: (i, 0, 0))
    return pl.pallas_call(
        _compress_kernel,
        grid=(b,),
        in_specs=[blk, _const_spec(pe_rows[0].shape), _const_spec(pe_rows[1].shape),
                  _const_spec(w_top.shape), _const_spec(w_bot.shape), _const_spec(w2x.shape)],
        out_specs=[out, out],
        out_shape=[jax.ShapeDtypeStruct((b, nr, BRANCH_WIDTH), BF16)] * 2,
        compiler_params=pltpu.CompilerParams(dimension_semantics=("arbitrary",)),
        name="nsa_compress",
    )(r, pe_rows[0], pe_rows[1], w_top, w_bot, w2x)


def _nsa_kernel(dq_ref, dqr_ref, dg_ref, kc_ref, vc_ref, ks_ref, vs_ref, kw_ref, vw_ref,
                ovt_ref, exp_ref, gsel_ref, wbias_ref, o_ref, *scratch, tq, n_cmp, n_slc, n_sel):
    qi = pl.program_id(1)
    t0 = qi * tq
    nr = kc_ref.shape[0]
    low = _low_half()
    q_cols = lambda h: slice((h // 2) * LANES, (h // 2 + 1) * LANES)
    kv_col = lambda h: (h % 2) * LANES

    dq = dq_ref[...]
    tpos = t0 + lax.broadcasted_iota(jnp.int32, (tq, nr), 0)
    ci = lax.broadcasted_iota(jnp.int32, (tq, nr), 1)
    cmask = (ci * CMP_STRIDE + (CMP_BLOCK - 1) <= tpos) & (ci < n_cmp)
    p_sum = jnp.zeros((tq, nr), F32)
    o_heads = []
    for h in range(N_HEADS):
        s = jnp.where(cmask, _dot_nt(dq[:, q_cols(h)], kc_ref[:, kv_col(h):kv_col(h) + LANES]), NEG_INF)
        m = jnp.max(s, axis=-1, keepdims=True)
        e = jnp.where(cmask, jnp.exp(s - m), 0.0)
        p = e / jnp.maximum(jnp.sum(e, axis=-1, keepdims=True), 1e-30)
        p_sum = p_sum + p
        o_heads.append(_dot(p.astype(BF16), vc_ref[:, kv_col(h):kv_col(h) + LANES]))
    o_cmp = jnp.concatenate([jnp.where(low, o_heads[0], o_heads[1]), jnp.where(low, o_heads[2], o_heads[3])],
                            axis=1)

    p_hi, p_lo = _split_hi_lo(p_sum)
    ovt = ovt_ref[...]
    imp_t = _dot_nt(ovt, p_hi) + _dot_nt(ovt, p_lo)
    blk = lax.broadcasted_iota(jnp.int32, (n_slc, tq), 0)
    qblk = lax.shift_right_arithmetic(t0 + lax.broadcasted_iota(jnp.int32, (n_slc, tq), 1),
                                      int(math.log2(SLC_BLOCK)))
    forced = jnp.where(blk == 0, 1, jnp.where(blk == qblk, 1, jnp.where(blk == qblk - 1, 1, 0)))
    score = jnp.where(blk <= qblk, jnp.where(forced > 0, FORCE_SCORE, imp_t), NEG_INF)
    rank = jnp.zeros((n_slc, tq), F32)
    for i in range(n_slc):
        si = score[i:i + 1, :]
        tie = jnp.where(blk > i, 1.0, 0.0)
        rank = rank + jnp.where(si > score, 1.0, jnp.where(si == score, tie, 0.0))
    sel_bias_t = jnp.where(rank < n_sel, jnp.where(score > 0.5 * NEG_INF, 0.0, NEG_INF), NEG_INF)
    pad_rows = exp_ref.shape[1] - n_slc
    sel_bias = jnp.concatenate([sel_bias_t, jnp.full((pad_rows, tq), NEG_INF, F32)], axis=0).T.astype(BF16)

    dqr = dqr_ref[...]
    pairs = lambda acc: jnp.concatenate([_normalise_pair(acc[0], acc[1]), _normalise_pair(acc[2], acc[3])], axis=1)
    chains = [(dqr[:, q_cols(h)], (ks_ref, kv_col(h)), (vs_ref, kv_col(h))) for h in range(N_HEADS)]
    o_slc = pairs(_strip_attention(chains, tq, 0, qi + 1, lambda kb: _dot(sel_bias, exp_ref[kb]),
                                   _causal_bias(tq), scratch))
    chains = [(dqr[:, q_cols(h)], (kw_ref, kv_col(h)), (vw_ref, kv_col(h))) for h in range(N_HEADS)]
    win_lo = jnp.maximum(qi - (wbias_ref.shape[0] - 1), 0)
    o_win = pairs(_strip_attention(chains, tq, win_lo, qi + 1, lambda kb: wbias_ref[qi - kb], None, scratch))

    gates = _dot_hilo(jax.nn.sigmoid(dg_ref[...]), gsel_ref[...])
    w = BRANCH_WIDTH
    o_ref[...] = gates[:, :w] * o_cmp + gates[:, w:2 * w] * o_slc + gates[:, 2 * w:] * o_win


def _nsa_constants(tq, s_len):
    n_cmp = (s_len - CMP_BLOCK) // CMP_STRIDE + 1
    n_slc = s_len // SLC_BLOCK
    nr = s_len // CMP_STRIDE
    c0 = np.arange(n_cmp)[:, None] * CMP_STRIDE
    s0 = np.arange(n_slc)[None, :] * SLC_BLOCK
    overlap = np.clip(np.minimum(c0 + CMP_BLOCK, s0 + SLC_BLOCK) - np.maximum(c0, s0), 0, None) / CMP_STRIDE
    ovt = np.zeros((n_slc, nr), np.float32)
    ovt[:, :n_cmp] = overlap.T
    rows = -(-n_slc // 128) * 128
    expand = np.zeros((s_len // tq, rows, tq), np.float32)
    tok = np.arange(s_len)
    expand[tok // tq, tok // SLC_BLOCK, tok % tq] = 1.0
    gsel = np.zeros((128, 3 * BRANCH_WIDTH), np.float32)
    for h in range(N_HEADS):
        for j in range(3):
            gsel[h * 3 + j, j * BRANCH_WIDTH + h * HEAD_DIM:j * BRANCH_WIDTH + (h + 1) * HEAD_DIM] = 1.0
    nwin = -(-WINDOW // tq) + 1
    d = np.arange(nwin)[:, None, None] * tq + np.arange(tq)[None, :, None] - np.arange(tq)[None, None, :]
    wbias = np.where((d >= 0) & (d < WINDOW), 0.0, NEG_INF).astype(np.float32)
    return n_cmp, n_slc, ovt, expand, gsel, wbias


def _nsa_attention(dq, dqr, dg, k_cmp, v_cmp, ks, vs, kw, vw):
    b, s_len, w = dq.shape
    tq = ATT_TILE
    n_cmp, n_slc, ovt, expand, gsel, wbias = _nsa_constants(tq, s_len)
    q_spec, kv_spec, _ = _att_specs(tq, s_len, w, w)
    nr = k_cmp.shape[1]
    cmp_spec = pl.BlockSpec((None, nr, w), lambda bi, i: (bi, 0, 0))
    g_spec = pl.BlockSpec((None, tq, dg.shape[-1]), lambda bi, i: (bi, i, 0))
    consts = [jnp.asarray(ovt, BF16), jnp.asarray(expand, BF16), jnp.asarray(gsel, BF16), jnp.asarray(wbias)]
    return pl.pallas_call(
        functools.partial(_nsa_kernel, tq=tq, n_cmp=n_cmp, n_slc=n_slc, n_sel=min(N_SELECT, n_slc)),
        grid=(b, s_len // tq),
        in_specs=[q_spec, q_spec, g_spec, cmp_spec, cmp_spec, kv_spec, kv_spec, kv_spec, kv_spec]
        + [_const_spec(c.shape) for c in consts],
        out_specs=q_spec,
        out_shape=jax.ShapeDtypeStruct((b, s_len, w), F32),
        scratch_shapes=_strip_scratch(N_HEADS, tq, s_len),
        compiler_params=pltpu.CompilerParams(dimension_semantics=("arbitrary", "arbitrary"),
                                             vmem_limit_bytes=BIG_VMEM_LIMIT),
        name="nsa_attention",
    )(dq, dqr, dg, k_cmp, v_cmp, ks, vs, kw, vw, *consts)


def _mem_kv_kernel(m_ref, g_ref, w_ref, k_ref, v_ref):
    kv = _dot(_rms(m_ref[...], g_ref[...]).astype(BF16), w_ref[...])
    k_ref[...] = kv[:, :BRANCH_WIDTH].astype(BF16)
    v_ref[...] = kv[:, BRANCH_WIDTH:].astype(BF16)


def _mem_kv(mem2, g, w_kv):
    rows, d = mem2.shape
    tm = ROW_TILE
    out = pl.BlockSpec((tm, BRANCH_WIDTH), lambda i: (i, 0))
    return pl.pallas_call(
        _mem_kv_kernel,
        grid=(rows // tm,),
        in_specs=[pl.BlockSpec((tm, d), lambda i: (i, 0)), _const_spec((1, d)), _const_spec(w_kv.shape)],
        out_specs=[out, out],
        out_shape=[jax.ShapeDtypeStruct((rows, BRANCH_WIDTH), BF16)] * 2,
        compiler_params=pltpu.CompilerParams(dimension_semantics=("arbitrary",)),
        name="mem_kv",
    )(mem2, g.astype(F32).reshape(1, d), w_kv.astype(BF16))


def _mem_attn_kernel(q_ref, k_ref, v_ref, o_ref):
    q = q_ref[...]
    k = k_ref[...]
    v = v_ref[...]
    o = jnp.zeros(o_ref.shape, F32)
    for h in range(N_HEADS):
        s = _dot_nt(q * _head_mask(h, BF16), k)
        e = jnp.exp(s - jnp.max(s, axis=-1, keepdims=True))
        oh = _dot(e.astype(BF16), v) / jnp.sum(e, axis=-1, keepdims=True)
        o = jnp.where(_head_mask(h, F32) > 0.5, oh, o)
    o_ref[...] = o


def _mem_attention(q, k, v):
    b, s_len, w = q.shape
    tq = MEM_ATT_TILE
    q_spec = pl.BlockSpec((None, tq, w), lambda bi, i: (bi, i, 0))
    kv_spec = pl.BlockSpec((None, k.shape[1], w), lambda bi, i: (bi, 0, 0))
    return pl.pallas_call(
        _mem_attn_kernel,
        grid=(b, s_len // tq),
        in_specs=[q_spec, kv_spec, kv_spec],
        out_specs=q_spec,
        out_shape=jax.ShapeDtypeStruct((b, s_len, w), F32),
        compiler_params=pltpu.CompilerParams(dimension_semantics=("arbitrary", "arbitrary")),
        name="mem_attention",
    )(q, k, v)


def _merge_kernel(x_ref, g_ref, oa_ref, ob_ref, oc_ref, od_ref, oe_ref, zs_ref, wm_ref, bm_ref, wb_ref,
                  wo_ref, fg_ref, o_ref, *, final_norm):
    x = x_ref[...]
    d = x.shape[-1]
    w = BRANCH_WIDTH
    h = _rms(x, g_ref[...]).astype(BF16)
    mixed = jnp.zeros(x.shape, F32)
    for n, br_ref in enumerate((oa_ref, ob_ref, oc_ref, od_ref, oe_ref)):
        br = (br_ref[...] * zs_ref[:, n * w:(n + 1) * w]).astype(BF16)
        y = _dot(br, wb_ref[n])
        gate = jax.nn.sigmoid(_dot(h, wm_ref[:, n * d:(n + 1) * d]) + bm_ref[:, n * d:(n + 1) * d])
        mixed = mixed + gate * y
    out = x + _dot(mixed.astype(BF16), wo_ref[...])
    if final_norm:
        out = _rms(out, fg_ref[...])
    o_ref[...] = out


def _merge(x2, g, branches, zs, w_merge, b_merge, w_branch, w_out, final_g, final_norm):
    t, d = x2.shape
    tm = ROW_TILE
    row = lambda wd: pl.BlockSpec((tm, wd), lambda i: (i, 0))
    return pl.pallas_call(
        functools.partial(_merge_kernel, final_norm=final_norm),
        grid=(t // tm,),
        in_specs=[row(d), _const_spec((1, d))] + [row(BRANCH_WIDTH)] * N_BRANCHES + [row(zs.shape[1])]
        + [_const_spec(w_merge.shape), _const_spec((1, N_BRANCHES * d)), _const_spec(w_branch.shape),
           _const_spec(w_out.shape), _const_spec((1, d))],
        out_specs=row(d),
        out_shape=jax.ShapeDtypeStruct((t, d), F32),
        compiler_params=pltpu.CompilerParams(dimension_semantics=("arbitrary",),
                                             vmem_limit_bytes=BIG_VMEM_LIMIT),
        name="merge",
    )(x2, g.astype(F32).reshape(1, d), *branches, zs, w_merge.astype(BF16),
      b_merge.astype(F32).reshape(1, -1), w_branch.astype(BF16), w_out.astype(BF16),
      final_g.astype(F32).reshape(1, d))


def kernel(x, mem, norm_g, w_in, diff_lambda, diff_subln_g, s5_lambda_re, s5_lambda_im, s5_log_dt,
           s5_b_re, s5_b_im, s5_c_re, s5_c_im, s5_d, w_glu, b_glu, nsa_pe, nsa_w1, nsa_w2, mem_norm_g,
           w_mem_kv, w_merge, b_merge, w_branch, w_out, final_g):
    bsz, s_len, d = x.shape
    depth = w_in.shape[0]
    t = bsz * s_len
    w = BRANCH_WIDTH
    tables = _rope_tables(s_len)
    x2 = x.astype(F32).reshape(t, d)
    mem2 = mem.astype(F32).reshape(-1, d)
    for l in range(depth):
        proj = dict(zip([n for n, _, _ in _IN_OUTS],
                        _in_proj(x2, norm_g[l].astype(F32), _in_weights(w_in[l].astype(F32)), tables, s_len)))
        seq = lambda name: proj[name].reshape(bsz, s_len, -1)

        dl = diff_lambda[l].astype(F32)
        lam_init = 0.8 - 0.6 * math.exp(-0.3 * l)
        lam = jnp.exp(jnp.sum(dl[0] * dl[1])) - jnp.exp(jnp.sum(dl[2] * dl[3])) + lam_init
        o_a = _diff_attention(seq("qa"), seq("ka"), seq("va"), lam, diff_subln_g[l], lam_init)

        o_b = _dilated_attention(seq("qb"), seq("kb"), seq("vb"))

        u_tb = jnp.swapaxes(seq("cu"), 0, 1).reshape(t, w)
        s5p = _s5_params(s5_lambda_re[l], s5_lambda_im[l], s5_log_dt[l], s5_b_re[l], s5_b_im[l],
                         s5_c_re[l], s5_c_im[l])
        o_c = _s5_branch(u_tb, bsz, s5p, s5_d[l], w_glu[l], b_glu[l])
        o_c = jnp.swapaxes(o_c.reshape(s_len, bsz, w), 0, 1)

        k_cmp, v_cmp = _compress(seq("kvc"), nsa_pe[l], nsa_w1[l], nsa_w2[l])
        o_d = _nsa_attention(seq("dq"), seq("dqr"), seq("dg"), k_cmp, v_cmp,
                             seq("ks"), seq("vs"), seq("kw"), seq("vw"))

        k_mem, v_mem = _mem_kv(mem2, mem_norm_g[l], w_mem_kv[l])
        o_e = _mem_attention(seq("eq"), k_mem.reshape(bsz, -1, w), v_mem.reshape(bsz, -1, w))

        branches = [o.reshape(t, w) for o in (o_a, o_b, o_c, o_d, o_e)]
        x2 = _merge(x2, norm_g[l], branches, proj["zs"], w_merge[l], b_merge[l], w_branch[l], w_out[l],
                    final_g, final_norm=(l == depth - 1))
    return x2.reshape(bsz, s_len, d).astype(x.dtype)
```

```python
import functools
import math

import numpy as np
import jax
import jax.numpy as jnp
from jax import lax
from jax.experimental import pallas as pl
from jax.experimental.pallas import tpu as pltpu

F32 = jnp.float32
BF16 = jnp.bfloat16

HEAD_DIM = 64
BRANCH_WIDTH = 256
N_HEADS = 4
N_BRANCHES = 5
DIFF_QK_DIM = 32
DIL_PATTERNS = ((128, 1), (512, 4), (2048, 16))
S5_GROUP = 16
S5_GROUPS = 16
S5_STATE = 64
CMP_BLOCK = 32
CMP_STRIDE = 16
SLC_BLOCK = 64
N_SELECT = 16
WINDOW = 512
ROPE_THETA = 10000.0
RMS_EPS = 1e-6
NEG_INF = -1e30
FORCE_SCORE = 1e9

V7X_VMEM_BYTES = 64 * 1024 * 1024
BIG_VMEM_LIMIT = V7X_VMEM_BYTES - 8 * 1024 * 1024

LANES = 128
ATT_TILE = 256
ROW_TILE = 256
MEM_ATT_TILE = 1024
S5_CHUNK = 128

_NT = (((1,), (1,)), ((), ()))


def _rms(x, g):
    return x * lax.rsqrt(jnp.mean(x * x, axis=-1, keepdims=True) + RMS_EPS) * g


def _dot(a, b):
    return jnp.dot(a, b, preferred_element_type=F32)


def _dot_nt(a, b):
    return lax.dot_general(a, b, _NT, preferred_element_type=F32)


def _split_hi_lo(x):
    hi = x.astype(BF16)
    lo = (x - hi.astype(F32)).astype(BF16)
    return hi, lo


def _dot_hilo(x, w):
    hi, lo = _split_hi_lo(x)
    return _dot(hi, w) + _dot(lo, w)


def _const_spec(shape):
    n = len(shape)
    return pl.BlockSpec(shape, lambda *_: (0,) * n, pipeline_mode=pl.Buffered(1))


def _head_mask(h, dtype):
    lane = lax.broadcasted_iota(jnp.int32, (1, BRANCH_WIDTH), 1)
    return jnp.where((lane >= h * HEAD_DIM) & (lane < (h + 1) * HEAD_DIM), 1.0, 0.0).astype(dtype)


def _low_half():
    return lax.broadcasted_iota(jnp.int32, (1, LANES), 1) < HEAD_DIM


def _swap_halves(x):
    return pltpu.roll(x, HEAD_DIM, 1)


def _augment_heads(v):
    low = _low_half()
    parts = []
    for j in range(v.shape[1] // LANES):
        pair = v[:, j * LANES:(j + 1) * LANES]
        parts += [jnp.where(low, pair, 1.0), jnp.where(low, 1.0, pair)]
    return jnp.concatenate(parts, axis=1)


def _shared_kv_variants(kv):
    low = _low_half()
    sw = _swap_halves(kv)
    k2 = jnp.concatenate([jnp.where(low, kv, 0.0), jnp.where(low, 0.0, sw)], axis=1)
    v2 = jnp.concatenate([jnp.where(low, sw, 1.0), jnp.where(low, 1.0, kv)], axis=1)
    return k2, v2


def _normalise_pair(acc_even, acc_odd):
    return jnp.where(_low_half(), acc_even / _swap_halves(acc_even), acc_odd / _swap_halves(acc_odd))


_IN_COLS = (("qa", 256), ("ka", 256), ("va", 256), ("qb", 256), ("kb", 256), ("vb", 256), ("cu", 256),
            ("dq", 256), ("kvc", 128), ("kvs", 128), ("kvw", 128), ("dg", 128), ("eq", 256), ("zs", 1280))
_IN_OFFS = dict(zip([n for n, _ in _IN_COLS], np.cumsum([0] + [w for _, w in _IN_COLS])[:-1].tolist()))
_IN_WIDTH = dict(_IN_COLS)
_IN_OUTS = (("qa", 256, BF16), ("ka", 256, BF16), ("va", 512, BF16),
            ("qb", 256, BF16), ("kb", 256, BF16), ("vb", 512, BF16),
            ("cu", 256, F32), ("dq", 256, BF16), ("dqr", 256, BF16), ("kvc", 128, F32),
            ("ks", 256, BF16), ("vs", 256, BF16), ("kw", 256, BF16), ("vw", 256, BF16),
            ("dg", 128, F32), ("eq", 256, BF16), ("zs", 1280, F32))


def _rotate_half(y, group):
    half = group // 2
    lane = lax.broadcasted_iota(jnp.int32, (1, LANES), 1)
    first = (lane & (group - 1)) < half
    parts = []
    for j in range(y.shape[1] // LANES):
        v = y[:, j * LANES:(j + 1) * LANES]
        parts.append(jnp.where(first, pltpu.roll(v, LANES - half, 1), pltpu.roll(v, half, 1)))
    return jnp.concatenate(parts, axis=1)


def _in_proj_kernel(x_ref, g_ref, w_ref, cosa_ref, sina_ref, cosb_ref, sinb_ref, cosk_ref, sink_ref, *out_refs):
    out = dict(zip([n for n, _, _ in _IN_OUTS], out_refs))
    h = _rms(x_ref[...], g_ref[...]).astype(BF16)

    def proj(name):
        off = _IN_OFFS[name]
        return _dot(h, w_ref[:, off:off + _IN_WIDTH[name]])

    def rope(y, cos_ref, sin_ref, group):
        return y * cos_ref[...] + _rotate_half(y, group) * sin_ref[...]

    def put(name, y):
        out[name][...] = y.astype(out[name].dtype)

    put("qa", rope(proj("qa"), cosa_ref, sina_ref, DIFF_QK_DIM))
    put("ka", rope(proj("ka"), cosa_ref, sina_ref, DIFF_QK_DIM))
    put("va", _augment_heads(proj("va")))
    put("qb", rope(proj("qb"), cosb_ref, sinb_ref, HEAD_DIM))
    put("kb", rope(proj("kb"), cosb_ref, sinb_ref, HEAD_DIM))
    put("vb", _augment_heads(proj("vb")))
    put("cu", proj("cu"))
    dq = proj("dq")
    put("dq", dq)
    put("dqr", rope(dq, cosb_ref, sinb_ref, HEAD_DIM))
    put("kvc", proj("kvc"))
    for kv_name, k_name, v_name in (("kvs", "ks", "vs"), ("kvw", "kw", "vw")):
        k2, v2 = _shared_kv_variants(rope(proj(kv_name), cosk_ref, sink_ref, HEAD_DIM))
        put(k_name, k2)
        put(v_name, v2)
    put("dg", proj("dg"))
    put("eq", proj("eq"))
    z = proj("zs")
    put("zs", z * jax.nn.sigmoid(z))


def _in_weights(w):
    sizes = (256,) * 10 + (256, 64, 64, 64, 64, 64, 64, 12, 256, 256, 256)
    offs = np.cumsum((0,) + sizes)
    (a_q, a_k, a_v, a_z, b_q, b_k, b_v, b_z, c_u, c_z, d_q, d_kc, d_vc, d_ks, d_vs, d_kw, d_vw,
     d_g, d_z, e_q, e_z) = [w[:, offs[i]:offs[i + 1]] for i in range(len(sizes))]
    scale = HEAD_DIM ** -0.5
    cat = lambda *t: jnp.concatenate(t, axis=1)
    cols = [a_q, a_k, a_v, b_q * scale, b_k, b_v, c_u, d_q * scale, cat(d_kc, d_vc), cat(d_ks, d_vs),
            cat(d_kw, d_vw), jnp.pad(d_g, ((0, 0), (0, 128 - d_g.shape[1]))), e_q * scale,
            a_z, b_z, c_z, d_z, e_z]
    return cat(*cols).astype(BF16)


def _rope_tables(s_len):
    def table(group, width):
        half = group // 2
        inv_freq = ROPE_THETA ** (-jnp.arange(half, dtype=F32) / half)
        ang = jnp.arange(s_len, dtype=F32)[:, None] * inv_freq[None, :]
        cos = jnp.tile(jnp.cos(ang), (1, width // half))
        sin = jnp.tile(jnp.concatenate([-jnp.sin(ang), jnp.sin(ang)], axis=1), (1, width // group))
        return cos, sin
    cos_a, sin_a = table(DIFF_QK_DIM, BRANCH_WIDTH)
    cos_b, sin_b = table(HEAD_DIM, BRANCH_WIDTH)
    cos_k = jnp.concatenate([cos_b[:, :HEAD_DIM], jnp.ones((s_len, HEAD_DIM), F32)], axis=1)
    sin_k = jnp.concatenate([sin_b[:, :HEAD_DIM], jnp.zeros((s_len, HEAD_DIM), F32)], axis=1)
    return cos_a, sin_a, cos_b, sin_b, cos_k, sin_k


def _in_proj(x2, g, wcat, tables, s_len):
    t, d = x2.shape
    tm = ROW_TILE
    nsb = s_len // tm
    row = lambda w: pl.BlockSpec((tm, w), lambda i: (i, 0))
    tab = lambda a: pl.BlockSpec((tm, a.shape[1]), lambda i: (i % nsb, 0))
    return pl.pallas_call(
        _in_proj_kernel,
        grid=(t // tm,),
        in_specs=[row(d), _const_spec((1, d)), _const_spec(wcat.shape)] + [tab(a) for a in tables],
        out_specs=[row(w) for _, w, _ in _IN_OUTS],
        out_shape=[jax.ShapeDtypeStruct((t, w), dt) for _, w, dt in _IN_OUTS],
        compiler_params=pltpu.CompilerParams(dimension_semantics=("arbitrary",),
                                             vmem_limit_bytes=BIG_VMEM_LIMIT),
        name="in_proj",
    )(x2, g.reshape(1, d), wcat, *tables)


def _lane_fold(x, op):
    parts = [x[:, j * LANES:(j + 1) * LANES] for j in range(x.shape[1] // LANES)]
    return functools.reduce(op, parts)


def _strip_attention(chains, tq, kb_lo, kb_hi, bias_fn, last_bias, scratch, exp_scale=1.0):
    s_ref, m_ref, acc_ref = scratch
    n = len(chains)
    reps = tq // LANES
    c1 = exp_scale * math.log2(math.e)

    def tile(cache, ref, col, width, kb, ntiles=1):
        key = (id(ref), col, width)
        if key not in cache:
            cache[key] = ref[pl.ds(pl.multiple_of(kb * tq, tq), ntiles * tq), col:col + width]
        return cache[key]

    def scores(kb, extra):
        bias = bias_fn(kb) if bias_fn is not None else None
        if extra is not None:
            bias = extra if bias is None else bias + extra
        cache, out = {}, []
        for q, (k_ref, k_col), _ in chains:
            s = _dot_nt(q, tile(cache, k_ref, k_col, q.shape[1], kb)) * c1
            out.append(s if bias is None else s + bias)
        return out

    def tile_pairs(lo, hi, step):
        def two(j, carry):
            step(lo + 2 * j, 2)
            return carry
        cnt = hi - lo
        lax.fori_loop(0, lax.shift_right_arithmetic(cnt, 1), two, 0)

        @pl.when((cnt & 1) == 1)
        def _():
            step(hi - 1, 1)

    m_ref[0:n] = jnp.full((n,) + m_ref.shape[1:], NEG_INF, F32)

    def pass1(kb, ntiles):
        tiles = [scores(kb + j, None) for j in range(ntiles)]
        for i in range(n):
            m = m_ref[i]
            for j in range(ntiles):
                s_ref[i, kb + j] = tiles[j][i]
                m = jnp.maximum(m, _lane_fold(tiles[j][i], jnp.maximum))
            m_ref[i] = m

    tile_pairs(kb_lo, kb_hi - 1, pass1)
    for i, s in enumerate(scores(kb_hi - 1, last_bias)):
        s_ref[i, kb_hi - 1] = s
        m = jnp.max(jnp.maximum(m_ref[i], _lane_fold(s, jnp.maximum)), axis=-1, keepdims=True)
        m_ref[i] = jnp.broadcast_to(m, m_ref.shape[1:])
        acc_ref[i] = jnp.zeros(acc_ref.shape[1:], F32)


    def pass2(kb, ntiles):
        cache = {}
        for i, (_, _, (v_ref, v_col)) in enumerate(chains):
            m = jnp.concatenate([m_ref[i]] * reps, axis=1)
            p = [jnp.exp2(s_ref[i, kb + j] - m).astype(BF16) for j in range(ntiles)]
            p = p[0] if ntiles == 1 else jnp.concatenate(p, axis=1)
            acc_ref[i] += _dot(p, tile(cache, v_ref, v_col, LANES, kb, ntiles))

    tile_pairs(kb_lo, kb_hi, pass2)
    return [acc_ref[i] for i in range(n)]


def _strip_scratch(n, tq, s_len):
    return [pltpu.VMEM((n, s_len // tq, tq, tq), F32), pltpu.VMEM((n, tq, LANES), F32),
            pltpu.VMEM((n, tq, LANES), F32)]


def _causal_bias(tq):
    r = lax.broadcasted_iota(jnp.int32, (tq, tq), 0)
    c = lax.broadcasted_iota(jnp.int32, (tq, tq), 1)
    return jnp.where(c <= r, 0.0, NEG_INF).astype(F32)


def _att_specs(tq, s_len, k_width, v_width):
    q_spec = pl.BlockSpec((None, tq, BRANCH_WIDTH), lambda b, i: (b, i, 0))
    kv = lambda w: pl.BlockSpec((None, s_len, w), lambda b, i: (b, 0, 0))
    return q_spec, kv(k_width), kv(v_width)


def _diff_kernel(lam_ref, q_ref, k_ref, v_ref, g_ref, hm_ref, o_ref, *scratch, tq, out_scale):
    qi = pl.program_id(1)
    q = q_ref[...]
    lane = lax.broadcasted_iota(jnp.int32, (1, BRANCH_WIDTH), 1)
    lam = lam_ref[0]
    chains = []
    for hc in range(2 * N_HEADS):
        lo = hc * DIFF_QK_DIM
        cmask = jnp.where((lane >= lo) & (lane < lo + DIFF_QK_DIM), 1.0, 0.0).astype(BF16)
        chains.append((q * cmask, (k_ref, 0), (v_ref, (hc // 2) * LANES)))
    acc = _strip_attention(chains, tq, 0, qi + 1, None, _causal_bias(tq), scratch,
                           exp_scale=DIFF_QK_DIM ** -0.5)
    halves = []
    for pair in range(N_HEADS // 2):
        even, odd = 4 * pair, 4 * pair + 2
        halves.append(_normalise_pair(acc[even], acc[odd]) - lam * _normalise_pair(acc[even + 1], acc[odd + 1]))
    o = jnp.concatenate(halves, axis=1)
    ms = _dot_hilo(o * o, hm_ref[...])
    o_ref[...] = o * lax.rsqrt(ms + RMS_EPS) * g_ref[...] * out_scale


def _diff_attention(q, k, v_aug, lam, subln_g, lam_init):
    b, s_len, w = q.shape
    tq = ATT_TILE
    q_spec, k_spec, v_spec = _att_specs(tq, s_len, w, v_aug.shape[-1])
    head = np.arange(w) // HEAD_DIM
    hm = jnp.asarray((head[:, None] == head[None, :]) / HEAD_DIM, dtype=BF16)
    g = jnp.tile(subln_g.astype(F32), N_HEADS).reshape(1, w)
    return pl.pallas_call(
        functools.partial(_diff_kernel, tq=tq, out_scale=1.0 - lam_init),
        grid=(b, s_len // tq),
        in_specs=[pl.BlockSpec(memory_space=pltpu.SMEM), q_spec, k_spec, v_spec,
                  _const_spec((1, w)), _const_spec((w, w))],
        out_specs=q_spec,
        out_shape=jax.ShapeDtypeStruct((b, s_len, w), F32),
        scratch_shapes=_strip_scratch(2 * N_HEADS, tq, s_len),
        compiler_params=pltpu.CompilerParams(dimension_semantics=("arbitrary", "arbitrary"),
                                             vmem_limit_bytes=BIG_VMEM_LIMIT),
        name="diff_attention",
    )(lam.reshape(1), q, k, v_aug, g, hm)


def _dil_kernel(q_ref, k_ref, v_ref, bias_ref, o_ref, *scratch, tq):
    qi = pl.program_id(1)
    q = q_ref[...]
    chains = [(q * _head_mask(h, BF16), (k_ref, 0), (v_ref, h * LANES)) for h in range(N_HEADS)]
    acc = _strip_attention(chains, tq, 0, qi + 1, lambda kb: bias_ref[qi - kb], None, scratch)
    o_ref[...] = jnp.concatenate([_normalise_pair(acc[0], acc[1]), _normalise_pair(acc[2], acc[3])], axis=1)


def _dilated_bias(tq, s_len):
    nq = s_len // tq
    d = (np.arange(nq)[:, None, None] * tq + np.arange(tq)[None, :, None] - np.arange(tq)[None, None, :])
    count = np.zeros(d.shape, np.float64)
    for window, dil in DIL_PATTERNS:
        count += (d >= 0) & (d <= window) & (d % dil == 0)
    return np.where(count > 0, np.log2(np.maximum(count, 1.0)), NEG_INF).astype(np.float32)


def _dilated_attention(q, k, v_aug):
    b, s_len, w = q.shape
    tq = ATT_TILE
    q_spec, k_spec, v_spec = _att_specs(tq, s_len, w, v_aug.shape[-1])
    bias = jnp.asarray(_dilated_bias(tq, s_len))
    return pl.pallas_call(
        functools.partial(_dil_kernel, tq=tq),
        grid=(b, s_len // tq),
        in_specs=[q_spec, k_spec, v_spec, _const_spec(bias.shape)],
        out_specs=q_spec,
        out_shape=jax.ShapeDtypeStruct((b, s_len, w), F32),
        scratch_shapes=_strip_scratch(N_HEADS, tq, s_len),
        compiler_params=pltpu.CompilerParams(dimension_semantics=("arbitrary", "arbitrary"),
                                             vmem_limit_bytes=BIG_VMEM_LIMIT),
        name="dilated_attention",
    )(q, k, v_aug, bias)


def _s5_kernel(u_ref, bm_ref, cm_ref, are_ref, aim_ref, d_ref, wg_ref, bg_ref, o_ref, st_ref, *xs_refs,
               ts, nb):
    n = S5_GROUPS * S5_STATE
    nc = n // LANES

    @pl.when(pl.program_id(0) == 0)
    def _():
        st_ref[...] = jnp.zeros(st_ref.shape, F32)

    u = u_ref[...].reshape(nb * ts, u_ref.shape[-1])
    bu = _dot(u.astype(BF16), bm_ref[...])
    for c, x_ref in enumerate(xs_refs):
        for b in range(nb):
            x_ref[pl.ds(b, ts, stride=nb), :] = bu[b * ts:(b + 1) * ts, c * LANES:(c + 1) * LANES]
    a_re = jnp.broadcast_to(are_ref[...], (nb, n))
    a_im = jnp.broadcast_to(aim_ref[...], (nb, n))

    def step(t, carry):
        x_re, x_im = carry
        rows = pl.ds(pl.multiple_of(t * nb, nb), nb)
        bu_re = jnp.concatenate([x_ref[rows, :] for x_ref in xs_refs[:nc]], axis=1)
        bu_im = jnp.concatenate([x_ref[rows, :] for x_ref in xs_refs[nc:]], axis=1)
        n_re = a_re * x_re - a_im * x_im + bu_re
        n_im = a_re * x_im + a_im * x_re + bu_im
        for c in range(nc):
            xs_refs[c][rows, :] = n_re[:, c * LANES:(c + 1) * LANES]
            xs_refs[nc + c][rows, :] = n_im[:, c * LANES:(c + 1) * LANES]
        return n_re, n_im

    x_re, x_im = lax.fori_loop(0, ts, step, (st_ref[:, 0:n], st_ref[:, n:2 * n]))
    st_ref[:, 0:n] = x_re
    st_ref[:, n:2 * n] = x_im

    xs = jnp.concatenate(
        [jnp.concatenate([x_ref[pl.ds(b, ts, stride=nb), :].astype(BF16) for x_ref in xs_refs], axis=1)
         for b in range(nb)], axis=0)
    y = _dot(xs, cm_ref[...]) + d_ref[...] * u
    t = _dot(jax.nn.gelu(y).astype(BF16), wg_ref[...]) + bg_ref[...]
    o_ref[...] = (t[:, :BRANCH_WIDTH] * jax.nn.sigmoid(t[:, BRANCH_WIDTH:])).reshape(o_ref.shape)


def _s5_params(lam_re, lam_im, log_dt, b_re, b_im, c_re, c_im):
    g, n, p = S5_GROUPS, S5_STATE, S5_GROUP
    lr, li = lam_re.astype(F32), lam_im.astype(F32)
    dt = jnp.exp(log_dt.astype(F32))[:, None]
    mag = jnp.exp(lr * dt)
    a_re, a_im = mag * jnp.cos(li * dt), mag * jnp.sin(li * dt)
    den = lr * lr + li * li
    n_re, n_im = a_re - 1.0, a_im
    z_re = (n_re * lr + n_im * li) / den
    z_im = (n_im * lr - n_re * li) / den
    br, bi = b_re.astype(F32), b_im.astype(F32)
    bb_re = z_re[..., None] * br - z_im[..., None] * bi
    bb_im = z_re[..., None] * bi + z_im[..., None] * br
    eye = jnp.eye(g, dtype=F32)
    blockdiag_in = lambda t: jnp.einsum("gnp,gh->gphn", t, eye).reshape(g * p, g * n)
    blockdiag_out = lambda t: jnp.einsum("gpn,gh->gnhp", t, eye).reshape(g * n, g * p)
    bm = jnp.concatenate([blockdiag_in(bb_re), blockdiag_in(bb_im)], axis=1)
    cm = jnp.concatenate([blockdiag_out(c_re.astype(F32)), -blockdiag_out(c_im.astype(F32))], axis=0)
    return bm.astype(BF16), cm.astype(BF16), a_re.reshape(1, g * n), a_im.reshape(1, g * n)


def _s5_branch(u, params, d_skip, w_glu, b_glu):
    nb, s_len, w = u.shape
    assert nb == 8
    ts = S5_CHUNK
    bm, cm, a_re, a_im = params
    n2 = bm.shape[1]
    blk = pl.BlockSpec((nb, ts, w), lambda i: (0, i, 0))
    return pl.pallas_call(
        functools.partial(_s5_kernel, ts=ts, nb=nb),
        grid=(s_len // ts,),
        in_specs=[blk, _const_spec(bm.shape), _const_spec(cm.shape), _const_spec(a_re.shape),
                  _const_spec(a_im.shape), _const_spec((1, w)), _const_spec(w_glu.shape),
                  _const_spec((1, 2 * w))],
        out_specs=blk,
        out_shape=jax.ShapeDtypeStruct((nb, s_len, w), F32),
        scratch_shapes=[pltpu.VMEM((nb, n2), F32)] + [pltpu.VMEM((ts * nb, LANES), F32)] * (n2 // LANES),
        compiler_params=pltpu.CompilerParams(dimension_semantics=("arbitrary",),
                                             vmem_limit_bytes=BIG_VMEM_LIMIT),
        name="s5_scan",
    )(u, bm, cm, a_re, a_im, d_skip.astype(F32).reshape(1, w), w_glu.astype(BF16),
      b_glu.astype(F32).reshape(1, 2 * w))


def _compress_kernel(r_ref, pe_top_ref, pe_bot_ref, w_top_ref, w_bot_ref, w2_ref, k_ref, v_ref):
    r = r_ref[...]
    top = (r + pe_top_ref[...]).astype(BF16)
    nxt = pltpu.roll(r, r.shape[0] - 1, 0)
    bot = (nxt + pe_bot_ref[...]).astype(BF16)
    hid = jax.nn.gelu(_dot(top, w_top_ref[...]) + _dot(bot, w_bot_ref[...]))
    k2, v2 = _shared_kv_variants(_dot(hid.astype(BF16), w2_ref[...]))
    k_ref[...] = k2.astype(BF16)
    v_ref[...] = v2.astype(BF16)


def _compress(kvc, pe, w1, w2):
    b, s_len, _ = kvc.shape
    nr = s_len // CMP_STRIDE
    per = CMP_BLOCK // CMP_STRIDE
    assert per == 2
    hid = w1.shape[-1]
    r = kvc.reshape(b, nr, CMP_STRIDE * 2 * HEAD_DIM)
    w1r = w1.astype(F32).reshape(2, per, CMP_STRIDE, HEAD_DIM, hid)
    per_r = pe.astype(F32).reshape(2, per, CMP_STRIDE, HEAD_DIM)

    def expand(j):
        wk = jnp.pad(w1r[0, j], ((0, 0), (0, HEAD_DIM), (0, hid)))
        wv = jnp.pad(w1r[1, j], ((0, 0), (HEAD_DIM, 0), (hid, 0)))
        return (wk + wv).reshape(CMP_STRIDE * 2 * HEAD_DIM, 2 * hid).astype(BF16)

    pe_rows = [jnp.concatenate([per_r[0, j], per_r[1, j]], axis=-1).reshape(1, -1) for j in range(per)]
    z = jnp.zeros((hid, HEAD_DIM), F32)
    w2f = w2.astype(F32)
    w2x = jnp.concatenate([jnp.concatenate([w2f[0], z], axis=1),
                           jnp.concatenate([z, w2f[1]], axis=1)], axis=0).astype(BF16)
    w_top, w_bot = expand(0), expand(1)
    blk = pl.BlockSpec((None, nr, r.shape[-1]), lambda i: (i, 0, 0))
    out = pl.BlockSpec((None, nr, BRANCH_WIDTH), lambda i: (i, 0, 0))
    return pl.pallas_call(
        _compress_kernel,
        grid=(b,),
        in_specs=[blk, _const_spec(pe_rows[0].shape), _const_spec(pe_rows[1].shape),
                  _const_spec(w_top.shape), _const_spec(w_bot.shape), _const_spec(w2x.shape)],
        out_specs=[out, out],
        out_shape=[jax.ShapeDtypeStruct((b, nr, BRANCH_WIDTH), BF16)] * 2,
        compiler_params=pltpu.CompilerParams(dimension_semantics=("arbitrary",)),
        name="nsa_compress",
    )(r, pe_rows[0], pe_rows[1], w_top, w_bot, w2x)


def _nsa_kernel(dq_ref, dqr_ref, dg_ref, kc_ref, vc_ref, ks_ref, vs_ref, kw_ref, vw_ref,
                ovt_ref, exp_ref, gsel_ref, wbias_ref, o_ref, *scratch, tq, n_cmp, n_slc, n_sel):
    qi = pl.program_id(1)
    t0 = qi * tq
    nr = kc_ref.shape[0]
    low = _low_half()
    q_cols = lambda h: slice((h // 2) * LANES, (h // 2 + 1) * LANES)
    kv_col = lambda h: (h % 2) * LANES

    dq = dq_ref[...]
    tpos = t0 + lax.broadcasted_iota(jnp.int32, (tq, nr), 0)
    ci = lax.broadcasted_iota(jnp.int32, (tq, nr), 1)
    cmask = (ci * CMP_STRIDE + (CMP_BLOCK - 1) <= tpos) & (ci < n_cmp)
    p_sum = jnp.zeros((tq, nr), F32)
    o_heads = []
    for h in range(N_HEADS):
        s = jnp.where(cmask, _dot_nt(dq[:, q_cols(h)], kc_ref[:, kv_col(h):kv_col(h) + LANES]), NEG_INF)
        m = jnp.max(s, axis=-1, keepdims=True)
        e = jnp.where(cmask, jnp.exp(s - m), 0.0)
        p = e / jnp.maximum(jnp.sum(e, axis=-1, keepdims=True), 1e-30)
        p_sum = p_sum + p
        o_heads.append(_dot(p.astype(BF16), vc_ref[:, kv_col(h):kv_col(h) + LANES]))
    o_cmp = jnp.concatenate([jnp.where(low, o_heads[0], o_heads[1]), jnp.where(low, o_heads[2], o_heads[3])],
                            axis=1)

    p_hi, p_lo = _split_hi_lo(p_sum)
    ovt = ovt_ref[...]
    imp_t = _dot_nt(ovt, p_hi) + _dot_nt(ovt, p_lo)
    blk = lax.broadcasted_iota(jnp.int32, (n_slc, tq), 0)
    qblk = lax.shift_right_arithmetic(t0 + lax.broadcasted_iota(jnp.int32, (n_slc, tq), 1),
                                      int(math.log2(SLC_BLOCK)))
    forced = jnp.where(blk == 0, 1, jnp.where(blk == qblk, 1, jnp.where(blk == qblk - 1, 1, 0)))
    score = jnp.where(blk <= qblk, jnp.where(forced > 0, FORCE_SCORE, imp_t), NEG_INF)
    rank = jnp.zeros((n_slc, tq), F32)
    for i in range(n_slc):
        si = score[i:i + 1, :]
        tie = jnp.where(blk > i, 1.0, 0.0)
        rank = rank + jnp.where(si > score, 1.0, jnp.where(si == score, tie, 0.0))
    sel_bias_t = jnp.where(rank < n_sel, jnp.where(score > 0.5 * NEG_INF, 0.0, NEG_INF), NEG_INF)
    pad_rows = exp_ref.shape[1] - n_slc
    sel_bias = jnp.concatenate([sel_bias_t, jnp.full((pad_rows, tq), NEG_INF, F32)], axis=0).T.astype(BF16)

    dqr = dqr_ref[...]
    pairs = lambda acc: jnp.concatenate([_normalise_pair(acc[0], acc[1]), _normalise_pair(acc[2], acc[3])], axis=1)
    chains = [(dqr[:, q_cols(h)], (ks_ref, kv_col(h)), (vs_ref, kv_col(h))) for h in range(N_HEADS)]
    o_slc = pairs(_strip_attention(chains, tq, 0, qi + 1, lambda kb: _dot(sel_bias, exp_ref[kb]),
                                   _causal_bias(tq), scratch))
    chains = [(dqr[:, q_cols(h)], (kw_ref, kv_col(h)), (vw_ref, kv_col(h))) for h in range(N_HEADS)]
    win_lo = jnp.maximum(qi - (wbias_ref.shape[0] - 1), 0)
    o_win = pairs(_strip_attention(chains, tq, win_lo, qi + 1, lambda kb: wbias_ref[qi - kb], None, scratch))

    gates = _dot_hilo(jax.nn.sigmoid(dg_ref[...]), gsel_ref[...])
    w = BRANCH_WIDTH
    o_ref[...] = gates[:, :w] * o_cmp + gates[:, w:2 * w] * o_slc + gates[:, 2 * w:] * o_win


def _nsa_constants(tq, s_len):
    n_cmp = (s_len - CMP_BLOCK) // CMP_STRIDE + 1
    n_slc = s_len // SLC_BLOCK
    nr = s_len // CMP_STRIDE
    c0 = np.arange(n_cmp)[:, None] * CMP_STRIDE
    s0 = np.arange(n_slc)[None, :] * SLC_BLOCK
    overlap = np.clip(np.minimum(c0 + CMP_BLOCK, s0 + SLC_BLOCK) - np.maximum(c0, s0), 0, None) / CMP_STRIDE
    ovt = np.zeros((n_slc, nr), np.float32)
    ovt[:, :n_cmp] = overlap.T
    rows = -(-n_slc // 128) * 128
    expand = np.zeros((s_len // tq, rows, tq), np.float32)
    tok = np.arange(s_len)
    expand[tok // tq, tok // SLC_BLOCK, tok % tq] = 1.0
    gsel = np.zeros((128, 3 * BRANCH_WIDTH), np.float32)
    for h in range(N_HEADS):
        for j in range(3):
            gsel[h * 3 + j, j * BRANCH_WIDTH + h * HEAD_DIM:j * BRANCH_WIDTH + (h + 1) * HEAD_DIM] = 1.0
    nwin = -(-WINDOW // tq) + 1
    d = np.arange(nwin)[:, None, None] * tq + np.arange(tq)[None, :, None] - np.arange(tq)[None, None, :]
    wbias = np.where((d >= 0) & (d < WINDOW), 0.0, NEG_INF).astype(np.float32)
    return n_cmp, n_slc, ovt, expand, gsel, wbias


def _nsa_attention(dq, dqr, dg, k_cmp, v_cmp, ks, vs, kw, vw):
    b, s_len, w = dq.shape
    tq = ATT_TILE
    n_cmp, n_slc, ovt, expand, gsel, wbias = _nsa_constants(tq, s_len)
    q_spec, kv_spec, _ = _att_specs(tq, s_len, w, w)
    nr = k_cmp.shape[1]
    cmp_spec = pl.BlockSpec((None, nr, w), lambda bi, i: (bi, 0, 0))
    g_spec = pl.BlockSpec((None, tq, dg.shape[-1]), lambda bi, i: (bi, i, 0))
    consts = [jnp.asarray(ovt, BF16), jnp.asarray(expand, BF16), jnp.asarray(gsel, BF16), jnp.asarray(wbias)]
    return pl.pallas_call(
        functools.partial(_nsa_kernel, tq=tq, n_cmp=n_cmp, n_slc=n_slc, n_sel=min(N_SELECT, n_slc)),
        grid=(b, s_len // tq),
        in_specs=[q_spec, q_spec, g_spec, cmp_spec, cmp_spec, kv_spec, kv_spec, kv_spec, kv_spec]
        + [_const_spec(c.shape) for c in consts],
        out_specs=q_spec,
        out_shape=jax.ShapeDtypeStruct((b, s_len, w), F32),
        scratch_shapes=_strip_scratch(N_HEADS, tq, s_len),
        compiler_params=pltpu.CompilerParams(dimension_semantics=("arbitrary", "arbitrary"),
                                             vmem_limit_bytes=BIG_VMEM_LIMIT),
        name="nsa_attention",
    )(dq, dqr, dg, k_cmp, v_cmp, ks, vs, kw, vw, *consts)


def _mem_kv_kernel(m_ref, g_ref, w_ref, k_ref, v_ref):
    kv = _dot(_rms(m_ref[...], g_ref[...]).astype(BF16), w_ref[...])
    k_ref[...] = kv[:, :BRANCH_WIDTH].astype(BF16)
    v_ref[...] = kv[:, BRANCH_WIDTH:].astype(BF16)


def _mem_kv(mem2, g, w_kv):
    rows, d = mem2.shape
    tm = ROW_TILE
    out = pl.BlockSpec((tm, BRANCH_WIDTH), lambda i: (i, 0))
    return pl.pallas_call(
        _mem_kv_kernel,
        grid=(rows // tm,),
        in_specs=[pl.BlockSpec((tm, d), lambda i: (i, 0)), _const_spec((1, d)), _const_spec(w_kv.shape)],
        out_specs=[out, out],
        out_shape=[jax.ShapeDtypeStruct((rows, BRANCH_WIDTH), BF16)] * 2,
        compiler_params=pltpu.CompilerParams(dimension_semantics=("arbitrary",)),
        name="mem_kv",
    )(mem2, g.astype(F32).reshape(1, d), w_kv.astype(BF16))


def _mem_attn_kernel(q_ref, k_ref, v_ref, o_ref):
    q = q_ref[...]
    k = k_ref[...]
    v = v_ref[...]
    o = jnp.zeros(o_ref.shape, F32)
    for h in range(N_HEADS):
        s = _dot_nt(q * _head_mask(h, BF16), k)
        e = jnp.exp(s - jnp.max(s, axis=-1, keepdims=True))
        oh = _dot(e.astype(BF16), v) / jnp.sum(e, axis=-1, keepdims=True)
        o = jnp.where(_head_mask(h, F32) > 0.5, oh, o)
    o_ref[...] = o


def _mem_attention(q, k, v):
    b, s_len, w = q.shape
    tq = MEM_ATT_TILE
    q_spec = pl.BlockSpec((None, tq, w), lambda bi, i: (bi, i, 0))
    kv_spec = pl.BlockSpec((None, k.shape[1], w), lambda bi, i: (bi, 0, 0))
    return pl.pallas_call(
        _mem_attn_kernel,
        grid=(b, s_len // tq),
        in_specs=[q_spec, kv_spec, kv_spec],
        out_specs=q_spec,
        out_shape=jax.ShapeDtypeStruct((b, s_len, w), F32),
        compiler_params=pltpu.CompilerParams(dimension_semantics=("arbitrary", "arbitrary")),
        name="mem_attention",
    )(q, k, v)


def _merge_kernel(x_ref, g_ref, oa_ref, ob_ref, oc_ref, od_ref, oe_ref, zs_ref, wm_ref, bm_ref, wb_ref,
                  wo_ref, fg_ref, o_ref, *, final_norm):
    x = x_ref[...]
    d = x.shape[-1]
    w = BRANCH_WIDTH
    h = _rms(x, g_ref[...]).astype(BF16)
    mixed = jnp.zeros(x.shape, F32)
    for n, br_ref in enumerate((oa_ref, ob_ref, oc_ref, od_ref, oe_ref)):
        br = (br_ref[...] * zs_ref[:, n * w:(n + 1) * w]).astype(BF16)
        y = _dot(br, wb_ref[n])
        gate = jax.nn.sigmoid(_dot(h, wm_ref[:, n * d:(n + 1) * d]) + bm_ref[:, n * d:(n + 1) * d])
        mixed = mixed + gate * y
    out = x + _dot(mixed.astype(BF16), wo_ref[...])
    if final_norm:
        out = _rms(out, fg_ref[...])
    o_ref[...] = out


def _merge(x2, g, branches, zs, w_merge, b_merge, w_branch, w_out, final_g, final_norm):
    t, d = x2.shape
    tm = ROW_TILE
    row = lambda wd: pl.BlockSpec((tm, wd), lambda i: (i, 0))
    return pl.pallas_call(
        functools.partial(_merge_kernel, final_norm=final_norm),
        grid=(t // tm,),
        in_specs=[row(d), _const_spec((1, d))] + [row(BRANCH_WIDTH)] * N_BRANCHES + [row(zs.shape[1])]
        + [_const_spec(w_merge.shape), _const_spec((1, N_BRANCHES * d)), _const_spec(w_branch.shape),
           _const_spec(w_out.shape), _const_spec((1, d))],
        out_specs=row(d),
        out_shape=jax.ShapeDtypeStruct((t, d), F32),
        compiler_params=pltpu.CompilerParams(dimension_semantics=("arbitrary",),
                                             vmem_limit_bytes=BIG_VMEM_LIMIT),
        name="merge",
    )(x2, g.astype(F32).reshape(1, d), *branches, zs, w_merge.astype(BF16),
      b_merge.astype(F32).reshape(1, -1), w_branch.astype(BF16), w_out.astype(BF16),
      final_g.astype(F32).reshape(1, d))


def kernel(x, mem, norm_g, w_in, diff_lambda, diff_subln_g, s5_lambda_re, s5_lambda_im, s5_log_dt,
           s5_b_re, s5_b_im, s5_c_re, s5_c_im, s5_d, w_glu, b_glu, nsa_pe, nsa_w1, nsa_w2, mem_norm_g,
           w_mem_kv, w_merge, b_merge, w_branch, w_out, final_g):
    bsz, s_len, d = x.shape
    depth = w_in.shape[0]
    t = bsz * s_len
    w = BRANCH_WIDTH
    tables = _rope_tables(s_len)
    x2 = x.astype(F32).reshape(t, d)
    mem2 = mem.astype(F32).reshape(-1, d)
    for l in range(depth):
        proj = dict(zip([n for n, _, _ in _IN_OUTS],
                        _in_proj(x2, norm_g[l].astype(F32), _in_weights(w_in[l].astype(F32)), tables, s_len)))
        seq = lambda name: proj[name].reshape(bsz, s_len, -1)

        dl = diff_lambda[l].astype(F32)
        lam_init = 0.8 - 0.6 * math.exp(-0.3 * l)
        lam = jnp.exp(jnp.sum(dl[0] * dl[1])) - jnp.exp(jnp.sum(dl[2] * dl[3])) + lam_init
        o_a = _diff_attention(seq("qa"), seq("ka"), seq("va"), lam, diff_subln_g[l], lam_init)

        o_b = _dilated_attention(seq("qb"), seq("kb"), seq("vb"))

        s5p = _s5_params(s5_lambda_re[l], s5_lambda_im[l], s5_log_dt[l], s5_b_re[l], s5_b_im[l],
                         s5_c_re[l], s5_c_im[l])
        o_c = _s5_branch(seq("cu"), s5p, s5_d[l], w_glu[l], b_glu[l])

        k_cmp, v_cmp = _compress(seq("kvc"), nsa_pe[l], nsa_w1[l], nsa_w2[l])
        o_d = _nsa_attention(seq("dq"), seq("dqr"), seq("dg"), k_cmp, v_cmp,
                             seq("ks"), seq("vs"), seq("kw"), seq("vw"))

        k_mem, v_mem = _mem_kv(mem2, mem_norm_g[l], w_mem_kv[l])
        o_e = _mem_attention(seq("eq"), k_mem.reshape(bsz, -1, w), v_mem.reshape(bsz, -1, w))

        branches = [o.reshape(t, w) for o in (o_a, o_b, o_c, o_d, o_e)]
        x2 = _merge(x2, norm_g[l], branches, proj["zs"], w_merge[l], b_merge[l], w_branch[l], w_out[l],
                    final_g, final_norm=(l == depth - 1))
    return x2.reshape(bsz, s_len, d).astype(x.dtype)
```

```python
import functools
import math

import numpy as np
import jax
import jax.numpy as jnp
from jax import lax
from jax.experimental import pallas as pl
from jax.experimental.pallas import tpu as pltpu

F32 = jnp.float32
BF16 = jnp.bfloat16

HEAD_DIM = 64
BRANCH_WIDTH = 256
N_HEADS = 4
N_BRANCHES = 5
DIFF_QK_DIM = 32
DIL_PATTERNS = ((128, 1), (512, 4), (2048, 16))
S5_GROUP = 16
S5_GROUPS = 16
S5_STATE = 64
CMP_BLOCK = 32
CMP_STRIDE = 16
SLC_BLOCK = 64
N_SELECT = 16
WINDOW = 512
ROPE_THETA = 10000.0
RMS_EPS = 1e-6
NEG_INF = -1e30
FORCE_SCORE = 1e9

V7X_VMEM_BYTES = 64 * 1024 * 1024
BIG_VMEM_LIMIT = V7X_VMEM_BYTES - 8 * 1024 * 1024

LANES = 128
ATT_TILE = 256
ROW_TILE = 512
MEM_ATT_TILE = 1024
S5_CHUNK = 128

_NT = (((1,), (1,)), ((), ()))


def _rms(x, g):
    return x * lax.rsqrt(jnp.mean(x * x, axis=-1, keepdims=True) + RMS_EPS) * g


def _dot(a, b):
    return jnp.dot(a, b, preferred_element_type=F32)


def _dot_nt(a, b):
    return lax.dot_general(a, b, _NT, preferred_element_type=F32)


def _split_hi_lo(x):
    hi = x.astype(BF16)
    lo = (x - hi.astype(F32)).astype(BF16)
    return hi, lo


def _dot_hilo(x, w):
    hi, lo = _split_hi_lo(x)
    return _dot(hi, w) + _dot(lo, w)


def _const_spec(shape):
    n = len(shape)
    return pl.BlockSpec(shape, lambda *_: (0,) * n, pipeline_mode=pl.Buffered(1))


def _head_mask(h, dtype):
    lane = lax.broadcasted_iota(jnp.int32, (1, BRANCH_WIDTH), 1)
    return jnp.where((lane >= h * HEAD_DIM) & (lane < (h + 1) * HEAD_DIM), 1.0, 0.0).astype(dtype)


def _low_half():
    return lax.broadcasted_iota(jnp.int32, (1, LANES), 1) < HEAD_DIM


def _swap_halves(x):
    return pltpu.roll(x, HEAD_DIM, 1)


def _augment_heads(v):
    low = _low_half()
    parts = []
    for j in range(v.shape[1] // LANES):
        pair = v[:, j * LANES:(j + 1) * LANES]
        parts += [jnp.where(low, pair, 1.0), jnp.where(low, 1.0, pair)]
    return jnp.concatenate(parts, axis=1)


def _shared_kv_variants(kv):
    low = _low_half()
    sw = _swap_halves(kv)
    k2 = jnp.concatenate([jnp.where(low, kv, 0.0), jnp.where(low, 0.0, sw)], axis=1)
    v2 = jnp.concatenate([jnp.where(low, sw, 1.0), jnp.where(low, 1.0, kv)], axis=1)
    return k2, v2


def _normalise_pair(acc_even, acc_odd):
    return jnp.where(_low_half(), acc_even / _swap_halves(acc_even), acc_odd / _swap_halves(acc_odd))


_IN_COLS = (("qa", 256), ("ka", 256), ("va", 256), ("qb", 256), ("kb", 256), ("vb", 256), ("cu", 256),
            ("dq", 256), ("kvc", 128), ("kvs", 128), ("kvw", 128), ("dg", 128), ("eq", 256), ("zs", 1280))
_IN_OFFS = dict(zip([n for n, _ in _IN_COLS], np.cumsum([0] + [w for _, w in _IN_COLS])[:-1].tolist()))
_IN_WIDTH = dict(_IN_COLS)
_IN_OUTS = (("qa", 256, BF16), ("ka", 256, BF16), ("va", 512, BF16),
            ("qb", 256, BF16), ("kb", 256, BF16), ("vb", 512, BF16),
            ("cu", 256, F32), ("dq", 256, BF16), ("dqr", 256, BF16), ("kvc", 128, F32),
            ("ks", 256, BF16), ("vs", 256, BF16), ("kw", 256, BF16), ("vw", 256, BF16),
            ("dg", 128, F32), ("eq", 256, BF16), ("zs", 1280, F32))


def _rotate_half(y, group):
    half = group // 2
    lane = lax.broadcasted_iota(jnp.int32, (1, LANES), 1)
    first = (lane & (group - 1)) < half
    parts = []
    for j in range(y.shape[1] // LANES):
        v = y[:, j * LANES:(j + 1) * LANES]
        parts.append(jnp.where(first, pltpu.roll(v, LANES - half, 1), pltpu.roll(v, half, 1)))
    return jnp.concatenate(parts, axis=1)


def _in_proj_kernel(x_ref, g_ref, w_ref, cosa_ref, sina_ref, cosb_ref, sinb_ref, cosk_ref, sink_ref, *out_refs):
    out = dict(zip([n for n, _, _ in _IN_OUTS], out_refs))
    h = _rms(x_ref[...], g_ref[...]).astype(BF16)

    def proj(name):
        off = _IN_OFFS[name]
        return _dot(h, w_ref[:, off:off + _IN_WIDTH[name]])

    def rope(y, cos_ref, sin_ref, group):
        return y * cos_ref[...] + _rotate_half(y, group) * sin_ref[...]

    def put(name, y):
        out[name][...] = y.astype(out[name].dtype)

    put("qa", rope(proj("qa"), cosa_ref, sina_ref, DIFF_QK_DIM))
    put("ka", rope(proj("ka"), cosa_ref, sina_ref, DIFF_QK_DIM))
    put("va", _augment_heads(proj("va")))
    put("qb", rope(proj("qb"), cosb_ref, sinb_ref, HEAD_DIM))
    put("kb", rope(proj("kb"), cosb_ref, sinb_ref, HEAD_DIM))
    put("vb", _augment_heads(proj("vb")))
    put("cu", proj("cu"))
    dq = proj("dq")
    put("dq", dq)
    put("dqr", rope(dq, cosb_ref, sinb_ref, HEAD_DIM))
    put("kvc", proj("kvc"))
    for kv_name, k_name, v_name in (("kvs", "ks", "vs"), ("kvw", "kw", "vw")):
        k2, v2 = _shared_kv_variants(rope(proj(kv_name), cosk_ref, sink_ref, HEAD_DIM))
        put(k_name, k2)
        put(v_name, v2)
    put("dg", proj("dg"))
    put("eq", proj("eq"))
    z = proj("zs")
    put("zs", z * jax.nn.sigmoid(z))


def _in_weights(w):
    sizes = (256,) * 10 + (256, 64, 64, 64, 64, 64, 64, 12, 256, 256, 256)
    offs = np.cumsum((0,) + sizes)
    (a_q, a_k, a_v, a_z, b_q, b_k, b_v, b_z, c_u, c_z, d_q, d_kc, d_vc, d_ks, d_vs, d_kw, d_vw,
     d_g, d_z, e_q, e_z) = [w[:, offs[i]:offs[i + 1]] for i in range(len(sizes))]
    scale = HEAD_DIM ** -0.5
    cat = lambda *t: jnp.concatenate(t, axis=1)
    cols = [a_q, a_k, a_v, b_q * scale, b_k, b_v, c_u, d_q * scale, cat(d_kc, d_vc), cat(d_ks, d_vs),
            cat(d_kw, d_vw), jnp.pad(d_g, ((0, 0), (0, 128 - d_g.shape[1]))), e_q * scale,
            a_z, b_z, c_z, d_z, e_z]
    return cat(*cols).astype(BF16)


def _rope_tables(s_len):
    def table(group, width):
        half = group // 2
        inv_freq = ROPE_THETA ** (-jnp.arange(half, dtype=F32) / half)
        ang = jnp.arange(s_len, dtype=F32)[:, None] * inv_freq[None, :]
        cos = jnp.tile(jnp.cos(ang), (1, width // half))
        sin = jnp.tile(jnp.concatenate([-jnp.sin(ang), jnp.sin(ang)], axis=1), (1, width // group))
        return cos, sin
    cos_a, sin_a = table(DIFF_QK_DIM, BRANCH_WIDTH)
    cos_b, sin_b = table(HEAD_DIM, BRANCH_WIDTH)
    cos_k = jnp.concatenate([cos_b[:, :HEAD_DIM], jnp.ones((s_len, HEAD_DIM), F32)], axis=1)
    sin_k = jnp.concatenate([sin_b[:, :HEAD_DIM], jnp.zeros((s_len, HEAD_DIM), F32)], axis=1)
    return cos_a, sin_a, cos_b, sin_b, cos_k, sin_k


def _in_proj(x2, g, wcat, tables, s_len):
    t, d = x2.shape
    tm = ROW_TILE
    nsb = s_len // tm
    row = lambda w: pl.BlockSpec((tm, w), lambda i: (i, 0))
    tab = lambda a: pl.BlockSpec((tm, a.shape[1]), lambda i: (i % nsb, 0))
    return pl.pallas_call(
        _in_proj_kernel,
        grid=(t // tm,),
        in_specs=[row(d), _const_spec((1, d)), _const_spec(wcat.shape)] + [tab(a) for a in tables],
        out_specs=[row(w) for _, w, _ in _IN_OUTS],
        out_shape=[jax.ShapeDtypeStruct((t, w), dt) for _, w, dt in _IN_OUTS],
        compiler_params=pltpu.CompilerParams(dimension_semantics=("arbitrary",),
                                             vmem_limit_bytes=BIG_VMEM_LIMIT),
        name="in_proj",
    )(x2, g.reshape(1, d), wcat, *tables)


def _lane_fold(x, op):
    parts = [x[:, j * LANES:(j + 1) * LANES] for j in range(x.shape[1] // LANES)]
    return functools.reduce(op, parts)


def _strip_attention(chains, tq, kb_lo, kb_hi, bias_fn, last_bias, scratch, exp_scale=1.0):
    s_ref, m_ref, acc_ref, q_ref = scratch
    n = len(chains)
    reps = tq // LANES
    c1 = exp_scale * math.log2(math.e)

    def grouped(side):
        groups = {}
        for i, chain in enumerate(chains):
            ref, col = chain[side]
            groups.setdefault((id(ref), col), (ref, col, []))[2].append(i)
        return list(groups.values())

    k_groups, row0 = [], 0
    for k_ref, k_col, idx in grouped(1):
        kw = chains[idx[0]][0].shape[1]
        for j, i in enumerate(idx):
            q_ref[row0 + j * tq:row0 + (j + 1) * tq, 0:kw] = chains[i][0]
        k_groups.append((k_ref, k_col, kw, idx, row0))
        row0 += len(idx) * tq
    v_groups = grouped(2)

    def tile(ref, col, width, kb, ntiles=1):
        return ref[pl.ds(pl.multiple_of(kb * tq, tq), ntiles * tq), col:col + width]

    def scores(kb):
        bias = bias_fn(kb) if bias_fn is not None else None
        if last_bias is not None:
            extra = jnp.where(kb == kb_hi - 1, last_bias, 0.0)
            bias = extra if bias is None else bias + extra
        out = [None] * n
        for k_ref, k_col, kw, idx, r0 in k_groups:
            s_all = _dot_nt(q_ref[r0:r0 + len(idx) * tq, 0:kw], tile(k_ref, k_col, kw, kb)) * c1
            for j, i in enumerate(idx):
                s = s_all[j * tq:(j + 1) * tq]
                out[i] = s if bias is None else s + bias
        return out

    def tile_pairs(lo, hi, step):
        def two(j, carry):
            step(lo + 2 * j, 2)
            return carry
        cnt = hi - lo
        lax.fori_loop(0, lax.shift_right_arithmetic(cnt, 1), two, 0)

        @pl.when((cnt & 1) == 1)
        def _():
            step(hi - 1, 1)

    m_ref[0:n] = jnp.full((n,) + m_ref.shape[1:], NEG_INF, F32)

    def pass1(kb, ntiles):
        tiles = [scores(kb + j) for j in range(ntiles)]
        for i in range(n):
            m = m_ref[i]
            for j in range(ntiles):
                s_ref[i, kb + j] = tiles[j][i]
                m = jnp.maximum(m, _lane_fold(tiles[j][i], jnp.maximum))
            m_ref[i] = m

    tile_pairs(kb_lo, kb_hi, pass1)
    for i in range(n):
        m_ref[i] = jnp.broadcast_to(jnp.max(m_ref[i], axis=-1, keepdims=True), m_ref.shape[1:])
        acc_ref[i] = jnp.zeros(acc_ref.shape[1:], F32)

    def pass2(kb, ntiles):
        for v_ref, v_col, idx in v_groups:
            stacked = []
            for i in idx:
                m = jnp.concatenate([m_ref[i]] * reps, axis=1)
                p = [jnp.exp2(s_ref[i, kb + j] - m).astype(BF16) for j in range(ntiles)]
                stacked.append(p[0] if ntiles == 1 else jnp.concatenate(p, axis=1))
            p_all = stacked[0] if len(idx) == 1 else jnp.concatenate(stacked, axis=0)
            pv = _dot(p_all, tile(v_ref, v_col, LANES, kb, ntiles))
            for j, i in enumerate(idx):
                acc_ref[i] += pv[j * tq:(j + 1) * tq]

    tile_pairs(kb_lo, kb_hi, pass2)
    return [acc_ref[i] for i in range(n)]


def _strip_scratch(n, tq, s_len):
    return [pltpu.VMEM((n, s_len // tq, tq, tq), F32), pltpu.VMEM((n, tq, LANES), F32),
            pltpu.VMEM((n, tq, LANES), F32), pltpu.VMEM((n * tq, BRANCH_WIDTH), BF16)]


def _causal_bias(tq):
    r = lax.broadcasted_iota(jnp.int32, (tq, tq), 0)
    c = lax.broadcasted_iota(jnp.int32, (tq, tq), 1)
    return jnp.where(c <= r, 0.0, NEG_INF).astype(F32)


def _att_specs(tq, s_len, k_width, v_width):
    q_spec = pl.BlockSpec((None, tq, BRANCH_WIDTH), lambda b, i: (b, i, 0))
    kv = lambda w: pl.BlockSpec((None, s_len, w), lambda b, i: (b, 0, 0))
    return q_spec, kv(k_width), kv(v_width)


def _diff_kernel(lam_ref, q_ref, k_ref, v_ref, g_ref, hm_ref, o_ref, *scratch, tq, out_scale):
    qi = pl.program_id(1)
    q = q_ref[...]
    lane = lax.broadcasted_iota(jnp.int32, (1, BRANCH_WIDTH), 1)
    lam = lam_ref[0]
    chains = []
    for hc in range(2 * N_HEADS):
        lo = hc * DIFF_QK_DIM
        cmask = jnp.where((lane >= lo) & (lane < lo + DIFF_QK_DIM), 1.0, 0.0).astype(BF16)
        chains.append((q * cmask, (k_ref, 0), (v_ref, (hc // 2) * LANES)))
    acc = _strip_attention(chains, tq, 0, qi + 1, None, _causal_bias(tq), scratch,
                           exp_scale=DIFF_QK_DIM ** -0.5)
    halves = []
    for pair in range(N_HEADS // 2):
        even, odd = 4 * pair, 4 * pair + 2
        halves.append(_normalise_pair(acc[even], acc[odd]) - lam * _normalise_pair(acc[even + 1], acc[odd + 1]))
    o = jnp.concatenate(halves, axis=1)
    ms = _dot_hilo(o * o, hm_ref[...])
    o_ref[...] = o * lax.rsqrt(ms + RMS_EPS) * g_ref[...] * out_scale


def _diff_attention(q, k, v_aug, lam, subln_g, lam_init):
    b, s_len, w = q.shape
    tq = ATT_TILE
    q_spec, k_spec, v_spec = _att_specs(tq, s_len, w, v_aug.shape[-1])
    head = np.arange(w) // HEAD_DIM
    hm = jnp.asarray((head[:, None] == head[None, :]) / HEAD_DIM, dtype=BF16)
    g = jnp.tile(subln_g.astype(F32), N_HEADS).reshape(1, w)
    return pl.pallas_call(
        functools.partial(_diff_kernel, tq=tq, out_scale=1.0 - lam_init),
        grid=(b, s_len // tq),
        in_specs=[pl.BlockSpec(memory_space=pltpu.SMEM), q_spec, k_spec, v_spec,
                  _const_spec((1, w)), _const_spec((w, w))],
        out_specs=q_spec,
        out_shape=jax.ShapeDtypeStruct((b, s_len, w), F32),
        scratch_shapes=_strip_scratch(2 * N_HEADS, tq, s_len),
        compiler_params=pltpu.CompilerParams(dimension_semantics=("arbitrary", "arbitrary"),
                                             vmem_limit_bytes=BIG_VMEM_LIMIT),
        name="diff_attention",
    )(lam.reshape(1), q, k, v_aug, g, hm)


def _dil_kernel(q_ref, k_ref, v_ref, bias_ref, o_ref, *scratch, tq):
    qi = pl.program_id(1)
    q = q_ref[...]
    chains = [(q * _head_mask(h, BF16), (k_ref, 0), (v_ref, h * LANES)) for h in range(N_HEADS)]
    acc = _strip_attention(chains, tq, 0, qi + 1, lambda kb: bias_ref[qi - kb], None, scratch)
    o_ref[...] = jnp.concatenate([_normalise_pair(acc[0], acc[1]), _normalise_pair(acc[2], acc[3])], axis=1)


def _dilated_bias(tq, s_len):
    nq = s_len // tq
    d = (np.arange(nq)[:, None, None] * tq + np.arange(tq)[None, :, None] - np.arange(tq)[None, None, :])
    count = np.zeros(d.shape, np.float64)
    for window, dil in DIL_PATTERNS:
        count += (d >= 0) & (d <= window) & (d % dil == 0)
    return np.where(count > 0, np.log2(np.maximum(count, 1.0)), NEG_INF).astype(np.float32)


def _dilated_attention(q, k, v_aug):
    b, s_len, w = q.shape
    tq = ATT_TILE
    q_spec, k_spec, v_spec = _att_specs(tq, s_len, w, v_aug.shape[-1])
    bias = jnp.asarray(_dilated_bias(tq, s_len))
    return pl.pallas_call(
        functools.partial(_dil_kernel, tq=tq),
        grid=(b, s_len // tq),
        in_specs=[q_spec, k_spec, v_spec, _const_spec(bias.shape)],
        out_specs=q_spec,
        out_shape=jax.ShapeDtypeStruct((b, s_len, w), F32),
        scratch_shapes=_strip_scratch(N_HEADS, tq, s_len),
        compiler_params=pltpu.CompilerParams(dimension_semantics=("arbitrary", "arbitrary"),
                                             vmem_limit_bytes=BIG_VMEM_LIMIT),
        name="dilated_attention",
    )(q, k, v_aug, bias)


def _s5_kernel(u_ref, bm_ref, cm_ref, are_ref, aim_ref, d_ref, wg_ref, bg_ref, o_ref, st_ref, *xs_refs,
               ts, nb):
    n = S5_GROUPS * S5_STATE
    nc = n // LANES

    @pl.when(pl.program_id(0) == 0)
    def _():
        st_ref[...] = jnp.zeros(st_ref.shape, F32)

    u = u_ref[...].reshape(nb * ts, u_ref.shape[-1])
    bu = _dot(u.astype(BF16), bm_ref[...])
    for c, x_ref in enumerate(xs_refs):
        for b in range(nb):
            x_ref[pl.ds(b, ts, stride=nb), :] = bu[b * ts:(b + 1) * ts, c * LANES:(c + 1) * LANES]
    a_re = jnp.broadcast_to(are_ref[...], (nb, n))
    a_im = jnp.broadcast_to(aim_ref[...], (nb, n))

    def step(t, carry):
        x_re, x_im = carry
        rows = pl.ds(pl.multiple_of(t * nb, nb), nb)
        bu_re = jnp.concatenate([x_ref[rows, :] for x_ref in xs_refs[:nc]], axis=1)
        bu_im = jnp.concatenate([x_ref[rows, :] for x_ref in xs_refs[nc:]], axis=1)
        n_re = a_re * x_re - a_im * x_im + bu_re
        n_im = a_re * x_im + a_im * x_re + bu_im
        for c in range(nc):
            xs_refs[c][rows, :] = n_re[:, c * LANES:(c + 1) * LANES]
            xs_refs[nc + c][rows, :] = n_im[:, c * LANES:(c + 1) * LANES]
        return n_re, n_im

    x_re, x_im = lax.fori_loop(0, ts, step, (st_ref[:, 0:n], st_ref[:, n:2 * n]))
    st_ref[:, 0:n] = x_re
    st_ref[:, n:2 * n] = x_im

    xs = jnp.concatenate(
        [jnp.concatenate([x_ref[pl.ds(b, ts, stride=nb), :].astype(BF16) for x_ref in xs_refs], axis=1)
         for b in range(nb)], axis=0)
    y = _dot(xs, cm_ref[...]) + d_ref[...] * u
    t = _dot(jax.nn.gelu(y).astype(BF16), wg_ref[...]) + bg_ref[...]
    o_ref[...] = (t[:, :BRANCH_WIDTH] * jax.nn.sigmoid(t[:, BRANCH_WIDTH:])).reshape(o_ref.shape)


def _s5_params(lam_re, lam_im, log_dt, b_re, b_im, c_re, c_im):
    g, n, p = S5_GROUPS, S5_STATE, S5_GROUP
    lr, li = lam_re.astype(F32), lam_im.astype(F32)
    dt = jnp.exp(log_dt.astype(F32))[:, None]
    mag = jnp.exp(lr * dt)
    a_re, a_im = mag * jnp.cos(li * dt), mag * jnp.sin(li * dt)
    den = lr * lr + li * li
    n_re, n_im = a_re - 1.0, a_im
    z_re = (n_re * lr + n_im * li) / den
    z_im = (n_im * lr - n_re * li) / den
    br, bi = b_re.astype(F32), b_im.astype(F32)
    bb_re = z_re[..., None] * br - z_im[..., None] * bi
    bb_im = z_re[..., None] * bi + z_im[..., None] * br
    eye = jnp.eye(g, dtype=F32)
    blockdiag_in = lambda t: jnp.einsum("gnp,gh->gphn", t, eye).reshape(g * p, g * n)
    blockdiag_out = lambda t: jnp.einsum("gpn,gh->gnhp", t, eye).reshape(g * n, g * p)
    bm = jnp.concatenate([blockdiag_in(bb_re), blockdiag_in(bb_im)], axis=1)
    cm = jnp.concatenate([blockdiag_out(c_re.astype(F32)), -blockdiag_out(c_im.astype(F32))], axis=0)
    return bm.astype(BF16), cm.astype(BF16), a_re.reshape(1, g * n), a_im.reshape(1, g * n)


def _s5_branch(u, params, d_skip, w_glu, b_glu):
    nb, s_len, w = u.shape
    assert nb == 8
    ts = S5_CHUNK
    bm, cm, a_re, a_im = params
    n2 = bm.shape[1]
    blk = pl.BlockSpec((nb, ts, w), lambda i: (0, i, 0))
    return pl.pallas_call(
        functools.partial(_s5_kernel, ts=ts, nb=nb),
        grid=(s_len // ts,),
        in_specs=[blk, _const_spec(bm.shape), _const_spec(cm.shape), _const_spec(a_re.shape),
                  _const_spec(a_im.shape), _const_spec((1, w)), _const_spec(w_glu.shape),
                  _const_spec((1, 2 * w))],
        out_specs=blk,
        out_shape=jax.ShapeDtypeStruct((nb, s_len, w), F32),
        scratch_shapes=[pltpu.VMEM((nb, n2), F32)] + [pltpu.VMEM((ts * nb, LANES), F32)] * (n2 // LANES),
        compiler_params=pltpu.CompilerParams(dimension_semantics=("arbitrary",),
                                             vmem_limit_bytes=BIG_VMEM_LIMIT),
        name="s5_scan",
    )(u, bm, cm, a_re, a_im, d_skip.astype(F32).reshape(1, w), w_glu.astype(BF16),
      b_glu.astype(F32).reshape(1, 2 * w))


def _compress_kernel(r_ref, pe_top_ref, pe_bot_ref, w_top_ref, w_bot_ref, w2_ref, k_ref, v_ref):
    r = r_ref[...]
    top = (r + pe_top_ref[...]).astype(BF16)
    nxt = pltpu.roll(r, r.shape[0] - 1, 0)
    bot = (nxt + pe_bot_ref[...]).astype(BF16)
    hid = jax.nn.gelu(_dot(top, w_top_ref[...]) + _dot(bot, w_bot_ref[...]))
    k2, v2 = _shared_kv_variants(_dot(hid.astype(BF16), w2_ref[...]))
    k_ref[...] = k2.astype(BF16)
    v_ref[...] = v2.astype(BF16)


def _compress(kvc, pe, w1, w2):
    b, s_len, _ = kvc.shape
    nr = s_len // CMP_STRIDE
    per = CMP_BLOCK // CMP_STRIDE
    assert per == 2
    hid = w1.shape[-1]
    r = kvc.reshape(b, nr, CMP_STRIDE * 2 * HEAD_DIM)
    w1r = w1.astype(F32).reshape(2, per, CMP_STRIDE, HEAD_DIM, hid)
    per_r = pe.astype(F32).reshape(2, per, CMP_STRIDE, HEAD_DIM)

    def expand(j):
        wk = jnp.pad(w1r[0, j], ((0, 0), (0, HEAD_DIM), (0, hid)))
        wv = jnp.pad(w1r[1, j], ((0, 0), (HEAD_DIM, 0), (hid, 0)))
        return (wk + wv).reshape(CMP_STRIDE * 2 * HEAD_DIM, 2 * hid).astype(BF16)

    pe_rows = [jnp.concatenate([per_r[0, j], per_r[1, j]], axis=-1).reshape(1, -1) for j in range(per)]
    z = jnp.zeros((hid, HEAD_DIM), F32)
    w2f = w2.astype(F32)
    w2x = jnp.concatenate([jnp.concatenate([w2f[0], z], axis=1),
                           jnp.concatenate([z, w2f[1]], axis=1)], axis=0).astype(BF16)
    w_top, w_bot = expand(0), expand(1)
    blk = pl.BlockSpec((None, nr, r.shape[-1]), lambda i: (i, 0, 0))
    out = pl.BlockSpec((None, nr, BRANCH_WIDTH), lambda i: (i, 0, 0))
    return pl.pallas_call(
        _compress_kernel,
        grid=(b,),
        in_specs=[blk, _const_spec(pe_rows[0].shape), _const_spec(pe_rows[1].shape),
                  _const_spec(w_top.shape), _const_spec(w_bot.shape), _const_spec(w2x.shape)],
        out_specs=[out, out],
        out_shape=[jax.ShapeDtypeStruct((b, nr, BRANCH_WIDTH), BF16)] * 2,
        compiler_params=pltpu.CompilerParams(dimension_semantics=("arbitrary",)),
        name="nsa_compress",
    )(r, pe_rows[0], pe_rows[1], w_top, w_bot, w2x)


def _nsa_kernel(dq_ref, dqr_ref, dg_ref, kc_ref, vc_ref, ks_ref, vs_ref, kw_ref, vw_ref,
                ovt_ref, exp_ref, gsel_ref, wbias_ref, o_ref, *scratch, tq, n_cmp, n_slc, n_sel):
    qi = pl.program_id(1)
    t0 = qi * tq
    nr = kc_ref.shape[0]
    low = _low_half()
    q_cols = lambda h: slice((h // 2) * LANES, (h // 2 + 1) * LANES)
    kv_col = lambda h: (h % 2) * LANES

    dq = dq_ref[...]
    tpos = t0 + lax.broadcasted_iota(jnp.int32, (tq, nr), 0)
    ci = lax.broadcasted_iota(jnp.int32, (tq, nr), 1)
    cmask = (ci * CMP_STRIDE + (CMP_BLOCK - 1) <= tpos) & (ci < n_cmp)
    p_sum = jnp.zeros((tq, nr), F32)
    o_heads = []
    for h in range(N_HEADS):
        s = jnp.where(cmask, _dot_nt(dq[:, q_cols(h)], kc_ref[:, kv_col(h):kv_col(h) + LANES]), NEG_INF)
        m = jnp.max(s, axis=-1, keepdims=True)
        e = jnp.where(cmask, jnp.exp(s - m), 0.0)
        p = e / jnp.maximum(jnp.sum(e, axis=-1, keepdims=True), 1e-30)
        p_sum = p_sum + p
        o_heads.append(_dot(p.astype(BF16), vc_ref[:, kv_col(h):kv_col(h) + LANES]))
    o_cmp = jnp.concatenate([jnp.where(low, o_heads[0], o_heads[1]), jnp.where(low, o_heads[2], o_heads[3])],
                            axis=1)

    p_hi, p_lo = _split_hi_lo(p_sum)
    ovt = ovt_ref[...]
    imp_t = _dot_nt(ovt, p_hi) + _dot_nt(ovt, p_lo)
    blk = lax.broadcasted_iota(jnp.int32, (n_slc, tq), 0)
    qblk = lax.shift_right_arithmetic(t0 + lax.broadcasted_iota(jnp.int32, (n_slc, tq), 1),
                                      int(math.log2(SLC_BLOCK)))
    forced = jnp.where(blk == 0, 1, jnp.where(blk == qblk, 1, jnp.where(blk == qblk - 1, 1, 0)))
    score = jnp.where(blk <= qblk, jnp.where(forced > 0, FORCE_SCORE, imp_t), NEG_INF)
    rank = jnp.zeros((n_slc, tq), F32)
    for i in range(n_slc):
        si = score[i:i + 1, :]
        tie = jnp.where(blk > i, 1.0, 0.0)
        rank = rank + jnp.where(si > score, 1.0, jnp.where(si == score, tie, 0.0))
    sel_bias_t = jnp.where(rank < n_sel, jnp.where(score > 0.5 * NEG_INF, 0.0, NEG_INF), NEG_INF)
    pad_rows = exp_ref.shape[1] - n_slc
    sel_bias = jnp.concatenate([sel_bias_t, jnp.full((pad_rows, tq), NEG_INF, F32)], axis=0).T.astype(BF16)

    dqr = dqr_ref[...]
    pairs = lambda acc: jnp.concatenate([_normalise_pair(acc[0], acc[1]), _normalise_pair(acc[2], acc[3])], axis=1)
    chains = [(dqr[:, q_cols(h)], (ks_ref, kv_col(h)), (vs_ref, kv_col(h))) for h in range(N_HEADS)]
    o_slc = pairs(_strip_attention(chains, tq, 0, qi + 1, lambda kb: _dot(sel_bias, exp_ref[kb]),
                                   _causal_bias(tq), scratch))
    chains = [(dqr[:, q_cols(h)], (kw_ref, kv_col(h)), (vw_ref, kv_col(h))) for h in range(N_HEADS)]
    win_lo = jnp.maximum(qi - (wbias_ref.shape[0] - 1), 0)
    o_win = pairs(_strip_attention(chains, tq, win_lo, qi + 1, lambda kb: wbias_ref[qi - kb], None, scratch))

    gates = _dot_hilo(jax.nn.sigmoid(dg_ref[...]), gsel_ref[...])
    w = BRANCH_WIDTH
    o_ref[...] = gates[:, :w] * o_cmp + gates[:, w:2 * w] * o_slc + gates[:, 2 * w:] * o_win


def _nsa_constants(tq, s_len):
    n_cmp = (s_len - CMP_BLOCK) // CMP_STRIDE + 1
    n_slc = s_len // SLC_BLOCK
    nr = s_len // CMP_STRIDE
    c0 = np.arange(n_cmp)[:, None] * CMP_STRIDE
    s0 = np.arange(n_slc)[None, :] * SLC_BLOCK
    overlap = np.clip(np.minimum(c0 + CMP_BLOCK, s0 + SLC_BLOCK) - np.maximum(c0, s0), 0, None) / CMP_STRIDE
    ovt = np.zeros((n_slc, nr), np.float32)
    ovt[:, :n_cmp] = overlap.T
    rows = -(-n_slc // 128) * 128
    expand = np.zeros((s_len // tq, rows, tq), np.float32)
    tok = np.arange(s_len)
    expand[tok // tq, tok // SLC_BLOCK, tok % tq] = 1.0
    gsel = np.zeros((128, 3 * BRANCH_WIDTH), np.float32)
    for h in range(N_HEADS):
        for j in range(3):
            gsel[h * 3 + j, j * BRANCH_WIDTH + h * HEAD_DIM:j * BRANCH_WIDTH + (h + 1) * HEAD_DIM] = 1.0
    nwin = -(-WINDOW // tq) + 1
    d = np.arange(nwin)[:, None, None] * tq + np.arange(tq)[None, :, None] - np.arange(tq)[None, None, :]
    wbias = np.where((d >= 0) & (d < WINDOW), 0.0, NEG_INF).astype(np.float32)
    return n_cmp, n_slc, ovt, expand, gsel, wbias


def _nsa_attention(dq, dqr, dg, k_cmp, v_cmp, ks, vs, kw, vw):
    b, s_len, w = dq.shape
    tq = ATT_TILE
    n_cmp, n_slc, ovt, expand, gsel, wbias = _nsa_constants(tq, s_len)
    q_spec, kv_spec, _ = _att_specs(tq, s_len, w, w)
    nr = k_cmp.shape[1]
    cmp_spec = pl.BlockSpec((None, nr, w), lambda bi, i: (bi, 0, 0))
    g_spec = pl.BlockSpec((None, tq, dg.shape[-1]), lambda bi, i: (bi, i, 0))
    consts = [jnp.asarray(ovt, BF16), jnp.asarray(expand, BF16), jnp.asarray(gsel, BF16), jnp.asarray(wbias)]
    return pl.pallas_call(
        functools.partial(_nsa_kernel, tq=tq, n_cmp=n_cmp, n_slc=n_slc, n_sel=min(N_SELECT, n_slc)),
        grid=(b, s_len // tq),
        in_specs=[q_spec, q_spec, g_spec, cmp_spec, cmp_spec, kv_spec, kv_spec, kv_spec, kv_spec]
        + [_const_spec(c.shape) for c in consts],
        out_specs=q_spec,
        out_shape=jax.ShapeDtypeStruct((b, s_len, w), F32),
        scratch_shapes=_strip_scratch(N_HEADS, tq, s_len),
        compiler_params=pltpu.CompilerParams(dimension_semantics=("arbitrary", "arbitrary"),
                                             vmem_limit_bytes=BIG_VMEM_LIMIT),
        name="nsa_attention",
    )(dq, dqr, dg, k_cmp, v_cmp, ks, vs, kw, vw, *consts)


def _mem_kv_kernel(m_ref, g_ref, w_ref, k_ref, v_ref):
    kv = _dot(_rms(m_ref[...], g_ref[...]).astype(BF16), w_ref[...])
    k_ref[...] = kv[:, :BRANCH_WIDTH].astype(BF16)
    v_ref[...] = kv[:, BRANCH_WIDTH:].astype(BF16)


def _mem_kv(mem2, g, w_kv):
    rows, d = mem2.shape
    tm = ROW_TILE
    out = pl.BlockSpec((tm, BRANCH_WIDTH), lambda i: (i, 0))
    return pl.pallas_call(
        _mem_kv_kernel,
        grid=(rows // tm,),
        in_specs=[pl.BlockSpec((tm, d), lambda i: (i, 0)), _const_spec((1, d)), _const_spec(w_kv.shape)],
        out_specs=[out, out],
        out_shape=[jax.ShapeDtypeStruct((rows, BRANCH_WIDTH), BF16)] * 2,
        compiler_params=pltpu.CompilerParams(dimension_semantics=("arbitrary",)),
        name="mem_kv",
    )(mem2, g.astype(F32).reshape(1, d), w_kv.astype(BF16))


def _mem_attn_kernel(q_ref, k_ref, v_ref, o_ref):
    q = q_ref[...]
    k = k_ref[...]
    v = v_ref[...]
    o = jnp.zeros(o_ref.shape, F32)
    for h in range(N_HEADS):
        s = _dot_nt(q * _head_mask(h, BF16), k)
        e = jnp.exp(s - jnp.max(s, axis=-1, keepdims=True))
        oh = _dot(e.astype(BF16), v) / jnp.sum(e, axis=-1, keepdims=True)
        o = jnp.where(_head_mask(h, F32) > 0.5, oh, o)
    o_ref[...] = o


def _mem_attention(q, k, v):
    b, s_len, w = q.shape
    tq = MEM_ATT_TILE
    q_spec = pl.BlockSpec((None, tq, w), lambda bi, i: (bi, i, 0))
    kv_spec = pl.BlockSpec((None, k.shape[1], w), lambda bi, i: (bi, 0, 0))
    return pl.pallas_call(
        _mem_attn_kernel,
        grid=(b, s_len // tq),
        in_specs=[q_spec, kv_spec, kv_spec],
        out_specs=q_spec,
        out_shape=jax.ShapeDtypeStruct((b, s_len, w), F32),
        compiler_params=pltpu.CompilerParams(dimension_semantics=("arbitrary", "arbitrary")),
        name="mem_attention",
    )(q, k, v)


def _merge_kernel(x_ref, g_ref, oa_ref, ob_ref, oc_ref, od_ref, oe_ref, zs_ref, wm_ref, bm_ref, wb_ref,
                  wo_ref, fg_ref, o_ref, *, final_norm):
    x = x_ref[...]
    d = x.shape[-1]
    w = BRANCH_WIDTH
    h = _rms(x, g_ref[...]).astype(BF16)
    mixed = jnp.zeros(x.shape, F32)
    for n, br_ref in enumerate((oa_ref, ob_ref, oc_ref, od_ref, oe_ref)):
        br = (br_ref[...] * zs_ref[:, n * w:(n + 1) * w]).astype(BF16)
        y = _dot(br, wb_ref[n])
        gate = jax.nn.sigmoid(_dot(h, wm_ref[:, n * d:(n + 1) * d]) + bm_ref[:, n * d:(n + 1) * d])
        mixed = mixed + gate * y
    out = x + _dot(mixed.astype(BF16), wo_ref[...])
    if final_norm:
        out = _rms(out, fg_ref[...])
    o_ref[...] = out


def _merge(x2, g, branches, zs, w_merge, b_merge, w_branch, w_out, final_g, final_norm):
    t, d = x2.shape
    tm = ROW_TILE
    row = lambda wd: pl.BlockSpec((tm, wd), lambda i: (i, 0))
    return pl.pallas_call(
        functools.partial(_merge_kernel, final_norm=final_norm),
        grid=(t // tm,),
        in_specs=[row(d), _const_spec((1, d))] + [row(BRANCH_WIDTH)] * N_BRANCHES + [row(zs.shape[1])]
        + [_const_spec(w_merge.shape), _const_spec((1, N_BRANCHES * d)), _const_spec(w_branch.shape),
           _const_spec(w_out.shape), _const_spec((1, d))],
        out_specs=row(d),
        out_shape=jax.ShapeDtypeStruct((t, d), F32),
        compiler_params=pltpu.CompilerParams(dimension_semantics=("arbitrary",),
                                             vmem_limit_bytes=BIG_VMEM_LIMIT),
        name="merge",
    )(x2, g.astype(F32).reshape(1, d), *branches, zs, w_merge.astype(BF16),
      b_merge.astype(F32).reshape(1, -1), w_branch.astype(BF16), w_out.astype(BF16),
      final_g.astype(F32).reshape(1, d))


def kernel(x, mem, norm_g, w_in, diff_lambda, diff_subln_g, s5_lambda_re, s5_lambda_im, s5_log_dt,
           s5_b_re, s5_b_im, s5_c_re, s5_c_im, s5_d, w_glu, b_glu, nsa_pe, nsa_w1, nsa_w2, mem_norm_g,
           w_mem_kv, w_merge, b_merge, w_branch, w_out, final_g):
    bsz, s_len, d = x.shape
    depth = w_in.shape[0]
    t = bsz * s_len
    w = BRANCH_WIDTH
    tables = _rope_tables(s_len)
    x2 = x.astype(F32).reshape(t, d)
    mem2 = mem.astype(F32).reshape(-1, d)
    for l in range(depth):
        proj = dict(zip([n for n, _, _ in _IN_OUTS],
                        _in_proj(x2, norm_g[l].astype(F32), _in_weights(w_in[l].astype(F32)), tables, s_len)))
        seq = lambda name: proj[name].reshape(bsz, s_len, -1)

        dl = diff_lambda[l].astype(F32)
        lam_init = 0.8 - 0.6 * math.exp(-0.3 * l)
        lam = jnp.exp(jnp.sum(dl[0] * dl[1])) - jnp.exp(jnp.sum(dl[2] * dl[3])) + lam_init
        o_a = _diff_attention(seq("qa"), seq("ka"), seq("va"), lam, diff_subln_g[l], lam_init)

        o_b = _dilated_attention(seq("qb"), seq("kb"), seq("vb"))

        s5p = _s5_params(s5_lambda_re[l], s5_lambda_im[l], s5_log_dt[l], s5_b_re[l], s5_b_im[l],
                         s5_c_re[l], s5_c_im[l])
        o_c = _s5_branch(seq("cu"), s5p, s5_d[l], w_glu[l], b_glu[l])

        k_cmp, v_cmp = _compress(seq("kvc"), nsa_pe[l], nsa_w1[l], nsa_w2[l])
        o_d = _nsa_attention(seq("dq"), seq("dqr"), seq("dg"), k_cmp, v_cmp,
                             seq("ks"), seq("vs"), seq("kw"), seq("vw"))

        k_mem, v_mem = _mem_kv(mem2, mem_norm_g[l], w_mem_kv[l])
        o_e = _mem_attention(seq("eq"), k_mem.reshape(bsz, -1, w), v_mem.reshape(bsz, -1, w))

        branches = [o.reshape(t, w) for o in (o_a, o_b, o_c, o_d, o_e)]
        x2 = _merge(x2, norm_g[l], branches, proj["zs"], w_merge[l], b_merge[l], w_branch[l], w_out[l],
                    final_g, final_norm=(l == depth - 1))
    return x2.reshape(bsz, s_len, d).astype(x.dtype)
```

```python
import functools
import math

import numpy as np
import jax
import jax.numpy as jnp
from jax import lax
from jax.experimental import pallas as pl
from jax.experimental.pallas import tpu as pltpu

F32 = jnp.float32
BF16 = jnp.bfloat16

HEAD_DIM = 64
BRANCH_WIDTH = 256
N_HEADS = 4
N_BRANCHES = 5
DIFF_QK_DIM = 32
DIL_PATTERNS = ((128, 1), (512, 4), (2048, 16))
S5_GROUP = 16
S5_GROUPS = 16
S5_STATE = 64
CMP_BLOCK = 32
CMP_STRIDE = 16
SLC_BLOCK = 64
N_SELECT = 16
WINDOW = 512
ROPE_THETA = 10000.0
RMS_EPS = 1e-6
NEG_INF = -1e30
FORCE_SCORE = 1e9

V7X_VMEM_BYTES = 64 * 1024 * 1024
BIG_VMEM_LIMIT = V7X_VMEM_BYTES - 8 * 1024 * 1024

LANES = 128
ATT_TILE = 512
ROW_TILE = 512
MERGE_ROW_TILE = 256
MEM_ATT_TILE = 1024
S5_CHUNK = 128

_NT = (((1,), (1,)), ((), ()))


def _rms(x, g):
    return x * lax.rsqrt(jnp.mean(x * x, axis=-1, keepdims=True) + RMS_EPS) * g


def _dot(a, b):
    return jnp.dot(a, b, preferred_element_type=F32)


def _dot_nt(a, b):
    return lax.dot_general(a, b, _NT, preferred_element_type=F32)


def _split_hi_lo(x):
    hi = x.astype(BF16)
    lo = (x - hi.astype(F32)).astype(BF16)
    return hi, lo


def _dot_hilo(x, w):
    hi, lo = _split_hi_lo(x)
    return _dot(hi, w) + _dot(lo, w)


def _const_spec(shape):
    n = len(shape)
    return pl.BlockSpec(shape, lambda *_: (0,) * n, pipeline_mode=pl.Buffered(1))


def _head_mask(h, dtype):
    lane = lax.broadcasted_iota(jnp.int32, (1, BRANCH_WIDTH), 1)
    return jnp.where((lane >= h * HEAD_DIM) & (lane < (h + 1) * HEAD_DIM), 1.0, 0.0).astype(dtype)


def _low_half():
    return lax.broadcasted_iota(jnp.int32, (1, LANES), 1) < HEAD_DIM


def _swap_halves(x):
    return pltpu.roll(x, HEAD_DIM, 1)


def _augment_heads(v):
    low = _low_half()
    parts = []
    for j in range(v.shape[1] // LANES):
        pair = v[:, j * LANES:(j + 1) * LANES]
        parts += [jnp.where(low, pair, 1.0), jnp.where(low, 1.0, pair)]
    return jnp.concatenate(parts, axis=1)


def _shared_kv_variants(kv):
    low = _low_half()
    sw = _swap_halves(kv)
    k2 = jnp.concatenate([jnp.where(low, kv, 0.0), jnp.where(low, 0.0, sw)], axis=1)
    v2 = jnp.concatenate([jnp.where(low, sw, 1.0), jnp.where(low, 1.0, kv)], axis=1)
    return k2, v2


def _normalise_pair(acc_even, acc_odd):
    return jnp.where(_low_half(), acc_even / _swap_halves(acc_even), acc_odd / _swap_halves(acc_odd))


_IN_COLS = (("qa", 256), ("ka", 256), ("va", 256), ("qb", 256), ("kb", 256), ("vb", 256), ("cu", 256),
            ("dq", 256), ("kvc", 128), ("kvs", 128), ("kvw", 128), ("dg", 128), ("eq", 256), ("zs", 1280))
_IN_OFFS = dict(zip([n for n, _ in _IN_COLS], np.cumsum([0] + [w for _, w in _IN_COLS])[:-1].tolist()))
_IN_WIDTH = dict(_IN_COLS)
_IN_OUTS = (("qa", 256, BF16), ("ka", 256, BF16), ("va", 512, BF16),
            ("qb", 256, BF16), ("kb", 256, BF16), ("vb", 512, BF16),
            ("cu", 256, F32), ("dq", 256, BF16), ("dqr", 256, BF16), ("kvc", 128, F32),
            ("ks", 256, BF16), ("vs", 256, BF16), ("kw", 256, BF16), ("vw", 256, BF16),
            ("dg", 128, F32), ("eq", 256, BF16), ("zs", 1280, F32))


def _rotate_half(y, group):
    half = group // 2
    lane = lax.broadcasted_iota(jnp.int32, (1, LANES), 1)
    first = (lane & (group - 1)) < half
    parts = []
    for j in range(y.shape[1] // LANES):
        v = y[:, j * LANES:(j + 1) * LANES]
        parts.append(jnp.where(first, pltpu.roll(v, LANES - half, 1), pltpu.roll(v, half, 1)))
    return jnp.concatenate(parts, axis=1)


def _in_proj_kernel(x_ref, g_ref, w_ref, cosa_ref, sina_ref, cosb_ref, sinb_ref, cosk_ref, sink_ref, *out_refs):
    out = dict(zip([n for n, _, _ in _IN_OUTS], out_refs))
    h = _rms(x_ref[...], g_ref[...]).astype(BF16)

    def proj(name):
        off = _IN_OFFS[name]
        return _dot(h, w_ref[:, off:off + _IN_WIDTH[name]])

    def rope(y, cos_ref, sin_ref, group):
        return y * cos_ref[...] + _rotate_half(y, group) * sin_ref[...]

    def put(name, y):
        out[name][...] = y.astype(out[name].dtype)

    put("qa", rope(proj("qa"), cosa_ref, sina_ref, DIFF_QK_DIM))
    put("ka", rope(proj("ka"), cosa_ref, sina_ref, DIFF_QK_DIM))
    put("va", _augment_heads(proj("va")))
    put("qb", rope(proj("qb"), cosb_ref, sinb_ref, HEAD_DIM))
    put("kb", rope(proj("kb"), cosb_ref, sinb_ref, HEAD_DIM))
    put("vb", _augment_heads(proj("vb")))
    put("cu", proj("cu"))
    dq = proj("dq")
    put("dq", dq)
    put("dqr", rope(dq, cosb_ref, sinb_ref, HEAD_DIM))
    put("kvc", proj("kvc"))
    for kv_name, k_name, v_name in (("kvs", "ks", "vs"), ("kvw", "kw", "vw")):
        k2, v2 = _shared_kv_variants(rope(proj(kv_name), cosk_ref, sink_ref, HEAD_DIM))
        put(k_name, k2)
        put(v_name, v2)
    put("dg", proj("dg"))
    put("eq", proj("eq"))
    z = proj("zs")
    put("zs", z * jax.nn.sigmoid(z))


def _in_weights(w):
    sizes = (256,) * 10 + (256, 64, 64, 64, 64, 64, 64, 12, 256, 256, 256)
    offs = np.cumsum((0,) + sizes)
    (a_q, a_k, a_v, a_z, b_q, b_k, b_v, b_z, c_u, c_z, d_q, d_kc, d_vc, d_ks, d_vs, d_kw, d_vw,
     d_g, d_z, e_q, e_z) = [w[:, offs[i]:offs[i + 1]] for i in range(len(sizes))]
    scale = HEAD_DIM ** -0.5
    cat = lambda *t: jnp.concatenate(t, axis=1)
    cols = [a_q, a_k, a_v, b_q * scale, b_k, b_v, c_u, d_q * scale, cat(d_kc, d_vc), cat(d_ks, d_vs),
            cat(d_kw, d_vw), jnp.pad(d_g, ((0, 0), (0, 128 - d_g.shape[1]))), e_q * scale,
            a_z, b_z, c_z, d_z, e_z]
    return cat(*cols).astype(BF16)


def _rope_tables(s_len):
    def table(group, width):
        half = group // 2
        inv_freq = ROPE_THETA ** (-jnp.arange(half, dtype=F32) / half)
        ang = jnp.arange(s_len, dtype=F32)[:, None] * inv_freq[None, :]
        cos = jnp.tile(jnp.cos(ang), (1, width // half))
        sin = jnp.tile(jnp.concatenate([-jnp.sin(ang), jnp.sin(ang)], axis=1), (1, width // group))
        return cos, sin
    cos_a, sin_a = table(DIFF_QK_DIM, BRANCH_WIDTH)
    cos_b, sin_b = table(HEAD_DIM, BRANCH_WIDTH)
    cos_k = jnp.concatenate([cos_b[:, :HEAD_DIM], jnp.ones((s_len, HEAD_DIM), F32)], axis=1)
    sin_k = jnp.concatenate([sin_b[:, :HEAD_DIM], jnp.zeros((s_len, HEAD_DIM), F32)], axis=1)
    return cos_a, sin_a, cos_b, sin_b, cos_k, sin_k


def _in_proj(x2, g, wcat, tables, s_len):
    t, d = x2.shape
    tm = ROW_TILE
    nsb = s_len // tm
    row = lambda w: pl.BlockSpec((tm, w), lambda i: (i, 0))
    tab = lambda a: pl.BlockSpec((tm, a.shape[1]), lambda i: (i % nsb, 0))
    return pl.pallas_call(
        _in_proj_kernel,
        grid=(t // tm,),
        in_specs=[row(d), _const_spec((1, d)), _const_spec(wcat.shape)] + [tab(a) for a in tables],
        out_specs=[row(w) for _, w, _ in _IN_OUTS],
        out_shape=[jax.ShapeDtypeStruct((t, w), dt) for _, w, dt in _IN_OUTS],
        compiler_params=pltpu.CompilerParams(dimension_semantics=("arbitrary",),
                                             vmem_limit_bytes=BIG_VMEM_LIMIT),
        name="in_proj",
    )(x2, g.reshape(1, d), wcat, *tables)


def _lane_fold(x, op):
    parts = [x[:, j * LANES:(j + 1) * LANES] for j in range(x.shape[1] // LANES)]
    return functools.reduce(op, parts)


def _strip_attention(chains, tq, kb_lo, kb_hi, bias_fn, last_bias, scratch, exp_scale=1.0):
    s_ref, m_ref, acc_ref = scratch
    n = len(chains)
    reps = tq // LANES
    c1 = exp_scale * math.log2(math.e)

    def tile(cache, ref, col, width, kb, ntiles=1):
        key = (id(ref), col, width)
        if key not in cache:
            cache[key] = ref[pl.ds(pl.multiple_of(kb * tq, tq), ntiles * tq), col:col + width]
        return cache[key]

    def scores(kb, extra):
        bias = bias_fn(kb) if bias_fn is not None else None
        if extra is not None:
            bias = extra if bias is None else bias + extra
        cache, out = {}, []
        for q, (k_ref, k_col), _ in chains:
            s = _dot_nt(q, tile(cache, k_ref, k_col, q.shape[1], kb)) * c1
            out.append(s if bias is None else s + bias)
        return out

    def tile_pairs(lo, hi, step):
        def two(j, carry):
            step(lo + 2 * j, 2)
            return carry
        cnt = hi - lo
        lax.fori_loop(0, lax.shift_right_arithmetic(cnt, 1), two, 0)

        @pl.when((cnt & 1) == 1)
        def _():
            step(hi - 1, 1)

    m_ref[0:n] = jnp.full((n,) + m_ref.shape[1:], NEG_INF, F32)

    def pass1(kb, ntiles):
        tiles = [scores(kb + j, None) for j in range(ntiles)]
        for i in range(n):
            m = m_ref[i]
            for j in range(ntiles):
                s_ref[i, kb + j] = tiles[j][i]
                m = jnp.maximum(m, _lane_fold(tiles[j][i], jnp.maximum))
            m_ref[i] = m

    tile_pairs(kb_lo, kb_hi - 1, pass1)
    for i, s in enumerate(scores(kb_hi - 1, last_bias)):
        s_ref[i, kb_hi - 1] = s
        m = jnp.max(jnp.maximum(m_ref[i], _lane_fold(s, jnp.maximum)), axis=-1, keepdims=True)
        m_ref[i] = jnp.broadcast_to(m, m_ref.shape[1:])
        acc_ref[i] = jnp.zeros(acc_ref.shape[1:], F32)

    def pass2(kb, ntiles):
        cache = {}
        for i, (_, _, (v_ref, v_col)) in enumerate(chains):
            m = jnp.concatenate([m_ref[i]] * reps, axis=1)
            p = [jnp.exp2(s_ref[i, kb + j] - m).astype(BF16) for j in range(ntiles)]
            p = p[0] if ntiles == 1 else jnp.concatenate(p, axis=1)
            acc_ref[i] += _dot(p, tile(cache, v_ref, v_col, LANES, kb, ntiles))

    tile_pairs(kb_lo, kb_hi, pass2)
    return [acc_ref[i] for i in range(n)]


def _strip_scratch(n, tq, s_len):
    return [pltpu.VMEM((n, s_len // tq, tq, tq), F32), pltpu.VMEM((n, tq, LANES), F32),
            pltpu.VMEM((n, tq, LANES), F32)]


def _causal_bias(tq):
    r = lax.broadcasted_iota(jnp.int32, (tq, tq), 0)
    c = lax.broadcasted_iota(jnp.int32, (tq, tq), 1)
    return jnp.where(c <= r, 0.0, NEG_INF).astype(F32)


def _att_specs(tq, s_len, k_width, v_width):
    q_spec = pl.BlockSpec((None, tq, BRANCH_WIDTH), lambda b, i: (b, i, 0))
    kv = lambda w: pl.BlockSpec((None, s_len, w), lambda b, i: (b, 0, 0))
    return q_spec, kv(k_width), kv(v_width)


def _diff_kernel(lam_ref, q_ref, k_ref, v_ref, g_ref, hm_ref, o_ref, *scratch, tq, out_scale):
    qi = pl.program_id(1)
    q = q_ref[...]
    lane = lax.broadcasted_iota(jnp.int32, (1, BRANCH_WIDTH), 1)
    lam = lam_ref[0]
    chains = []
    for hc in range(2 * N_HEADS):
        lo = hc * DIFF_QK_DIM
        cmask = jnp.where((lane >= lo) & (lane < lo + DIFF_QK_DIM), 1.0, 0.0).astype(BF16)
        chains.append((q * cmask, (k_ref, 0), (v_ref, (hc // 2) * LANES)))
    acc = _strip_attention(chains, tq, 0, qi + 1, None, _causal_bias(tq), scratch,
                           exp_scale=DIFF_QK_DIM ** -0.5)
    halves = []
    for pair in range(N_HEADS // 2):
        even, odd = 4 * pair, 4 * pair + 2
        halves.append(_normalise_pair(acc[even], acc[odd]) - lam * _normalise_pair(acc[even + 1], acc[odd + 1]))
    o = jnp.concatenate(halves, axis=1)
    ms = _dot_hilo(o * o, hm_ref[...])
    o_ref[...] = o * lax.rsqrt(ms + RMS_EPS) * g_ref[...] * out_scale


def _diff_attention(q, k, v_aug, lam, subln_g, lam_init):
    b, s_len, w = q.shape
    tq = ATT_TILE
    q_spec, k_spec, v_spec = _att_specs(tq, s_len, w, v_aug.shape[-1])
    head = np.arange(w) // HEAD_DIM
    hm = jnp.asarray((head[:, None] == head[None, :]) / HEAD_DIM, dtype=BF16)
    g = jnp.tile(subln_g.astype(F32), N_HEADS).reshape(1, w)
    return pl.pallas_call(
        functools.partial(_diff_kernel, tq=tq, out_scale=1.0 - lam_init),
        grid=(b, s_len // tq),
        in_specs=[pl.BlockSpec(memory_space=pltpu.SMEM), q_spec, k_spec, v_spec,
                  _const_spec((1, w)), _const_spec((w, w))],
        out_specs=q_spec,
        out_shape=jax.ShapeDtypeStruct((b, s_len, w), F32),
        scratch_shapes=_strip_scratch(2 * N_HEADS, tq, s_len),
        compiler_params=pltpu.CompilerParams(dimension_semantics=("arbitrary", "arbitrary"),
                                             vmem_limit_bytes=BIG_VMEM_LIMIT),
        name="diff_attention",
    )(lam.reshape(1), q, k, v_aug, g, hm)


def _dil_kernel(q_ref, k_ref, v_ref, bias_ref, o_ref, *scratch, tq):
    qi = pl.program_id(1)
    q = q_ref[...]
    chains = [(q * _head_mask(h, BF16), (k_ref, 0), (v_ref, h * LANES)) for h in range(N_HEADS)]
    acc = _strip_attention(chains, tq, 0, qi + 1, lambda kb: bias_ref[qi - kb], None, scratch)
    o_ref[...] = jnp.concatenate([_normalise_pair(acc[0], acc[1]), _normalise_pair(acc[2], acc[3])], axis=1)


def _dilated_bias(tq, s_len):
    nq = s_len // tq
    d = (np.arange(nq)[:, None, None] * tq + np.arange(tq)[None, :, None] - np.arange(tq)[None, None, :])
    count = np.zeros(d.shape, np.float64)
    for window, dil in DIL_PATTERNS:
        count += (d >= 0) & (d <= window) & (d % dil == 0)
    return np.where(count > 0, np.log2(np.maximum(count, 1.0)), NEG_INF).astype(np.float32)


def _dilated_attention(q, k, v_aug):
    b, s_len, w = q.shape
    tq = ATT_TILE
    q_spec, k_spec, v_spec = _att_specs(tq, s_len, w, v_aug.shape[-1])
    bias = jnp.asarray(_dilated_bias(tq, s_len))
    return pl.pallas_call(
        functools.partial(_dil_kernel, tq=tq),
        grid=(b, s_len // tq),
        in_specs=[q_spec, k_spec, v_spec, _const_spec(bias.shape)],
        out_specs=q_spec,
        out_shape=jax.ShapeDtypeStruct((b, s_len, w), F32),
        scratch_shapes=_strip_scratch(N_HEADS, tq, s_len),
        compiler_params=pltpu.CompilerParams(dimension_semantics=("arbitrary", "arbitrary"),
                                             vmem_limit_bytes=BIG_VMEM_LIMIT),
        name="dilated_attention",
    )(q, k, v_aug, bias)


def _s5_kernel(u_ref, bm_ref, cm_ref, are_ref, aim_ref, d_ref, wg_ref, bg_ref, o_ref, st_ref, *xs_refs,
               ts, nb):
    n = S5_GROUPS * S5_STATE
    nc = n // LANES

    @pl.when(pl.program_id(0) == 0)
    def _():
        st_ref[...] = jnp.zeros(st_ref.shape, F32)

    u = u_ref[...].reshape(nb * ts, u_ref.shape[-1])
    bu = _dot(u.astype(BF16), bm_ref[...])
    for c, x_ref in enumerate(xs_refs):
        for b in range(nb):
            x_ref[pl.ds(b, ts, stride=nb), :] = bu[b * ts:(b + 1) * ts, c * LANES:(c + 1) * LANES]
    a_re = jnp.broadcast_to(are_ref[...], (nb, n))
    a_im = jnp.broadcast_to(aim_ref[...], (nb, n))

    def step(t, carry):
        x_re, x_im = carry
        rows = pl.ds(pl.multiple_of(t * nb, nb), nb)
        bu_re = jnp.concatenate([x_ref[rows, :] for x_ref in xs_refs[:nc]], axis=1)
        bu_im = jnp.concatenate([x_ref[rows, :] for x_ref in xs_refs[nc:]], axis=1)
        n_re = a_re * x_re - a_im * x_im + bu_re
        n_im = a_re * x_im + a_im * x_re + bu_im
        for c in range(nc):
            xs_refs[c][rows, :] = n_re[:, c * LANES:(c + 1) * LANES]
            xs_refs[nc + c][rows, :] = n_im[:, c * LANES:(c + 1) * LANES]
        return n_re, n_im

    x_re, x_im = lax.fori_loop(0, ts, step, (st_ref[:, 0:n], st_ref[:, n:2 * n]))
    st_ref[:, 0:n] = x_re
    st_ref[:, n:2 * n] = x_im

    xs = jnp.concatenate(
        [jnp.concatenate([x_ref[pl.ds(b, ts, stride=nb), :].astype(BF16) for x_ref in xs_refs], axis=1)
         for b in range(nb)], axis=0)
    y = _dot(xs, cm_ref[...]) + d_ref[...] * u
    t = _dot(jax.nn.gelu(y).astype(BF16), wg_ref[...]) + bg_ref[...]
    o_ref[...] = (t[:, :BRANCH_WIDTH] * jax.nn.sigmoid(t[:, BRANCH_WIDTH:])).reshape(o_ref.shape)


def _s5_params(lam_re, lam_im, log_dt, b_re, b_im, c_re, c_im):
    g, n, p = S5_GROUPS, S5_STATE, S5_GROUP
    lr, li = lam_re.astype(F32), lam_im.astype(F32)
    dt = jnp.exp(log_dt.astype(F32))[:, None]
    mag = jnp.exp(lr * dt)
    a_re, a_im = mag * jnp.cos(li * dt), mag * jnp.sin(li * dt)
    den = lr * lr + li * li
    n_re, n_im = a_re - 1.0, a_im
    z_re = (n_re * lr + n_im * li) / den
    z_im = (n_im * lr - n_re * li) / den
    br, bi = b_re.astype(F32), b_im.astype(F32)
    bb_re = z_re[..., None] * br - z_im[..., None] * bi
    bb_im = z_re[..., None] * bi + z_im[..., None] * br
    eye = jnp.eye(g, dtype=F32)
    blockdiag_in = lambda t: jnp.einsum("gnp,gh->gphn", t, eye).reshape(g * p, g * n)
    blockdiag_out = lambda t: jnp.einsum("gpn,gh->gnhp", t, eye).reshape(g * n, g * p)
    bm = jnp.concatenate([blockdiag_in(bb_re), blockdiag_in(bb_im)], axis=1)
    cm = jnp.concatenate([blockdiag_out(c_re.astype(F32)), -blockdiag_out(c_im.astype(F32))], axis=0)
    return bm.astype(BF16), cm.astype(BF16), a_re.reshape(1, g * n), a_im.reshape(1, g * n)


def _s5_branch(u, params, d_skip, w_glu, b_glu):
    nb, s_len, w = u.shape
    assert nb == 8
    ts = S5_CHUNK
    bm, cm, a_re, a_im = params
    n2 = bm.shape[1]
    blk = pl.BlockSpec((nb, ts, w), lambda i: (0, i, 0))
    return pl.pallas_call(
        functools.partial(_s5_kernel, ts=ts, nb=nb),
        grid=(s_len // ts,),
        in_specs=[blk, _const_spec(bm.shape), _const_spec(cm.shape), _const_spec(a_re.shape),
                  _const_spec(a_im.shape), _const_spec((1, w)), _const_spec(w_glu.shape),
                  _const_spec((1, 2 * w))],
        out_specs=blk,
        out_shape=jax.ShapeDtypeStruct((nb, s_len, w), F32),
        scratch_shapes=[pltpu.VMEM((nb, n2), F32)] + [pltpu.VMEM((ts * nb, LANES), F32)] * (n2 // LANES),
        compiler_params=pltpu.CompilerParams(dimension_semantics=("arbitrary",),
                                             vmem_limit_bytes=BIG_VMEM_LIMIT),
        name="s5_scan",
    )(u, bm, cm, a_re, a_im, d_skip.astype(F32).reshape(1, w), w_glu.astype(BF16),
      b_glu.astype(F32).reshape(1, 2 * w))


def _compress_kernel(r_ref, pe_top_ref, pe_bot_ref, w_top_ref, w_bot_ref, w2_ref, k_ref, v_ref):
    r = r_ref[...]
    top = (r + pe_top_ref[...]).astype(BF16)
    nxt = pltpu.roll(r, r.shape[0] - 1, 0)
    bot = (nxt + pe_bot_ref[...]).astype(BF16)
    hid = jax.nn.gelu(_dot(top, w_top_ref[...]) + _dot(bot, w_bot_ref[...]))
    k2, v2 = _shared_kv_variants(_dot(hid.astype(BF16), w2_ref[...]))
    k_ref[...] = k2.astype(BF16)
    v_ref[...] = v2.astype(BF16)


def _compress(kvc, pe, w1, w2):
    b, s_len, _ = kvc.shape
    nr = s_len // CMP_STRIDE
    per = CMP_BLOCK // CMP_STRIDE
    assert per == 2
    hid = w1.shape[-1]
    r = kvc.reshape(b, nr, CMP_STRIDE * 2 * HEAD_DIM)
    w1r = w1.astype(F32).reshape(2, per, CMP_STRIDE, HEAD_DIM, hid)
    per_r = pe.astype(F32).reshape(2, per, CMP_STRIDE, HEAD_DIM)

    def expand(j):
        wk = jnp.pad(w1r[0, j], ((0, 0), (0, HEAD_DIM), (0, hid)))
        wv = jnp.pad(w1r[1, j], ((0, 0), (HEAD_DIM, 0), (hid, 0)))
        return (wk + wv).reshape(CMP_STRIDE * 2 * HEAD_DIM, 2 * hid).astype(BF16)

    pe_rows = [jnp.concatenate([per_r[0, j], per_r[1, j]], axis=-1).reshape(1, -1) for j in range(per)]
    z = jnp.zeros((hid, HEAD_DIM), F32)
    w2f = w2.astype(F32)
    w2x = jnp.concatenate([jnp.concatenate([w2f[0], z], axis=1),
                           jnp.concatenate([z, w2f[1]], axis=1)], axis=0).astype(BF16)
    w_top, w_bot = expand(0), expand(1)
    blk = pl.BlockSpec((None, nr, r.shape[-1]), lambda i: (i, 0, 0))
    out = pl.BlockSpec((None, nr, BRANCH_WIDTH), lambda i: (i, 0, 0))
    return pl.pallas_call(
        _compress_kernel,
        grid=(b,),
        in_specs=[blk, _const_spec(pe_rows[0].shape), _const_spec(pe_rows[1].shape),
                  _const_spec(w_top.shape), _const_spec(w_bot.shape), _const_spec(w2x.shape)],
        out_specs=[out, out],
        out_shape=[jax.ShapeDtypeStruct((b, nr, BRANCH_WIDTH), BF16)] * 2,
        compiler_params=pltpu.CompilerParams(dimension_semantics=("arbitrary",)),
        name="nsa_compress",
    )(r, pe_rows[0], pe_rows[1], w_top, w_bot, w2x)


def _nsa_kernel(dq_ref, dqr_ref, dg_ref, kc_ref, vc_ref, ks_ref, vs_ref, kw_ref, vw_ref,
                ovt_ref, exp_ref, gsel_ref, wbias_ref, o_ref, *scratch, tq, n_cmp, n_slc, n_sel):
    qi = pl.program_id(1)
    t0 = qi * tq
    nr = kc_ref.shape[0]
    low = _low_half()
    q_cols = lambda h: slice((h // 2) * LANES, (h // 2 + 1) * LANES)
    kv_col = lambda h: (h % 2) * LANES

    dq = dq_ref[...]
    tpos = t0 + lax.broadcasted_iota(jnp.int32, (tq, nr), 0)
    ci = lax.broadcasted_iota(jnp.int32, (tq, nr), 1)
    cmask = (ci * CMP_STRIDE + (CMP_BLOCK - 1) <= tpos) & (ci < n_cmp)
    p_sum = jnp.zeros((tq, nr), F32)
    o_heads = []
    for h in range(N_HEADS):
        s = jnp.where(cmask, _dot_nt(dq[:, q_cols(h)], kc_ref[:, kv_col(h):kv_col(h) + LANES]), NEG_INF)
        m = jnp.max(s, axis=-1, keepdims=True)
        e = jnp.where(cmask, jnp.exp(s - m), 0.0)
        p = e / jnp.maximum(jnp.sum(e, axis=-1, keepdims=True), 1e-30)
        p_sum = p_sum + p
        o_heads.append(_dot(p.astype(BF16), vc_ref[:, kv_col(h):kv_col(h) + LANES]))
    o_cmp = jnp.concatenate([jnp.where(low, o_heads[0], o_heads[1]), jnp.where(low, o_heads[2], o_heads[3])],
                            axis=1)

    p_hi, p_lo = _split_hi_lo(p_sum)
    ovt = ovt_ref[...]
    imp_t = _dot_nt(ovt, p_hi) + _dot_nt(ovt, p_lo)
    blk = lax.broadcasted_iota(jnp.int32, (n_slc, tq), 0)
    qblk = lax.shift_right_arithmetic(t0 + lax.broadcasted_iota(jnp.int32, (n_slc, tq), 1),
                                      int(math.log2(SLC_BLOCK)))
    forced = jnp.where(blk == 0, 1, jnp.where(blk == qblk, 1, jnp.where(blk == qblk - 1, 1, 0)))
    score = jnp.where(blk <= qblk, jnp.where(forced > 0, FORCE_SCORE, imp_t), NEG_INF)
    rank = jnp.zeros((n_slc, tq), F32)
    for i in range(n_slc):
        si = score[i:i + 1, :]
        tie = jnp.where(blk > i, 1.0, 0.0)
        rank = rank + jnp.where(si > score, 1.0, jnp.where(si == score, tie, 0.0))
    sel_bias_t = jnp.where(rank < n_sel, jnp.where(score > 0.5 * NEG_INF, 0.0, NEG_INF), NEG_INF)
    pad_rows = exp_ref.shape[1] - n_slc
    sel_bias = jnp.concatenate([sel_bias_t, jnp.full((pad_rows, tq), NEG_INF, F32)], axis=0).T.astype(BF16)

    dqr = dqr_ref[...]
    pairs = lambda acc: jnp.concatenate([_normalise_pair(acc[0], acc[1]), _normalise_pair(acc[2], acc[3])], axis=1)
    chains = [(dqr[:, q_cols(h)], (ks_ref, kv_col(h)), (vs_ref, kv_col(h))) for h in range(N_HEADS)]
    o_slc = pairs(_strip_attention(chains, tq, 0, qi + 1, lambda kb: _dot(sel_bias, exp_ref[kb]),
                                   _causal_bias(tq), scratch))
    chains = [(dqr[:, q_cols(h)], (kw_ref, kv_col(h)), (vw_ref, kv_col(h))) for h in range(N_HEADS)]
    win_lo = jnp.maximum(qi - (wbias_ref.shape[0] - 1), 0)
    o_win = pairs(_strip_attention(chains, tq, win_lo, qi + 1, lambda kb: wbias_ref[qi - kb], None, scratch))

    gates = _dot_hilo(jax.nn.sigmoid(dg_ref[...]), gsel_ref[...])
    w = BRANCH_WIDTH
    o_ref[...] = gates[:, :w] * o_cmp + gates[:, w:2 * w] * o_slc + gates[:, 2 * w:] * o_win


def _nsa_constants(tq, s_len):
    n_cmp = (s_len - CMP_BLOCK) // CMP_STRIDE + 1
    n_slc = s_len // SLC_BLOCK
    nr = s_len // CMP_STRIDE
    c0 = np.arange(n_cmp)[:, None] * CMP_STRIDE
    s0 = np.arange(n_slc)[None, :] * SLC_BLOCK
    overlap = np.clip(np.minimum(c0 + CMP_BLOCK, s0 + SLC_BLOCK) - np.maximum(c0, s0), 0, None) / CMP_STRIDE
    ovt = np.zeros((n_slc, nr), np.float32)
    ovt[:, :n_cmp] = overlap.T
    rows = -(-n_slc // 128) * 128
    expand = np.zeros((s_len // tq, rows, tq), np.float32)
    tok = np.arange(s_len)
    expand[tok // tq, tok // SLC_BLOCK, tok % tq] = 1.0
    gsel = np.zeros((128, 3 * BRANCH_WIDTH), np.float32)
    for h in range(N_HEADS):
        for j in range(3):
            gsel[h * 3 + j, j * BRANCH_WIDTH + h * HEAD_DIM:j * BRANCH_WIDTH + (h + 1) * HEAD_DIM] = 1.0
    nwin = -(-WINDOW // tq) + 1
    d = np.arange(nwin)[:, None, None] * tq + np.arange(tq)[None, :, None] - np.arange(tq)[None, None, :]
    wbias = np.where((d >= 0) & (d < WINDOW), 0.0, NEG_INF).astype(np.float32)
    return n_cmp, n_slc, ovt, expand, gsel, wbias


def _nsa_attention(dq, dqr, dg, k_cmp, v_cmp, ks, vs, kw, vw):
    b, s_len, w = dq.shape
    tq = ATT_TILE
    n_cmp, n_slc, ovt, expand, gsel, wbias = _nsa_constants(tq, s_len)
    q_spec, kv_spec, _ = _att_specs(tq, s_len, w, w)
    nr = k_cmp.shape[1]
    cmp_spec = pl.BlockSpec((None, nr, w), lambda bi, i: (bi, 0, 0))
    g_spec = pl.BlockSpec((None, tq, dg.shape[-1]), lambda bi, i: (bi, i, 0))
    consts = [jnp.asarray(ovt, BF16), jnp.asarray(expand, BF16), jnp.asarray(gsel, BF16), jnp.asarray(wbias)]
    return pl.pallas_call(
        functools.partial(_nsa_kernel, tq=tq, n_cmp=n_cmp, n_slc=n_slc, n_sel=min(N_SELECT, n_slc)),
        grid=(b, s_len // tq),
        in_specs=[q_spec, q_spec, g_spec, cmp_spec, cmp_spec, kv_spec, kv_spec, kv_spec, kv_spec]
        + [_const_spec(c.shape) for c in consts],
        out_specs=q_spec,
        out_shape=jax.ShapeDtypeStruct((b, s_len, w), F32),
        scratch_shapes=_strip_scratch(N_HEADS, tq, s_len),
        compiler_params=pltpu.CompilerParams(dimension_semantics=("arbitrary", "arbitrary"),
                                             vmem_limit_bytes=BIG_VMEM_LIMIT),
        name="nsa_attention",
    )(dq, dqr, dg, k_cmp, v_cmp, ks, vs, kw, vw, *consts)


def _mem_kv_kernel(m_ref, g_ref, w_ref, k_ref, v_ref):
    kv = _dot(_rms(m_ref[...], g_ref[...]).astype(BF16), w_ref[...])
    k_ref[...] = kv[:, :BRANCH_WIDTH].astype(BF16)
    v_ref[...] = kv[:, BRANCH_WIDTH:].astype(BF16)


def _mem_kv(mem2, g, w_kv):
    rows, d = mem2.shape
    tm = ROW_TILE
    out = pl.BlockSpec((tm, BRANCH_WIDTH), lambda i: (i, 0))
    return pl.pallas_call(
        _mem_kv_kernel,
        grid=(rows // tm,),
        in_specs=[pl.BlockSpec((tm, d), lambda i: (i, 0)), _const_spec((1, d)), _const_spec(w_kv.shape)],
        out_specs=[out, out],
        out_shape=[jax.ShapeDtypeStruct((rows, BRANCH_WIDTH), BF16)] * 2,
        compiler_params=pltpu.CompilerParams(dimension_semantics=("arbitrary",)),
        name="mem_kv",
    )(mem2, g.astype(F32).reshape(1, d), w_kv.astype(BF16))


def _mem_attn_kernel(q_ref, k_ref, v_ref, o_ref):
    q = q_ref[...]
    k = k_ref[...]
    v = v_ref[...]
    o = jnp.zeros(o_ref.shape, F32)
    for h in range(N_HEADS):
        s = _dot_nt(q * _head_mask(h, BF16), k)
        e = jnp.exp(s - jnp.max(s, axis=-1, keepdims=True))
        oh = _dot(e.astype(BF16), v) / jnp.sum(e, axis=-1, keepdims=True)
        o = jnp.where(_head_mask(h, F32) > 0.5, oh, o)
    o_ref[...] = o


def _mem_attention(q, k, v):
    b, s_len, w = q.shape
    tq = MEM_ATT_TILE
    q_spec = pl.BlockSpec((None, tq, w), lambda bi, i: (bi, i, 0))
    kv_spec = pl.BlockSpec((None, k.shape[1], w), lambda bi, i: (bi, 0, 0))
    return pl.pallas_call(
        _mem_attn_kernel,
        grid=(b, s_len // tq),
        in_specs=[q_spec, kv_spec, kv_spec],
        out_specs=q_spec,
        out_shape=jax.ShapeDtypeStruct((b, s_len, w), F32),
        compiler_params=pltpu.CompilerParams(dimension_semantics=("arbitrary", "arbitrary")),
        name="mem_attention",
    )(q, k, v)


def _merge_kernel(x_ref, g_ref, oa_ref, ob_ref, oc_ref, od_ref, oe_ref, zs_ref, wm_ref, bm_ref, wb_ref,
                  wo_ref, fg_ref, o_ref, *, final_norm):
    x = x_ref[...]
    d = x.shape[-1]
    w = BRANCH_WIDTH
    h = _rms(x, g_ref[...]).astype(BF16)
    mixed = jnp.zeros(x.shape, F32)
    for n, br_ref in enumerate((oa_ref, ob_ref, oc_ref, od_ref, oe_ref)):
        br = (br_ref[...] * zs_ref[:, n * w:(n + 1) * w]).astype(BF16)
        y = _dot(br, wb_ref[n])
        gate = jax.nn.sigmoid(_dot(h, wm_ref[:, n * d:(n + 1) * d]) + bm_ref[:, n * d:(n + 1) * d])
        mixed = mixed + gate * y
    out = x + _dot(mixed.astype(BF16), wo_ref[...])
    if final_norm:
        out = _rms(out, fg_ref[...])
    o_ref[...] = out


def _merge(x2, g, branches, zs, w_merge, b_merge, w_branch, w_out, final_g, final_norm):
    t, d = x2.shape
    tm = MERGE_ROW_TILE
    row = lambda wd: pl.BlockSpec((tm, wd), lambda i: (i, 0))
    return pl.pallas_call(
        functools.partial(_merge_kernel, final_norm=final_norm),
        grid=(t // tm,),
        in_specs=[row(d), _const_spec((1, d))] + [row(BRANCH_WIDTH)] * N_BRANCHES + [row(zs.shape[1])]
        + [_const_spec(w_merge.shape), _const_spec((1, N_BRANCHES * d)), _const_spec(w_branch.shape),
           _const_spec(w_out.shape), _const_spec((1, d))],
        out_specs=row(d),
        out_shape=jax.ShapeDtypeStruct((t, d), F32),
        compiler_params=pltpu.CompilerParams(dimension_semantics=("arbitrary",),
                                             vmem_limit_bytes=BIG_VMEM_LIMIT),
        name="merge",
    )(x2, g.astype(F32).reshape(1, d), *branches, zs, w_merge.astype(BF16),
      b_merge.astype(F32).reshape(1, -1), w_branch.astype(BF16), w_out.astype(BF16),
      final_g.astype(F32).reshape(1, d))


def kernel(x, mem, norm_g, w_in, diff_lambda, diff_subln_g, s5_lambda_re, s5_lambda_im, s5_log_dt,
           s5_b_re, s5_b_im, s5_c_re, s5_c_im, s5_d, w_glu, b_glu, nsa_pe, nsa_w1, nsa_w2, mem_norm_g,
           w_mem_kv, w_merge, b_merge, w_branch, w_out, final_g):
    bsz, s_len, d = x.shape
    depth = w_in.shape[0]
    t = bsz * s_len
    w = BRANCH_WIDTH
    tables = _rope_tables(s_len)
    x2 = x.astype(F32).reshape(t, d)
    mem2 = mem.astype(F32).reshape(-1, d)
    for l in range(depth):
        proj = dict(zip([n for n, _, _ in _IN_OUTS],
                        _in_proj(x2, norm_g[l].astype(F32), _in_weights(w_in[l].astype(F32)), tables, s_len)))
        seq = lambda name: proj[name].reshape(bsz, s_len, -1)

        dl = diff_lambda[l].astype(F32)
        lam_init = 0.8 - 0.6 * math.exp(-0.3 * l)
        lam = jnp.exp(jnp.sum(dl[0] * dl[1])) - jnp.exp(jnp.sum(dl[2] * dl[3])) + lam_init
        o_a = _diff_attention(seq("qa"), seq("ka"), seq("va"), lam, diff_subln_g[l], lam_init)

        o_b = _dilated_attention(seq("qb"), seq("kb"), seq("vb"))

        s5p = _s5_params(s5_lambda_re[l], s5_lambda_im[l], s5_log_dt[l], s5_b_re[l], s5_b_im[l],
                         s5_c_re[l], s5_c_im[l])
        o_c = _s5_branch(seq("cu"), s5p, s5_d[l], w_glu[l], b_glu[l])

        k_cmp, v_cmp = _compress(seq("kvc"), nsa_pe[l], nsa_w1[l], nsa_w2[l])
        o_d = _nsa_attention(seq("dq"), seq("dqr"), seq("dg"), k_cmp, v_cmp,
                             seq("ks"), seq("vs"), seq("kw"), seq("vw"))

        k_mem, v_mem = _mem_kv(mem2, mem_norm_g[l], w_mem_kv[l])
        o_e = _mem_attention(seq("eq"), k_mem.reshape(bsz, -1, w), v_mem.reshape(bsz, -1, w))

        branches = [o.reshape(t, w) for o in (o_a, o_b, o_c, o_d, o_e)]
        x2 = _merge(x2, norm_g[l], branches, proj["zs"], w_merge[l], b_merge[l], w_branch[l], w_out[l],
                    final_g, final_norm=(l == depth - 1))
    return x2.reshape(bsz, s_len, d).astype(x.dtype)
```

```python
import functools
import math

import numpy as np
import jax
import jax.numpy as jnp
from jax import lax
from jax.experimental import pallas as pl
from jax.experimental.pallas import tpu as pltpu

F32 = jnp.float32
BF16 = jnp.bfloat16

HEAD_DIM = 64
BRANCH_WIDTH = 256
N_HEADS = 4
N_BRANCHES = 5
DIFF_QK_DIM = 32
DIL_PATTERNS = ((128, 1), (512, 4), (2048, 16))
S5_GROUP = 16
S5_GROUPS = 16
S5_STATE = 64
CMP_BLOCK = 32
CMP_STRIDE = 16
SLC_BLOCK = 64
N_SELECT = 16
WINDOW = 512
ROPE_THETA = 10000.0
RMS_EPS = 1e-6
NEG_INF = -1e30
FORCE_SCORE = 1e9

V7X_VMEM_BYTES = 64 * 1024 * 1024
BIG_VMEM_LIMIT = V7X_VMEM_BYTES - 8 * 1024 * 1024

LANES = 128
ATT_TILE = 512
ROW_TILE = 512
MERGE_ROW_TILE = 256
MEM_ATT_TILE = 2048
S5_CHUNK = 128

_NT = (((1,), (1,)), ((), ()))


def _rms(x, g):
    return x * lax.rsqrt(jnp.mean(x * x, axis=-1, keepdims=True) + RMS_EPS) * g


def _dot(a, b):
    return jnp.dot(a, b, preferred_element_type=F32)


def _dot_nt(a, b):
    return lax.dot_general(a, b, _NT, preferred_element_type=F32)


def _split_hi_lo(x):
    hi = x.astype(BF16)
    lo = (x - hi.astype(F32)).astype(BF16)
    return hi, lo


def _dot_hilo(x, w):
    hi, lo = _split_hi_lo(x)
    return _dot(hi, w) + _dot(lo, w)


def _const_spec(shape):
    n = len(shape)
    return pl.BlockSpec(shape, lambda *_: (0,) * n, pipeline_mode=pl.Buffered(1))


def _head_mask(h, dtype):
    lane = lax.broadcasted_iota(jnp.int32, (1, BRANCH_WIDTH), 1)
    return jnp.where((lane >= h * HEAD_DIM) & (lane < (h + 1) * HEAD_DIM), 1.0, 0.0).astype(dtype)


def _low_half():
    return lax.broadcasted_iota(jnp.int32, (1, LANES), 1) < HEAD_DIM


def _swap_halves(x):
    return pltpu.roll(x, HEAD_DIM, 1)


def _augment_heads(v):
    low = _low_half()
    parts = []
    for j in range(v.shape[1] // LANES):
        pair = v[:, j * LANES:(j + 1) * LANES]
        parts += [jnp.where(low, pair, 1.0), jnp.where(low, 1.0, pair)]
    return jnp.concatenate(parts, axis=1)


def _shared_kv_variants(kv):
    low = _low_half()
    sw = _swap_halves(kv)
    k2 = jnp.concatenate([jnp.where(low, kv, 0.0), jnp.where(low, 0.0, sw)], axis=1)
    v2 = jnp.concatenate([jnp.where(low, sw, 1.0), jnp.where(low, 1.0, kv)], axis=1)
    return k2, v2


def _normalise_pair(acc_even, acc_odd):
    return jnp.where(_low_half(), acc_even / _swap_halves(acc_even), acc_odd / _swap_halves(acc_odd))


_IN_MAIN_WIDTH = 3200
_IN_SRC = {"qa": (True, 0, 256), "ka": (True, 256, 256), "va": (True, 512, 256), "az": (True, 768, 256),
           "qb": (True, 1024, 256), "kb": (True, 1280, 256), "vb": (True, 1536, 256), "bz": (True, 1792, 256),
           "cu": (True, 2048, 256), "cz": (True, 2304, 256), "dq": (True, 2560, 256),
           "kvc": (True, 2816, 128), "kvs": (True, 2944, 128), "kvw": (True, 3072, 128),
           "dg": (False, 0, 128), "dz": (False, 128, 256), "eq": (False, 384, 256), "ez": (False, 640, 256)}
_IN_TAIL_WIDTH = 896
_IN_SCALED = ("qb", "dq", "eq")
_IN_OUTS = (("qa", 256, BF16), ("ka", 256, BF16), ("va", 512, BF16),
            ("qb", 256, BF16), ("kb", 256, BF16), ("vb", 512, BF16),
            ("cu", 256, F32), ("dq", 256, BF16), ("dqr", 256, BF16), ("kvc", 128, F32),
            ("ks", 256, BF16), ("vs", 256, BF16), ("kw", 256, BF16), ("vw", 256, BF16),
            ("dg", 128, F32), ("eq", 256, BF16), ("zs", 1280, F32))


def _rotate_half(y, group):
    half = group // 2
    lane = lax.broadcasted_iota(jnp.int32, (1, LANES), 1)
    first = (lane & (group - 1)) < half
    parts = []
    for j in range(y.shape[1] // LANES):
        v = y[:, j * LANES:(j + 1) * LANES]
        parts.append(jnp.where(first, pltpu.roll(v, LANES - half, 1), pltpu.roll(v, half, 1)))
    return jnp.concatenate(parts, axis=1)


def _in_proj_kernel(x_ref, g_ref, wm_ref, wt_ref, cosa_ref, sina_ref, cosb_ref, sinb_ref, cosk_ref, sink_ref,
                    *out_refs):
    out = dict(zip([n for n, _, _ in _IN_OUTS], out_refs))
    h = _rms(x_ref[...], g_ref[...]).astype(BF16)

    def proj(name):
        main, off, width = _IN_SRC[name]
        y = _dot(h, (wm_ref if main else wt_ref)[:, off:off + width].astype(BF16))
        return y * HEAD_DIM ** -0.5 if name in _IN_SCALED else y

    def rope(y, cos_ref, sin_ref, group):
        return y * cos_ref[...] + _rotate_half(y, group) * sin_ref[...]

    def put(name, y):
        out[name][...] = y.astype(out[name].dtype)

    put("qa", rope(proj("qa"), cosa_ref, sina_ref, DIFF_QK_DIM))
    put("ka", rope(proj("ka"), cosa_ref, sina_ref, DIFF_QK_DIM))
    put("va", _augment_heads(proj("va")))
    put("qb", rope(proj("qb"), cosb_ref, sinb_ref, HEAD_DIM))
    put("kb", rope(proj("kb"), cosb_ref, sinb_ref, HEAD_DIM))
    put("vb", _augment_heads(proj("vb")))
    put("cu", proj("cu"))
    dq = proj("dq")
    put("dq", dq)
    put("dqr", rope(dq, cosb_ref, sinb_ref, HEAD_DIM))
    put("kvc", proj("kvc"))
    for kv_name, k_name, v_name in (("kvs", "ks", "vs"), ("kvw", "kw", "vw")):
        k2, v2 = _shared_kv_variants(rope(proj(kv_name), cosk_ref, sink_ref, HEAD_DIM))
        put(k_name, k2)
        put(v_name, v2)
    put("dg", proj("dg"))
    put("eq", proj("eq"))
    for n, name in enumerate(("az", "bz", "cz", "dz", "ez")):
        z = proj(name)
        out["zs"][:, n * BRANCH_WIDTH:(n + 1) * BRANCH_WIDTH] = z * jax.nn.sigmoid(z)


def _in_split(w_in):
    n_gate = 3 * N_HEADS
    tail = w_in[:, :, _IN_MAIN_WIDTH:]
    w_tail = jnp.concatenate([jnp.pad(tail[:, :, :n_gate], ((0, 0), (0, 0), (0, LANES - n_gate))),
                              tail[:, :, n_gate:]], axis=2)
    return w_in[:, :, :_IN_MAIN_WIDTH], w_tail


def _rope_tables(s_len):
    def table(group, width):
        half = group // 2
        inv_freq = ROPE_THETA ** (-jnp.arange(half, dtype=F32) / half)
        ang = jnp.arange(s_len, dtype=F32)[:, None] * inv_freq[None, :]
        cos = jnp.tile(jnp.cos(ang), (1, width // half))
        sin = jnp.tile(jnp.concatenate([-jnp.sin(ang), jnp.sin(ang)], axis=1), (1, width // group))
        return cos, sin
    cos_a, sin_a = table(DIFF_QK_DIM, BRANCH_WIDTH)
    cos_b, sin_b = table(HEAD_DIM, BRANCH_WIDTH)
    cos_k = jnp.concatenate([cos_b[:, :HEAD_DIM], jnp.ones((s_len, HEAD_DIM), F32)], axis=1)
    sin_k = jnp.concatenate([sin_b[:, :HEAD_DIM], jnp.zeros((s_len, HEAD_DIM), F32)], axis=1)
    return cos_a, sin_a, cos_b, sin_b, cos_k, sin_k


def _in_proj(x2, g, w_main, w_tail, layer, tables, s_len):
    t, d = x2.shape
    tm = ROW_TILE
    nsb = s_len // tm
    assert w_main.shape[2] == _IN_MAIN_WIDTH and w_tail.shape[2] == _IN_TAIL_WIDTH
    row = lambda w: pl.BlockSpec((tm, w), lambda i: (i, 0))
    tab = lambda a: pl.BlockSpec((tm, a.shape[1]), lambda i: (i % nsb, 0))
    of_layer = lambda a: pl.BlockSpec((None,) + a.shape[1:], lambda i: (layer, 0, 0), pipeline_mode=pl.Buffered(1))
    return pl.pallas_call(
        _in_proj_kernel,
        grid=(t // tm,),
        in_specs=[row(d), _const_spec((1, d)), of_layer(w_main), of_layer(w_tail)] + [tab(a) for a in tables],
        out_specs=[row(w) for _, w, _ in _IN_OUTS],
        out_shape=[jax.ShapeDtypeStruct((t, w), dt) for _, w, dt in _IN_OUTS],
        compiler_params=pltpu.CompilerParams(dimension_semantics=("arbitrary",),
                                             vmem_limit_bytes=BIG_VMEM_LIMIT),
        name="in_proj",
    )(x2, g.reshape(1, d), w_main, w_tail, *tables)


def _lane_fold(x, op):
    parts = [x[:, j * LANES:(j + 1) * LANES] for j in range(x.shape[1] // LANES)]
    return functools.reduce(op, parts)


def _strip_attention(chains, tq, kb_lo, kb_hi, bias_fn, last_bias, scratch, exp_scale=1.0):
    s_ref, m_ref, acc_ref = scratch
    n = len(chains)
    reps = tq // LANES
    c1 = exp_scale * math.log2(math.e)

    def tile(cache, ref, col, width, kb, ntiles=1):
        key = (id(ref), col, width)
        if key not in cache:
            cache[key] = ref[pl.ds(pl.multiple_of(kb * tq, tq), ntiles * tq), col:col + width]
        return cache[key]

    def scores(kb, extra):
        bias = bias_fn(kb) if bias_fn is not None else None
        if extra is not None:
            bias = extra if bias is None else bias + extra
        cache, out = {}, []
        for q, (k_ref, k_col), _ in chains:
            s = _dot_nt(q, tile(cache, k_ref, k_col, q.shape[1], kb)) * c1
            out.append(s if bias is None else s + bias)
        return out

    def tile_pairs(lo, hi, step):
        def two(j, carry):
            step(lo + 2 * j, 2)
            return carry
        cnt = hi - lo
        lax.fori_loop(0, lax.shift_right_arithmetic(cnt, 1), two, 0)

        @pl.when((cnt & 1) == 1)
        def _():
            step(hi - 1, 1)

    m_ref[0:n] = jnp.full((n,) + m_ref.shape[1:], NEG_INF, F32)

    def pass1(kb, ntiles):
        tiles = [scores(kb + j, None) for j in range(ntiles)]
        for i in range(n):
            m = m_ref[i]
            for j in range(ntiles):
                s_ref[i, kb + j] = tiles[j][i]
                m = jnp.maximum(m, _lane_fold(tiles[j][i], jnp.maximum))
            m_ref[i] = m

    tile_pairs(kb_lo, kb_hi - 1, pass1)
    for i, s in enumerate(scores(kb_hi - 1, last_bias)):
        s_ref[i, kb_hi - 1] = s
        m = jnp.max(jnp.maximum(m_ref[i], _lane_fold(s, jnp.maximum)), axis=-1, keepdims=True)
        m_ref[i] = jnp.broadcast_to(m, m_ref.shape[1:])
        acc_ref[i] = jnp.zeros(acc_ref.shape[1:], F32)

    def pass2(kb, ntiles):
        cache = {}
        for i, (_, _, (v_ref, v_col)) in enumerate(chains):
            m = jnp.concatenate([m_ref[i]] * reps, axis=1)
            p = [jnp.exp2(s_ref[i, kb + j] - m).astype(BF16) for j in range(ntiles)]
            p = p[0] if ntiles == 1 else jnp.concatenate(p, axis=1)
            acc_ref[i] += _dot(p, tile(cache, v_ref, v_col, LANES, kb, ntiles))

    tile_pairs(kb_lo, kb_hi, pass2)
    return [acc_ref[i] for i in range(n)]


def _strip_scratch(n, tq, s_len):
    return [pltpu.VMEM((n, s_len // tq, tq, tq), F32), pltpu.VMEM((n, tq, LANES), F32),
            pltpu.VMEM((n, tq, LANES), F32)]


def _causal_bias(tq):
    r = lax.broadcasted_iota(jnp.int32, (tq, tq), 0)
    c = lax.broadcasted_iota(jnp.int32, (tq, tq), 1)
    return jnp.where(c <= r, 0.0, NEG_INF).astype(F32)


def _att_specs(tq, s_len, k_width, v_width):
    q_spec = pl.BlockSpec((None, tq, BRANCH_WIDTH), lambda b, i: (b, i, 0))
    kv = lambda w: pl.BlockSpec((None, s_len, w), lambda b, i: (b, 0, 0))
    return q_spec, kv(k_width), kv(v_width)


def _diff_kernel(lam_ref, q_ref, k_ref, v_ref, g_ref, hm_ref, o_ref, *scratch, tq, out_scale):
    qi = pl.program_id(1)
    q = q_ref[...]
    lane = lax.broadcasted_iota(jnp.int32, (1, BRANCH_WIDTH), 1)
    lam = lam_ref[0]
    chains = []
    for hc in range(2 * N_HEADS):
        lo = hc * DIFF_QK_DIM
        cmask = jnp.where((lane >= lo) & (lane < lo + DIFF_QK_DIM), 1.0, 0.0).astype(BF16)
        chains.append((q * cmask, (k_ref, 0), (v_ref, (hc // 2) * LANES)))
    acc = _strip_attention(chains, tq, 0, qi + 1, None, _causal_bias(tq), scratch,
                           exp_scale=DIFF_QK_DIM ** -0.5)
    halves = []
    for pair in range(N_HEADS // 2):
        even, odd = 4 * pair, 4 * pair + 2
        halves.append(_normalise_pair(acc[even], acc[odd]) - lam * _normalise_pair(acc[even + 1], acc[odd + 1]))
    o = jnp.concatenate(halves, axis=1)
    ms = _dot_hilo(o * o, hm_ref[...])
    o_ref[...] = o * lax.rsqrt(ms + RMS_EPS) * g_ref[...] * out_scale


def _diff_attention(q, k, v_aug, lam, subln_g, lam_init):
    b, s_len, w = q.shape
    tq = ATT_TILE
    q_spec, k_spec, v_spec = _att_specs(tq, s_len, w, v_aug.shape[-1])
    head = np.arange(w) // HEAD_DIM
    hm = jnp.asarray((head[:, None] == head[None, :]) / HEAD_DIM, dtype=BF16)
    g = jnp.tile(subln_g.astype(F32), N_HEADS).reshape(1, w)
    return pl.pallas_call(
        functools.partial(_diff_kernel, tq=tq, out_scale=1.0 - lam_init),
        grid=(b, s_len // tq),
        in_specs=[pl.BlockSpec(memory_space=pltpu.SMEM), q_spec, k_spec, v_spec,
                  _const_spec((1, w)), _const_spec((w, w))],
        out_specs=q_spec,
        out_shape=jax.ShapeDtypeStruct((b, s_len, w), F32),
        scratch_shapes=_strip_scratch(2 * N_HEADS, tq, s_len),
        compiler_params=pltpu.CompilerParams(dimension_semantics=("arbitrary", "arbitrary"),
                                             vmem_limit_bytes=BIG_VMEM_LIMIT),
        name="diff_attention",
    )(lam.reshape(1), q, k, v_aug, g, hm)


def _dil_kernel(q_ref, k_ref, v_ref, bias_ref, o_ref, *scratch, tq):
    qi = pl.program_id(1)
    q = q_ref[...]
    chains = [(q * _head_mask(h, BF16), (k_ref, 0), (v_ref, h * LANES)) for h in range(N_HEADS)]
    acc = _strip_attention(chains, tq, 0, qi + 1, lambda kb: bias_ref[qi - kb], None, scratch)
    o_ref[...] = jnp.concatenate([_normalise_pair(acc[0], acc[1]), _normalise_pair(acc[2], acc[3])], axis=1)


def _dilated_bias(tq, s_len):
    nq = s_len // tq
    d = (np.arange(nq)[:, None, None] * tq + np.arange(tq)[None, :, None] - np.arange(tq)[None, None, :])
    count = np.zeros(d.shape, np.float64)
    for window, dil in DIL_PATTERNS:
        count += (d >= 0) & (d <= window) & (d % dil == 0)
    return np.where(count > 0, np.log2(np.maximum(count, 1.0)), NEG_INF).astype(np.float32)


def _dilated_attention(q, k, v_aug):
    b, s_len, w = q.shape
    tq = ATT_TILE
    q_spec, k_spec, v_spec = _att_specs(tq, s_len, w, v_aug.shape[-1])
    bias = jnp.asarray(_dilated_bias(tq, s_len))
    return pl.pallas_call(
        functools.partial(_dil_kernel, tq=tq),
        grid=(b, s_len // tq),
        in_specs=[q_spec, k_spec, v_spec, _const_spec(bias.shape)],
        out_specs=q_spec,
        out_shape=jax.ShapeDtypeStruct((b, s_len, w), F32),
        scratch_shapes=_strip_scratch(N_HEADS, tq, s_len),
        compiler_params=pltpu.CompilerParams(dimension_semantics=("arbitrary", "arbitrary"),
                                             vmem_limit_bytes=BIG_VMEM_LIMIT),
        name="dilated_attention",
    )(q, k, v_aug, bias)


def _s5_kernel(u_ref, bm_ref, cm_ref, are_ref, aim_ref, d_ref, wg_ref, bg_ref, o_ref, st_ref, *xs_refs,
               ts, nb):
    n = S5_GROUPS * S5_STATE
    nc = n // LANES

    @pl.when(pl.program_id(0) == 0)
    def _():
        st_ref[...] = jnp.zeros(st_ref.shape, F32)

    u = u_ref[...].reshape(nb * ts, u_ref.shape[-1])
    bu = _dot(u.astype(BF16), bm_ref[...])
    for c, x_ref in enumerate(xs_refs):
        for b in range(nb):
            x_ref[pl.ds(b, ts, stride=nb), :] = bu[b * ts:(b + 1) * ts, c * LANES:(c + 1) * LANES]
    a_re = jnp.broadcast_to(are_ref[...], (nb, n))
    a_im = jnp.broadcast_to(aim_ref[...], (nb, n))

    def step(t, carry):
        x_re, x_im = carry
        rows = pl.ds(pl.multiple_of(t * nb, nb), nb)
        bu_re = jnp.concatenate([x_ref[rows, :] for x_ref in xs_refs[:nc]], axis=1)
        bu_im = jnp.concatenate([x_ref[rows, :] for x_ref in xs_refs[nc:]], axis=1)
        n_re = a_re * x_re - a_im * x_im + bu_re
        n_im = a_re * x_im + a_im * x_re + bu_im
        for c in range(nc):
            xs_refs[c][rows, :] = n_re[:, c * LANES:(c + 1) * LANES]
            xs_refs[nc + c][rows, :] = n_im[:, c * LANES:(c + 1) * LANES]
        return n_re, n_im

    x_re, x_im = lax.fori_loop(0, ts, step, (st_ref[:, 0:n], st_ref[:, n:2 * n]))
    st_ref[:, 0:n] = x_re
    st_ref[:, n:2 * n] = x_im

    xs = jnp.concatenate(
        [jnp.concatenate([x_ref[pl.ds(b, ts, stride=nb), :].astype(BF16) for x_ref in xs_refs], axis=1)
         for b in range(nb)], axis=0)
    y = _dot(xs, cm_ref[...]) + d_ref[...] * u
    t = _dot(jax.nn.gelu(y).astype(BF16), wg_ref[...]) + bg_ref[...]
    o_ref[...] = (t[:, :BRANCH_WIDTH] * jax.nn.sigmoid(t[:, BRANCH_WIDTH:])).reshape(o_ref.shape)


def _s5_params(lam_re, lam_im, log_dt, b_re, b_im, c_re, c_im):
    g, n, p = S5_GROUPS, S5_STATE, S5_GROUP
    lr, li = lam_re.astype(F32), lam_im.astype(F32)
    dt = jnp.exp(log_dt.astype(F32))[:, None]
    mag = jnp.exp(lr * dt)
    a_re, a_im = mag * jnp.cos(li * dt), mag * jnp.sin(li * dt)
    den = lr * lr + li * li
    n_re, n_im = a_re - 1.0, a_im
    z_re = (n_re * lr + n_im * li) / den
    z_im = (n_im * lr - n_re * li) / den
    br, bi = b_re.astype(F32), b_im.astype(F32)
    bb_re = z_re[..., None] * br - z_im[..., None] * bi
    bb_im = z_re[..., None] * bi + z_im[..., None] * br
    eye = jnp.eye(g, dtype=F32)
    blockdiag_in = lambda t: jnp.einsum("gnp,gh->gphn", t, eye).reshape(g * p, g * n)
    blockdiag_out = lambda t: jnp.einsum("gpn,gh->gnhp", t, eye).reshape(g * n, g * p)
    bm = jnp.concatenate([blockdiag_in(bb_re), blockdiag_in(bb_im)], axis=1)
    cm = jnp.concatenate([blockdiag_out(c_re.astype(F32)), -blockdiag_out(c_im.astype(F32))], axis=0)
    return bm.astype(BF16), cm.astype(BF16), a_re.reshape(1, g * n), a_im.reshape(1, g * n)


def _s5_branch(u, params, d_skip, w_glu, b_glu):
    nb, s_len, w = u.shape
    assert nb == 8
    ts = S5_CHUNK
    bm, cm, a_re, a_im = params
    n2 = bm.shape[1]
    blk = pl.BlockSpec((nb, ts, w), lambda i: (0, i, 0))
    return pl.pallas_call(
        functools.partial(_s5_kernel, ts=ts, nb=nb),
        grid=(s_len // ts,),
        in_specs=[blk, _const_spec(bm.shape), _const_spec(cm.shape), _const_spec(a_re.shape),
                  _const_spec(a_im.shape), _const_spec((1, w)), _const_spec(w_glu.shape),
                  _const_spec((1, 2 * w))],
        out_specs=blk,
        out_shape=jax.ShapeDtypeStruct((nb, s_len, w), F32),
        scratch_shapes=[pltpu.VMEM((nb, n2), F32)] + [pltpu.VMEM((ts * nb, LANES), F32)] * (n2 // LANES),
        compiler_params=pltpu.CompilerParams(dimension_semantics=("arbitrary",),
                                             vmem_limit_bytes=BIG_VMEM_LIMIT),
        name="s5_scan",
    )(u, bm, cm, a_re, a_im, d_skip.astype(F32).reshape(1, w), w_glu.astype(BF16),
      b_glu.astype(F32).reshape(1, 2 * w))


def _compress_kernel(r_ref, pe_top_ref, pe_bot_ref, w_top_ref, w_bot_ref, w2_ref, k_ref, v_ref):
    r = r_ref[...]
    top = (r + pe_top_ref[...]).astype(BF16)
    nxt = pltpu.roll(r, r.shape[0] - 1, 0)
    bot = (nxt + pe_bot_ref[...]).astype(BF16)
    hid = jax.nn.gelu(_dot(top, w_top_ref[...]) + _dot(bot, w_bot_ref[...]))
    k2, v2 = _shared_kv_variants(_dot(hid.astype(BF16), w2_ref[...]))
    k_ref[...] = k2.astype(BF16)
    v_ref[...] = v2.astype(BF16)


def _compress(kvc, pe, w1, w2):
    b, s_len, _ = kvc.shape
    nr = s_len // CMP_STRIDE
    per = CMP_BLOCK // CMP_STRIDE
    assert per == 2
    hid = w1.shape[-1]
    r = kvc.reshape(b, nr, CMP_STRIDE * 2 * HEAD_DIM)
    w1r = w1.astype(F32).reshape(2, per, CMP_STRIDE, HEAD_DIM, hid)
    per_r = pe.astype(F32).reshape(2, per, CMP_STRIDE, HEAD_DIM)

    def expand(j):
        wk = jnp.pad(w1r[0, j], ((0, 0), (0, HEAD_DIM), (0, hid)))
        wv = jnp.pad(w1r[1, j], ((0, 0), (HEAD_DIM, 0), (hid, 0)))
        return (wk + wv).reshape(CMP_STRIDE * 2 * HEAD_DIM, 2 * hid).astype(BF16)

    pe_rows = [jnp.concatenate([per_r[0, j], per_r[1, j]], axis=-1).reshape(1, -1) for j in range(per)]
    z = jnp.zeros((hid, HEAD_DIM), F32)
    w2f = w2.astype(F32)
    w2x = jnp.concatenate([jnp.concatenate([w2f[0], z], axis=1),
                           jnp.concatenate([z, w2f[1]], axis=1)], axis=0).astype(BF16)
    w_top, w_bot = expand(0), expand(1)
    blk = pl.BlockSpec((None, nr, r.shape[-1]), lambda i: (i, 0, 0))
    out = pl.BlockSpec((None, nr, BRANCH_WIDTH), lambda i: (i, 0, 0))
    return pl.pallas_call(
        _compress_kernel,
        grid=(b,),
        in_specs=[blk, _const_spec(pe_rows[0].shape), _const_spec(pe_rows[1].shape),
                  _const_spec(w_top.shape), _const_spec(w_bot.shape), _const_spec(w2x.shape)],
        out_specs=[out, out],
        out_shape=[jax.ShapeDtypeStruct((b, nr, BRANCH_WIDTH), BF16)] * 2,
        compiler_params=pltpu.CompilerParams(dimension_semantics=("arbitrary",)),
        name="nsa_compress",
    )(r, pe_rows[0], pe_rows[1], w_top, w_bot, w2x)


def _nsa_kernel(dq_ref, dqr_ref, dg_ref, kc_ref, vc_ref, ks_ref, vs_ref, kw_ref, vw_ref,
                ovt_ref, exp_ref, gsel_ref, wbias_ref, o_ref, *scratch, tq, n_cmp, n_slc, n_sel):
    qi = pl.program_id(1)
    t0 = qi * tq
    nr = kc_ref.shape[0]
    low = _low_half()
    q_cols = lambda h: slice((h // 2) * LANES, (h // 2 + 1) * LANES)
    kv_col = lambda h: (h % 2) * LANES

    dq = dq_ref[...]
    tpos = t0 + lax.broadcasted_iota(jnp.int32, (tq, nr), 0)
    ci = lax.broadcasted_iota(jnp.int32, (tq, nr), 1)
    cmask = (ci * CMP_STRIDE + (CMP_BLOCK - 1) <= tpos) & (ci < n_cmp)
    p_sum = jnp.zeros((tq, nr), F32)
    o_heads = []
    for h in range(N_HEADS):
        s = jnp.where(cmask, _dot_nt(dq[:, q_cols(h)], kc_ref[:, kv_col(h):kv_col(h) + LANES]), NEG_INF)
        m = jnp.max(s, axis=-1, keepdims=True)
        e = jnp.where(cmask, jnp.exp(s - m), 0.0)
        p = e / jnp.maximum(jnp.sum(e, axis=-1, keepdims=True), 1e-30)
        p_sum = p_sum + p
        o_heads.append(_dot(p.astype(BF16), vc_ref[:, kv_col(h):kv_col(h) + LANES]))
    o_cmp = jnp.concatenate([jnp.where(low, o_heads[0], o_heads[1]), jnp.where(low, o_heads[2], o_heads[3])],
                            axis=1)

    p_hi, p_lo = _split_hi_lo(p_sum)
    ovt = ovt_ref[...]
    imp_t = _dot_nt(ovt, p_hi) + _dot_nt(ovt, p_lo)
    blk = lax.broadcasted_iota(jnp.int32, (n_slc, tq), 0)
    qblk = lax.shift_right_arithmetic(t0 + lax.broadcasted_iota(jnp.int32, (n_slc, tq), 1),
                                      int(math.log2(SLC_BLOCK)))
    forced = jnp.where(blk == 0, 1, jnp.where(blk == qblk, 1, jnp.where(blk == qblk - 1, 1, 0)))
    score = jnp.where(blk <= qblk, jnp.where(forced > 0, FORCE_SCORE, imp_t), NEG_INF)
    rank = jnp.zeros((n_slc, tq), F32)
    for i in range(n_slc):
        si = score[i:i + 1, :]
        tie = jnp.where(blk > i, 1.0, 0.0)
        rank = rank + jnp.where(si > score, 1.0, jnp.where(si == score, tie, 0.0))
    sel_bias_t = jnp.where(rank < n_sel, jnp.where(score > 0.5 * NEG_INF, 0.0, NEG_INF), NEG_INF)
    pad_rows = exp_ref.shape[1] - n_slc
    sel_bias = jnp.concatenate([sel_bias_t, jnp.full((pad_rows, tq), NEG_INF, F32)], axis=0).T.astype(BF16)

    dqr = dqr_ref[...]
    pairs = lambda acc: jnp.concatenate([_normalise_pair(acc[0], acc[1]), _normalise_pair(acc[2], acc[3])], axis=1)
    chains = [(dqr[:, q_cols(h)], (ks_ref, kv_col(h)), (vs_ref, kv_col(h))) for h in range(N_HEADS)]
    o_slc = pairs(_strip_attention(chains, tq, 0, qi + 1, lambda kb: _dot(sel_bias, exp_ref[kb]),
                                   _causal_bias(tq), scratch))
    chains = [(dqr[:, q_cols(h)], (kw_ref, kv_col(h)), (vw_ref, kv_col(h))) for h in range(N_HEADS)]
    win_lo = jnp.maximum(qi - (wbias_ref.shape[0] - 1), 0)
    o_win = pairs(_strip_attention(chains, tq, win_lo, qi + 1, lambda kb: wbias_ref[qi - kb], None, scratch))

    gates = _dot_hilo(jax.nn.sigmoid(dg_ref[...]), gsel_ref[...])
    w = BRANCH_WIDTH
    o_ref[...] = gates[:, :w] * o_cmp + gates[:, w:2 * w] * o_slc + gates[:, 2 * w:] * o_win


def _nsa_constants(tq, s_len):
    n_cmp = (s_len - CMP_BLOCK) // CMP_STRIDE + 1
    n_slc = s_len // SLC_BLOCK
    nr = s_len // CMP_STRIDE
    c0 = np.arange(n_cmp)[:, None] * CMP_STRIDE
    s0 = np.arange(n_slc)[None, :] * SLC_BLOCK
    overlap = np.clip(np.minimum(c0 + CMP_BLOCK, s0 + SLC_BLOCK) - np.maximum(c0, s0), 0, None) / CMP_STRIDE
    ovt = np.zeros((n_slc, nr), np.float32)
    ovt[:, :n_cmp] = overlap.T
    rows = -(-n_slc // 128) * 128
    expand = np.zeros((s_len // tq, rows, tq), np.float32)
    tok = np.arange(s_len)
    expand[tok // tq, tok // SLC_BLOCK, tok % tq] = 1.0
    gsel = np.zeros((128, 3 * BRANCH_WIDTH), np.float32)
    for h in range(N_HEADS):
        for j in range(3):
            gsel[h * 3 + j, j * BRANCH_WIDTH + h * HEAD_DIM:j * BRANCH_WIDTH + (h + 1) * HEAD_DIM] = 1.0
    nwin = -(-WINDOW // tq) + 1
    d = np.arange(nwin)[:, None, None] * tq + np.arange(tq)[None, :, None] - np.arange(tq)[None, None, :]
    wbias = np.where((d >= 0) & (d < WINDOW), 0.0, NEG_INF).astype(np.float32)
    return n_cmp, n_slc, ovt, expand, gsel, wbias


def _nsa_attention(dq, dqr, dg, k_cmp, v_cmp, ks, vs, kw, vw):
    b, s_len, w = dq.shape
    tq = ATT_TILE
    n_cmp, n_slc, ovt, expand, gsel, wbias = _nsa_constants(tq, s_len)
    q_spec, kv_spec, _ = _att_specs(tq, s_len, w, w)
    nr = k_cmp.shape[1]
    cmp_spec = pl.BlockSpec((None, nr, w), lambda bi, i: (bi, 0, 0))
    g_spec = pl.BlockSpec((None, tq, dg.shape[-1]), lambda bi, i: (bi, i, 0))
    consts = [jnp.asarray(ovt, BF16), jnp.asarray(expand, BF16), jnp.asarray(gsel, BF16), jnp.asarray(wbias)]
    return pl.pallas_call(
        functools.partial(_nsa_kernel, tq=tq, n_cmp=n_cmp, n_slc=n_slc, n_sel=min(N_SELECT, n_slc)),
        grid=(b, s_len // tq),
        in_specs=[q_spec, q_spec, g_spec, cmp_spec, cmp_spec, kv_spec, kv_spec, kv_spec, kv_spec]
        + [_const_spec(c.shape) for c in consts],
        out_specs=q_spec,
        out_shape=jax.ShapeDtypeStruct((b, s_len, w), F32),
        scratch_shapes=_strip_scratch(N_HEADS, tq, s_len),
        compiler_params=pltpu.CompilerParams(dimension_semantics=("arbitrary", "arbitrary"),
                                             vmem_limit_bytes=BIG_VMEM_LIMIT),
        name="nsa_attention",
    )(dq, dqr, dg, k_cmp, v_cmp, ks, vs, kw, vw, *consts)


def _mem_kv_kernel(m_ref, g_ref, w_ref, k_ref, v_ref):
    kv = _dot(_rms(m_ref[...], g_ref[...]).astype(BF16), w_ref[...])
    k_ref[...] = kv[:, :BRANCH_WIDTH].astype(BF16)
    v_ref[...] = kv[:, BRANCH_WIDTH:].astype(BF16)


def _mem_kv(mem2, g, w_kv):
    rows, d = mem2.shape
    tm = ROW_TILE
    out = pl.BlockSpec((tm, BRANCH_WIDTH), lambda i: (i, 0))
    return pl.pallas_call(
        _mem_kv_kernel,
        grid=(rows // tm,),
        in_specs=[pl.BlockSpec((tm, d), lambda i: (i, 0)), _const_spec((1, d)), _const_spec(w_kv.shape)],
        out_specs=[out, out],
        out_shape=[jax.ShapeDtypeStruct((rows, BRANCH_WIDTH), BF16)] * 2,
        compiler_params=pltpu.CompilerParams(dimension_semantics=("arbitrary",)),
        name="mem_kv",
    )(mem2, g.astype(F32).reshape(1, d), w_kv.astype(BF16))


def _mem_attn_kernel(q_ref, k_ref, v_ref, o_ref):
    q = q_ref[...]
    k = k_ref[...]
    v = v_ref[...]
    o = jnp.zeros(o_ref.shape, F32)
    for h in range(N_HEADS):
        s = _dot_nt(q * _head_mask(h, BF16), k)
        e = jnp.exp(s - jnp.max(s, axis=-1, keepdims=True))
        oh = _dot(e.astype(BF16), v) / jnp.sum(e, axis=-1, keepdims=True)
        o = jnp.where(_head_mask(h, F32) > 0.5, oh, o)
    o_ref[...] = o


def _mem_attention(q, k, v):
    b, s_len, w = q.shape
    tq = min(MEM_ATT_TILE, s_len)
    q_spec = pl.BlockSpec((None, tq, w), lambda bi, i: (bi, i, 0))
    kv_spec = pl.BlockSpec((None, k.shape[1], w), lambda bi, i: (bi, 0, 0))
    return pl.pallas_call(
        _mem_attn_kernel,
        grid=(b, s_len // tq),
        in_specs=[q_spec, kv_spec, kv_spec],
        out_specs=q_spec,
        out_shape=jax.ShapeDtypeStruct((b, s_len, w), F32),
        compiler_params=pltpu.CompilerParams(dimension_semantics=("arbitrary", "arbitrary")),
        name="mem_attention",
    )(q, k, v)


def _merge_kernel(x_ref, g_ref, oa_ref, ob_ref, oc_ref, od_ref, oe_ref, zs_ref, wm_ref, bm_ref, wb_ref,
                  wo_ref, fg_ref, o_ref, *, final_norm):
    x = x_ref[...]
    d = x.shape[-1]
    w = BRANCH_WIDTH
    h = _rms(x, g_ref[...]).astype(BF16)
    mixed = jnp.zeros(x.shape, F32)
    for n, br_ref in enumerate((oa_ref, ob_ref, oc_ref, od_ref, oe_ref)):
        br = (br_ref[...] * zs_ref[:, n * w:(n + 1) * w]).astype(BF16)
        y = _dot(br, wb_ref[n].astype(BF16))
        gate = jax.nn.sigmoid(_dot(h, wm_ref[:, n * d:(n + 1) * d].astype(BF16)) + bm_ref[:, n * d:(n + 1) * d])
        mixed = mixed + gate * y
    out = x + _dot(mixed.astype(BF16), wo_ref[...].astype(BF16))
    if final_norm:
        out = _rms(out, fg_ref[...])
    o_ref[...] = out


def _merge(x2, g, branches, zs, w_merge, b_merge, w_branch, w_out, layer, final_g, final_norm):
    t, d = x2.shape
    tm = MERGE_ROW_TILE
    row = lambda wd: pl.BlockSpec((tm, wd), lambda i: (i, 0))

    def of_layer(a):
        nd = a.ndim - 1
        return pl.BlockSpec((None,) + a.shape[1:], lambda i: (layer,) + (0,) * nd, pipeline_mode=pl.Buffered(1))

    b_merge = b_merge.astype(F32).reshape(b_merge.shape[0], 1, -1)
    return pl.pallas_call(
        functools.partial(_merge_kernel, final_norm=final_norm),
        grid=(t // tm,),
        in_specs=[row(d), _const_spec((1, d))] + [row(BRANCH_WIDTH)] * N_BRANCHES + [row(zs.shape[1])]
        + [of_layer(w_merge), of_layer(b_merge), of_layer(w_branch), of_layer(w_out), _const_spec((1, d))],
        out_specs=row(d),
        out_shape=jax.ShapeDtypeStruct((t, d), F32),
        compiler_params=pltpu.CompilerParams(dimension_semantics=("arbitrary",),
                                             vmem_limit_bytes=BIG_VMEM_LIMIT),
        name="merge",
    )(x2, g.astype(F32).reshape(1, d), *branches, zs, w_merge.astype(F32), b_merge, w_branch.astype(F32),
      w_out.astype(F32), final_g.astype(F32).reshape(1, d))


def kernel(x, mem, norm_g, w_in, diff_lambda, diff_subln_g, s5_lambda_re, s5_lambda_im, s5_log_dt,
           s5_b_re, s5_b_im, s5_c_re, s5_c_im, s5_d, w_glu, b_glu, nsa_pe, nsa_w1, nsa_w2, mem_norm_g,
           w_mem_kv, w_merge, b_merge, w_branch, w_out, final_g):
    bsz, s_len, d = x.shape
    depth = w_in.shape[0]
    t = bsz * s_len
    w = BRANCH_WIDTH
    tables = _rope_tables(s_len)
    x2 = x.astype(F32).reshape(t, d)
    mem2 = mem.astype(F32).reshape(-1, d)
    w_main, w_tail = _in_split(w_in.astype(F32))
    for l in range(depth):
        proj = dict(zip([n for n, _, _ in _IN_OUTS],
                        _in_proj(x2, norm_g[l].astype(F32), w_main, w_tail, l, tables, s_len)))
        seq = lambda name: proj[name].reshape(bsz, s_len, -1)

        dl = diff_lambda[l].astype(F32)
        lam_init = 0.8 - 0.6 * math.exp(-0.3 * l)
        lam = jnp.exp(jnp.sum(dl[0] * dl[1])) - jnp.exp(jnp.sum(dl[2] * dl[3])) + lam_init
        o_a = _diff_attention(seq("qa"), seq("ka"), seq("va"), lam, diff_subln_g[l], lam_init)

        o_b = _dilated_attention(seq("qb"), seq("kb"), seq("vb"))

        s5p = _s5_params(s5_lambda_re[l], s5_lambda_im[l], s5_log_dt[l], s5_b_re[l], s5_b_im[l],
                         s5_c_re[l], s5_c_im[l])
        o_c = _s5_branch(seq("cu"), s5p, s5_d[l], w_glu[l], b_glu[l])

        k_cmp, v_cmp = _compress(seq("kvc"), nsa_pe[l], nsa_w1[l], nsa_w2[l])
        o_d = _nsa_attention(seq("dq"), seq("dqr"), seq("dg"), k_cmp, v_cmp,
                             seq("ks"), seq("vs"), seq("kw"), seq("vw"))

        k_mem, v_mem = _mem_kv(mem2, mem_norm_g[l], w_mem_kv[l])
        o_e = _mem_attention(seq("eq"), k_mem.reshape(bsz, -1, w), v_mem.reshape(bsz, -1, w))

        branches = [o.reshape(t, w) for o in (o_a, o_b, o_c, o_d, o_e)]
        x2 = _merge(x2, norm_g[l], branches, proj["zs"], w_merge, b_merge, w_branch, w_out, l,
                    final_g, final_norm=(l == depth - 1))
    return x2.reshape(bsz, s_len, d).astype(x.dtype)
```

```python
import functools
import math

import numpy as np
import jax
import jax.numpy as jnp
from jax import lax
from jax.experimental import pallas as pl
from jax.experimental.pallas import tpu as pltpu

F32 = jnp.float32
BF16 = jnp.bfloat16

HEAD_DIM = 64
BRANCH_WIDTH = 256
N_HEADS = 4
N_BRANCHES = 5
DIFF_QK_DIM = 32
DIL_PATTERNS = ((128, 1), (512, 4), (2048, 16))
S5_GROUP = 16
S5_GROUPS = 16
S5_STATE = 64
CMP_BLOCK = 32
CMP_STRIDE = 16
SLC_BLOCK = 64
N_SELECT = 16
WINDOW = 512
ROPE_THETA = 10000.0
RMS_EPS = 1e-6
NEG_INF = -1e30
FORCE_SCORE = 1e9

V7X_VMEM_BYTES = 64 * 1024 * 1024
BIG_VMEM_LIMIT = V7X_VMEM_BYTES - 8 * 1024 * 1024

LANES = 128
ATT_TILE = 512
ROW_TILE = 512
MERGE_ROW_TILE = 256
MEM_ATT_TILE = 2048
S5_CHUNK = 128

_NT = (((1,), (1,)), ((), ()))


def _rms(x, g):
    return x * lax.rsqrt(jnp.mean(x * x, axis=-1, keepdims=True) + RMS_EPS) * g


def _dot(a, b):
    return jnp.dot(a, b, preferred_element_type=F32)


def _dot_nt(a, b):
    return lax.dot_general(a, b, _NT, preferred_element_type=F32)


def _split_hi_lo(x):
    hi = x.astype(BF16)
    lo = (x - hi.astype(F32)).astype(BF16)
    return hi, lo


def _dot_hilo(x, w):
    hi, lo = _split_hi_lo(x)
    return _dot(hi, w) + _dot(lo, w)


def _const_spec(shape):
    n = len(shape)
    return pl.BlockSpec(shape, lambda *_: (0,) * n, pipeline_mode=pl.Buffered(1))


def _head_mask(h, dtype):
    lane = lax.broadcasted_iota(jnp.int32, (1, BRANCH_WIDTH), 1)
    return jnp.where((lane >= h * HEAD_DIM) & (lane < (h + 1) * HEAD_DIM), 1.0, 0.0).astype(dtype)


def _low_half():
    return lax.broadcasted_iota(jnp.int32, (1, LANES), 1) < HEAD_DIM


def _swap_halves(x):
    return pltpu.roll(x, HEAD_DIM, 1)


def _augment_heads(v):
    low = _low_half()
    parts = []
    for j in range(v.shape[1] // LANES):
        pair = v[:, j * LANES:(j + 1) * LANES]
        parts += [jnp.where(low, pair, 1.0), jnp.where(low, 1.0, pair)]
    return jnp.concatenate(parts, axis=1)


def _shared_kv_variants(kv):
    low = _low_half()
    sw = _swap_halves(kv)
    k2 = jnp.concatenate([jnp.where(low, kv, 0.0), jnp.where(low, 0.0, sw)], axis=1)
    v2 = jnp.concatenate([jnp.where(low, sw, 1.0), jnp.where(low, 1.0, kv)], axis=1)
    return k2, v2


def _normalise_pair(acc_even, acc_odd):
    return jnp.where(_low_half(), acc_even / _swap_halves(acc_even), acc_odd / _swap_halves(acc_odd))


_IN_GROUPS = ("qa", "ka", "va", "az", "qb", "kb", "vb", "bz", "cu", "cz", "dq", "kvcs", "kvwg", "dz", "eq", "ez")
_IN_OFFS = {name: i * BRANCH_WIDTH for i, name in enumerate(_IN_GROUPS)}
_IN_GATE_END = 3212
_IN_SCALED = ("qb", "dq", "eq")
_IN_OUTS = (("qa", 256, BF16), ("ka", 256, BF16), ("va", 512, BF16),
            ("qb", 256, BF16), ("kb", 256, BF16), ("vb", 512, BF16),
            ("cu", 256, F32), ("dq", 256, BF16), ("dqr", 256, BF16), ("kvc", 128, F32),
            ("ks", 256, BF16), ("vs", 256, BF16), ("kw", 256, BF16), ("vw", 256, BF16),
            ("dg", 128, F32), ("eq", 256, BF16), ("zs", 1280, F32))


def _rotate_half(y, group):
    half = group // 2
    lane = lax.broadcasted_iota(jnp.int32, (1, LANES), 1)
    first = (lane & (group - 1)) < half
    parts = []
    for j in range(y.shape[1] // LANES):
        v = y[:, j * LANES:(j + 1) * LANES]
        parts.append(jnp.where(first, pltpu.roll(v, LANES - half, 1), pltpu.roll(v, half, 1)))
    return jnp.concatenate(parts, axis=1)


def _in_proj_kernel(x_ref, g_ref, w_ref, cosa_ref, sina_ref, cosb_ref, sinb_ref, cosk_ref, sink_ref, *out_refs):
    out = dict(zip([n for n, _, _ in _IN_OUTS], out_refs))
    h = _rms(x_ref[...], g_ref[...]).astype(BF16)

    def proj(name):
        off = _IN_OFFS[name]
        y = _dot(h, w_ref[:, off:off + BRANCH_WIDTH])
        return y * HEAD_DIM ** -0.5 if name in _IN_SCALED else y

    def rope(y, cos_ref, sin_ref, group):
        return y * cos_ref[...] + _rotate_half(y, group) * sin_ref[...]

    def put(name, y):
        out[name][...] = y.astype(out[name].dtype)

    put("qa", rope(proj("qa"), cosa_ref, sina_ref, DIFF_QK_DIM))
    put("ka", rope(proj("ka"), cosa_ref, sina_ref, DIFF_QK_DIM))
    put("va", _augment_heads(proj("va")))
    put("qb", rope(proj("qb"), cosb_ref, sinb_ref, HEAD_DIM))
    put("kb", rope(proj("kb"), cosb_ref, sinb_ref, HEAD_DIM))
    put("vb", _augment_heads(proj("vb")))
    put("cu", proj("cu"))
    dq = proj("dq")
    put("dq", dq)
    put("dqr", rope(dq, cosb_ref, sinb_ref, HEAD_DIM))
    kvcs = proj("kvcs")
    kvwg = proj("kvwg")
    put("kvc", kvcs[:, :LANES])
    put("dg", kvwg[:, LANES:])
    for kv, k_name, v_name in ((kvcs[:, LANES:], "ks", "vs"), (kvwg[:, :LANES], "kw", "vw")):
        k2, v2 = _shared_kv_variants(rope(kv, cosk_ref, sink_ref, HEAD_DIM))
        put(k_name, k2)
        put(v_name, v2)
    put("eq", proj("eq"))
    for n, name in enumerate(("az", "bz", "cz", "dz", "ez")):
        z = proj(name)
        out["zs"][:, n * BRANCH_WIDTH:(n + 1) * BRANCH_WIDTH] = z * jax.nn.sigmoid(z)


def _in_weights(w):
    pad = _IN_OFFS["dz"] - _IN_GATE_END
    assert w.shape[1] + pad == len(_IN_GROUPS) * BRANCH_WIDTH
    zeros = jnp.zeros((w.shape[0], pad), w.dtype)
    return jnp.concatenate([w[:, :_IN_GATE_END], zeros, w[:, _IN_GATE_END:]], axis=1).astype(BF16)


def _rope_tables(s_len):
    def table(group, width):
        half = group // 2
        inv_freq = ROPE_THETA ** (-jnp.arange(half, dtype=F32) / half)
        ang = jnp.arange(s_len, dtype=F32)[:, None] * inv_freq[None, :]
        cos = jnp.tile(jnp.cos(ang), (1, width // half))
        sin = jnp.tile(jnp.concatenate([-jnp.sin(ang), jnp.sin(ang)], axis=1), (1, width // group))
        return cos, sin
    cos_a, sin_a = table(DIFF_QK_DIM, BRANCH_WIDTH)
    cos_b, sin_b = table(HEAD_DIM, BRANCH_WIDTH)
    cos_k = jnp.concatenate([cos_b[:, :HEAD_DIM], jnp.ones((s_len, HEAD_DIM), F32)], axis=1)
    sin_k = jnp.concatenate([sin_b[:, :HEAD_DIM], jnp.zeros((s_len, HEAD_DIM), F32)], axis=1)
    return cos_a, sin_a, cos_b, sin_b, cos_k, sin_k


def _in_proj(x2, g, wcat, tables, s_len):
    t, d = x2.shape
    tm = ROW_TILE
    nsb = s_len // tm
    row = lambda w: pl.BlockSpec((tm, w), lambda i: (i, 0))
    tab = lambda a: pl.BlockSpec((tm, a.shape[1]), lambda i: (i % nsb, 0))
    return pl.pallas_call(
        _in_proj_kernel,
        grid=(t // tm,),
        in_specs=[row(d), _const_spec((1, d)), _const_spec(wcat.shape)] + [tab(a) for a in tables],
        out_specs=[row(w) for _, w, _ in _IN_OUTS],
        out_shape=[jax.ShapeDtypeStruct((t, w), dt) for _, w, dt in _IN_OUTS],
        compiler_params=pltpu.CompilerParams(dimension_semantics=("arbitrary",),
                                             vmem_limit_bytes=BIG_VMEM_LIMIT),
        name="in_proj",
    )(x2, g.reshape(1, d), wcat, *tables)


def _lane_fold(x, op):
    parts = [x[:, j * LANES:(j + 1) * LANES] for j in range(x.shape[1] // LANES)]
    return functools.reduce(op, parts)


def _strip_attention(chains, tq, kb_lo, kb_hi, bias_fn, last_bias, scratch, exp_scale=1.0):
    s_ref, m_ref, acc_ref = scratch
    n = len(chains)
    reps = tq // LANES
    c1 = exp_scale * math.log2(math.e)

    def tile(cache, ref, col, width, kb, ntiles=1):
        key = (id(ref), col, width)
        if key not in cache:
            cache[key] = ref[pl.ds(pl.multiple_of(kb * tq, tq), ntiles * tq), col:col + width]
        return cache[key]

    def scores(kb, extra):
        bias = bias_fn(kb) if bias_fn is not None else None
        if extra is not None:
            bias = extra if bias is None else bias + extra
        cache, out = {}, []
        for q, (k_ref, k_col), _ in chains:
            s = _dot_nt(q, tile(cache, k_ref, k_col, q.shape[1], kb)) * c1
            out.append(s if bias is None else s + bias)
        return out

    def tile_pairs(lo, hi, step):
        def two(j, carry):
            step(lo + 2 * j, 2)
            return carry
        cnt = hi - lo
        lax.fori_loop(0, lax.shift_right_arithmetic(cnt, 1), two, 0)

        @pl.when((cnt & 1) == 1)
        def _():
            step(hi - 1, 1)

    m_ref[0:n] = jnp.full((n,) + m_ref.shape[1:], NEG_INF, F32)

    def pass1(kb, ntiles):
        tiles = [scores(kb + j, None) for j in range(ntiles)]
        for i in range(n):
            m = m_ref[i]
            for j in range(ntiles):
                s_ref[i, kb + j] = tiles[j][i]
                m = jnp.maximum(m, _lane_fold(tiles[j][i], jnp.maximum))
            m_ref[i] = m

    tile_pairs(kb_lo, kb_hi - 1, pass1)
    for i, s in enumerate(scores(kb_hi - 1, last_bias)):
        s_ref[i, kb_hi - 1] = s
        m = jnp.max(jnp.maximum(m_ref[i], _lane_fold(s, jnp.maximum)), axis=-1, keepdims=True)
        m_ref[i] = jnp.broadcast_to(m, m_ref.shape[1:])
        acc_ref[i] = jnp.zeros(acc_ref.shape[1:], F32)

    def pass2(kb, ntiles):
        cache = {}
        for i, (_, _, (v_ref, v_col)) in enumerate(chains):
            m = jnp.concatenate([m_ref[i]] * reps, axis=1)
            p = [jnp.exp2(s_ref[i, kb + j] - m).astype(BF16) for j in range(ntiles)]
            p = p[0] if ntiles == 1 else jnp.concatenate(p, axis=1)
            acc_ref[i] += _dot(p, tile(cache, v_ref, v_col, LANES, kb, ntiles))

    tile_pairs(kb_lo, kb_hi, pass2)
    return [acc_ref[i] for i in range(n)]


def _strip_scratch(n, tq, s_len):
    return [pltpu.VMEM((n, s_len // tq, tq, tq), F32), pltpu.VMEM((n, tq, LANES), F32),
            pltpu.VMEM((n, tq, LANES), F32)]


def _causal_bias(tq):
    r = lax.broadcasted_iota(jnp.int32, (tq, tq), 0)
    c = lax.broadcasted_iota(jnp.int32, (tq, tq), 1)
    return jnp.where(c <= r, 0.0, NEG_INF).astype(F32)


def _att_specs(tq, s_len, k_width, v_width):
    q_spec = pl.BlockSpec((None, tq, BRANCH_WIDTH), lambda b, i: (b, i, 0))
    kv = lambda w: pl.BlockSpec((None, s_len, w), lambda b, i: (b, 0, 0))
    return q_spec, kv(k_width), kv(v_width)


def _diff_kernel(lam_ref, q_ref, k_ref, v_ref, g_ref, hm_ref, o_ref, *scratch, tq, out_scale):
    qi = pl.program_id(1)
    q = q_ref[...]
    lane = lax.broadcasted_iota(jnp.int32, (1, BRANCH_WIDTH), 1)
    lam = lam_ref[0]
    chains = []
    for hc in range(2 * N_HEADS):
        lo = hc * DIFF_QK_DIM
        cmask = jnp.where((lane >= lo) & (lane < lo + DIFF_QK_DIM), 1.0, 0.0).astype(BF16)
        chains.append((q * cmask, (k_ref, 0), (v_ref, (hc // 2) * LANES)))
    acc = _strip_attention(chains, tq, 0, qi + 1, None, _causal_bias(tq), scratch,
                           exp_scale=DIFF_QK_DIM ** -0.5)
    halves = []
    for pair in range(N_HEADS // 2):
        even, odd = 4 * pair, 4 * pair + 2
        halves.append(_normalise_pair(acc[even], acc[odd]) - lam * _normalise_pair(acc[even + 1], acc[odd + 1]))
    o = jnp.concatenate(halves, axis=1)
    ms = _dot_hilo(o * o, hm_ref[...])
    o_ref[...] = o * lax.rsqrt(ms + RMS_EPS) * g_ref[...] * out_scale


def _diff_attention(q, k, v_aug, lam, subln_g, lam_init):
    b, s_len, w = q.shape
    tq = ATT_TILE
    q_spec, k_spec, v_spec = _att_specs(tq, s_len, w, v_aug.shape[-1])
    head = np.arange(w) // HEAD_DIM
    hm = jnp.asarray((head[:, None] == head[None, :]) / HEAD_DIM, dtype=BF16)
    g = jnp.tile(subln_g.astype(F32), N_HEADS).reshape(1, w)
    return pl.pallas_call(
        functools.partial(_diff_kernel, tq=tq, out_scale=1.0 - lam_init),
        grid=(b, s_len // tq),
        in_specs=[pl.BlockSpec(memory_space=pltpu.SMEM), q_spec, k_spec, v_spec,
                  _const_spec((1, w)), _const_spec((w, w))],
        out_specs=q_spec,
        out_shape=jax.ShapeDtypeStruct((b, s_len, w), F32),
        scratch_shapes=_strip_scratch(2 * N_HEADS, tq, s_len),
        compiler_params=pltpu.CompilerParams(dimension_semantics=("arbitrary", "arbitrary"),
                                             vmem_limit_bytes=BIG_VMEM_LIMIT),
        name="diff_attention",
    )(lam.reshape(1), q, k, v_aug, g, hm)


def _dil_kernel(q_ref, k_ref, v_ref, bias_ref, o_ref, *scratch, tq):
    qi = pl.program_id(1)
    q = q_ref[...]
    chains = [(q * _head_mask(h, BF16), (k_ref, 0), (v_ref, h * LANES)) for h in range(N_HEADS)]
    acc = _strip_attention(chains, tq, 0, qi + 1, lambda kb: bias_ref[qi - kb], None, scratch)
    o_ref[...] = jnp.concatenate([_normalise_pair(acc[0], acc[1]), _normalise_pair(acc[2], acc[3])], axis=1)


def _dilated_bias(tq, s_len):
    nq = s_len // tq
    d = (np.arange(nq)[:, None, None] * tq + np.arange(tq)[None, :, None] - np.arange(tq)[None, None, :])
    count = np.zeros(d.shape, np.float64)
    for window, dil in DIL_PATTERNS:
        count += (d >= 0) & (d <= window) & (d % dil == 0)
    return np.where(count > 0, np.log2(np.maximum(count, 1.0)), NEG_INF).astype(np.float32)


def _dilated_attention(q, k, v_aug):
    b, s_len, w = q.shape
    tq = ATT_TILE
    q_spec, k_spec, v_spec = _att_specs(tq, s_len, w, v_aug.shape[-1])
    bias = jnp.asarray(_dilated_bias(tq, s_len))
    return pl.pallas_call(
        functools.partial(_dil_kernel, tq=tq),
        grid=(b, s_len // tq),
        in_specs=[q_spec, k_spec, v_spec, _const_spec(bias.shape)],
        out_specs=q_spec,
        out_shape=jax.ShapeDtypeStruct((b, s_len, w), F32),
        scratch_shapes=_strip_scratch(N_HEADS, tq, s_len),
        compiler_params=pltpu.CompilerParams(dimension_semantics=("arbitrary", "arbitrary"),
                                             vmem_limit_bytes=BIG_VMEM_LIMIT),
        name="dilated_attention",
    )(q, k, v_aug, bias)


def _s5_kernel(u_ref, bm_ref, cm_ref, are_ref, aim_ref, d_ref, wg_ref, bg_ref, o_ref, st_ref, *xs_refs,
               ts, nb):
    n = S5_GROUPS * S5_STATE
    nc = n // LANES

    @pl.when(pl.program_id(0) == 0)
    def _():
        st_ref[...] = jnp.zeros(st_ref.shape, F32)

    u = u_ref[...].reshape(nb * ts, u_ref.shape[-1])
    bu = _dot(u.astype(BF16), bm_ref[...])
    for c, x_ref in enumerate(xs_refs):
        for b in range(nb):
            x_ref[pl.ds(b, ts, stride=nb), :] = bu[b * ts:(b + 1) * ts, c * LANES:(c + 1) * LANES]
    a_re = jnp.broadcast_to(are_ref[...], (nb, n))
    a_im = jnp.broadcast_to(aim_ref[...], (nb, n))

    def step(t, carry):
        x_re, x_im = carry
        rows = pl.ds(pl.multiple_of(t * nb, nb), nb)
        bu_re = jnp.concatenate([x_ref[rows, :] for x_ref in xs_refs[:nc]], axis=1)
        bu_im = jnp.concatenate([x_ref[rows, :] for x_ref in xs_refs[nc:]], axis=1)
        n_re = a_re * x_re - a_im * x_im + bu_re
        n_im = a_re * x_im + a_im * x_re + bu_im
        for c in range(nc):
            xs_refs[c][rows, :] = n_re[:, c * LANES:(c + 1) * LANES]
            xs_refs[nc + c][rows, :] = n_im[:, c * LANES:(c + 1) * LANES]
        return n_re, n_im

    x_re, x_im = lax.fori_loop(0, ts, step, (st_ref[:, 0:n], st_ref[:, n:2 * n]))
    st_ref[:, 0:n] = x_re
    st_ref[:, n:2 * n] = x_im

    xs = jnp.concatenate(
        [jnp.concatenate([x_ref[pl.ds(b, ts, stride=nb), :].astype(BF16) for x_ref in xs_refs], axis=1)
         for b in range(nb)], axis=0)
    y = _dot(xs, cm_ref[...]) + d_ref[...] * u
    t = _dot(jax.nn.gelu(y).astype(BF16), wg_ref[...]) + bg_ref[...]
    o_ref[...] = (t[:, :BRANCH_WIDTH] * jax.nn.sigmoid(t[:, BRANCH_WIDTH:])).reshape(o_ref.shape)


def _s5_params(lam_re, lam_im, log_dt, b_re, b_im, c_re, c_im):
    g, n, p = S5_GROUPS, S5_STATE, S5_GROUP
    lr, li = lam_re.astype(F32), lam_im.astype(F32)
    dt = jnp.exp(log_dt.astype(F32))[:, None]
    mag = jnp.exp(lr * dt)
    a_re, a_im = mag * jnp.cos(li * dt), mag * jnp.sin(li * dt)
    den = lr * lr + li * li
    n_re, n_im = a_re - 1.0, a_im
    z_re = (n_re * lr + n_im * li) / den
    z_im = (n_im * lr - n_re * li) / den
    br, bi = b_re.astype(F32), b_im.astype(F32)
    bb_re = z_re[..., None] * br - z_im[..., None] * bi
    bb_im = z_re[..., None] * bi + z_im[..., None] * br
    eye = jnp.eye(g, dtype=F32)
    blockdiag_in = lambda t: jnp.einsum("gnp,gh->gphn", t, eye).reshape(g * p, g * n)
    blockdiag_out = lambda t: jnp.einsum("gpn,gh->gnhp", t, eye).reshape(g * n, g * p)
    bm = jnp.concatenate([blockdiag_in(bb_re), blockdiag_in(bb_im)], axis=1)
    cm = jnp.concatenate([blockdiag_out(c_re.astype(F32)), -blockdiag_out(c_im.astype(F32))], axis=0)
    return bm.astype(BF16), cm.astype(BF16), a_re.reshape(1, g * n), a_im.reshape(1, g * n)


def _s5_branch(u, params, d_skip, w_glu, b_glu):
    nb, s_len, w = u.shape
    assert nb == 8
    ts = S5_CHUNK
    bm, cm, a_re, a_im = params
    n2 = bm.shape[1]
    blk = pl.BlockSpec((nb, ts, w), lambda i: (0, i, 0))
    return pl.pallas_call(
        functools.partial(_s5_kernel, ts=ts, nb=nb),
        grid=(s_len // ts,),
        in_specs=[blk, _const_spec(bm.shape), _const_spec(cm.shape), _const_spec(a_re.shape),
                  _const_spec(a_im.shape), _const_spec((1, w)), _const_spec(w_glu.shape),
                  _const_spec((1, 2 * w))],
        out_specs=blk,
        out_shape=jax.ShapeDtypeStruct((nb, s_len, w), F32),
        scratch_shapes=[pltpu.VMEM((nb, n2), F32)] + [pltpu.VMEM((ts * nb, LANES), F32)] * (n2 // LANES),
        compiler_params=pltpu.CompilerParams(dimension_semantics=("arbitrary",),
                                             vmem_limit_bytes=BIG_VMEM_LIMIT),
        name="s5_scan",
    )(u, bm, cm, a_re, a_im, d_skip.astype(F32).reshape(1, w), w_glu.astype(BF16),
      b_glu.astype(F32).reshape(1, 2 * w))


def _compress_kernel(r_ref, pe_top_ref, pe_bot_ref, w_top_ref, w_bot_ref, w2_ref, k_ref, v_ref):
    r = r_ref[...]
    top = (r + pe_top_ref[...]).astype(BF16)
    nxt = pltpu.roll(r, r.shape[0] - 1, 0)
    bot = (nxt + pe_bot_ref[...]).astype(BF16)
    hid = jax.nn.gelu(_dot(top, w_top_ref[...]) + _dot(bot, w_bot_ref[...]))
    k2, v2 = _shared_kv_variants(_dot(hid.astype(BF16), w2_ref[...]))
    k_ref[...] = k2.astype(BF16)
    v_ref[...] = v2.astype(BF16)


def _compress(kvc, pe, w1, w2):
    b, s_len, _ = kvc.shape
    nr = s_len // CMP_STRIDE
    per = CMP_BLOCK // CMP_STRIDE
    assert per == 2
    hid = w1.shape[-1]
    r = kvc.reshape(b, nr, CMP_STRIDE * 2 * HEAD_DIM)
    w1r = w1.astype(F32).reshape(2, per, CMP_STRIDE, HEAD_DIM, hid)
    per_r = pe.astype(F32).reshape(2, per, CMP_STRIDE, HEAD_DIM)

    def expand(j):
        wk = jnp.pad(w1r[0, j], ((0, 0), (0, HEAD_DIM), (0, hid)))
        wv = jnp.pad(w1r[1, j], ((0, 0), (HEAD_DIM, 0), (hid, 0)))
        return (wk + wv).reshape(CMP_STRIDE * 2 * HEAD_DIM, 2 * hid).astype(BF16)

    pe_rows = [jnp.concatenate([per_r[0, j], per_r[1, j]], axis=-1).reshape(1, -1) for j in range(per)]
    z = jnp.zeros((hid, HEAD_DIM), F32)
    w2f = w2.astype(F32)
    w2x = jnp.concatenate([jnp.concatenate([w2f[0], z], axis=1),
                           jnp.concatenate([z, w2f[1]], axis=1)], axis=0).astype(BF16)
    w_top, w_bot = expand(0), expand(1)
    blk = pl.BlockSpec((None, nr, r.shape[-1]), lambda i: (i, 0, 0))
    out = pl.BlockSpec((None, nr, BRANCH_WIDTH), lambda i: (i, 0, 0))
    return pl.pallas_call(
        _compress_kernel,
        grid=(b,),
        in_specs=[blk, _const_spec(pe_rows[0].shape), _const_spec(pe_rows[1].shape),
                  _const_spec(w_top.shape), _const_spec(w_bot.shape), _const_spec(w2x.shape)],
        out_specs=[out, out],
        out_shape=[jax.ShapeDtypeStruct((b, nr, BRANCH_WIDTH), BF16)] * 2,
        compiler_params=pltpu.CompilerParams(dimension_semantics=("arbitrary",)),
        name="nsa_compress",
    )(r, pe_rows[0], pe_rows[1], w_top, w_bot, w2x)


def _nsa_kernel(dq_ref, dqr_ref, dg_ref, kc_ref, vc_ref, ks_ref, vs_ref, kw_ref, vw_ref,
                ovt_ref, exp_ref, gsel_ref, wbias_ref, o_ref, *scratch, tq, n_cmp, n_slc, n_sel):
    qi = pl.program_id(1)
    t0 = qi * tq
    nr = kc_ref.shape[0]
    low = _low_half()
    q_cols = lambda h: slice((h // 2) * LANES, (h // 2 + 1) * LANES)
    kv_col = lambda h: (h % 2) * LANES

    dq = dq_ref[...]
    tpos = t0 + lax.broadcasted_iota(jnp.int32, (tq, nr), 0)
    ci = lax.broadcasted_iota(jnp.int32, (tq, nr), 1)
    cmask = (ci * CMP_STRIDE + (CMP_BLOCK - 1) <= tpos) & (ci < n_cmp)
    p_sum = jnp.zeros((tq, nr), F32)
    o_heads = []
    for h in range(N_HEADS):
        s = jnp.where(cmask, _dot_nt(dq[:, q_cols(h)], kc_ref[:, kv_col(h):kv_col(h) + LANES]), NEG_INF)
        m = jnp.max(s, axis=-1, keepdims=True)
        e = jnp.where(cmask, jnp.exp(s - m), 0.0)
        p = e / jnp.maximum(jnp.sum(e, axis=-1, keepdims=True), 1e-30)
        p_sum = p_sum + p
        o_heads.append(_dot(p.astype(BF16), vc_ref[:, kv_col(h):kv_col(h) + LANES]))
    o_cmp = jnp.concatenate([jnp.where(low, o_heads[0], o_heads[1]), jnp.where(low, o_heads[2], o_heads[3])],
                            axis=1)

    p_hi, p_lo = _split_hi_lo(p_sum)
    ovt = ovt_ref[...]
    imp_t = _dot_nt(ovt, p_hi) + _dot_nt(ovt, p_lo)
    blk = lax.broadcasted_iota(jnp.int32, (n_slc, tq), 0)
    qblk = lax.shift_right_arithmetic(t0 + lax.broadcasted_iota(jnp.int32, (n_slc, tq), 1),
                                      int(math.log2(SLC_BLOCK)))
    forced = jnp.where(blk == 0, 1, jnp.where(blk == qblk, 1, jnp.where(blk == qblk - 1, 1, 0)))
    score = jnp.where(blk <= qblk, jnp.where(forced > 0, FORCE_SCORE, imp_t), NEG_INF)
    rank = jnp.zeros((n_slc, tq), F32)
    for i in range(n_slc):
        si = score[i:i + 1, :]
        tie = jnp.where(blk > i, 1.0, 0.0)
        rank = rank + jnp.where(si > score, 1.0, jnp.where(si == score, tie, 0.0))
    sel_bias_t = jnp.where(rank < n_sel, jnp.where(score > 0.5 * NEG_INF, 0.0, NEG_INF), NEG_INF)
    pad_rows = exp_ref.shape[1] - n_slc
    sel_bias = jnp.concatenate([sel_bias_t, jnp.full((pad_rows, tq), NEG_INF, F32)], axis=0).T.astype(BF16)

    dqr = dqr_ref[...]
    pairs = lambda acc: jnp.concatenate([_normalise_pair(acc[0], acc[1]), _normalise_pair(acc[2], acc[3])], axis=1)
    chains = [(dqr[:, q_cols(h)], (ks_ref, kv_col(h)), (vs_ref, kv_col(h))) for h in range(N_HEADS)]
    o_slc = pairs(_strip_attention(chains, tq, 0, qi + 1, lambda kb: _dot(sel_bias, exp_ref[kb]),
                                   _causal_bias(tq), scratch))
    chains = [(dqr[:, q_cols(h)], (kw_ref, kv_col(h)), (vw_ref, kv_col(h))) for h in range(N_HEADS)]
    win_lo = jnp.maximum(qi - (wbias_ref.shape[0] - 1), 0)
    o_win = pairs(_strip_attention(chains, tq, win_lo, qi + 1, lambda kb: wbias_ref[qi - kb], None, scratch))

    gates = _dot_hilo(jax.nn.sigmoid(dg_ref[...]), gsel_ref[...])
    w = BRANCH_WIDTH
    o_ref[...] = gates[:, :w] * o_cmp + gates[:, w:2 * w] * o_slc + gates[:, 2 * w:] * o_win


def _nsa_constants(tq, s_len):
    n_cmp = (s_len - CMP_BLOCK) // CMP_STRIDE + 1
    n_slc = s_len // SLC_BLOCK
    nr = s_len // CMP_STRIDE
    c0 = np.arange(n_cmp)[:, None] * CMP_STRIDE
    s0 = np.arange(n_slc)[None, :] * SLC_BLOCK
    overlap = np.clip(np.minimum(c0 + CMP_BLOCK, s0 + SLC_BLOCK) - np.maximum(c0, s0), 0, None) / CMP_STRIDE
    ovt = np.zeros((n_slc, nr), np.float32)
    ovt[:, :n_cmp] = overlap.T
    rows = -(-n_slc // 128) * 128
    expand = np.zeros((s_len // tq, rows, tq), np.float32)
    tok = np.arange(s_len)
    expand[tok // tq, tok // SLC_BLOCK, tok % tq] = 1.0
    gsel = np.zeros((128, 3 * BRANCH_WIDTH), np.float32)
    for h in range(N_HEADS):
        for j in range(3):
            gsel[h * 3 + j, j * BRANCH_WIDTH + h * HEAD_DIM:j * BRANCH_WIDTH + (h + 1) * HEAD_DIM] = 1.0
    nwin = -(-WINDOW // tq) + 1
    d = np.arange(nwin)[:, None, None] * tq + np.arange(tq)[None, :, None] - np.arange(tq)[None, None, :]
    wbias = np.where((d >= 0) & (d < WINDOW), 0.0, NEG_INF).astype(np.float32)
    return n_cmp, n_slc, ovt, expand, gsel, wbias


def _nsa_attention(dq, dqr, dg, k_cmp, v_cmp, ks, vs, kw, vw):
    b, s_len, w = dq.shape
    tq = ATT_TILE
    n_cmp, n_slc, ovt, expand, gsel, wbias = _nsa_constants(tq, s_len)
    q_spec, kv_spec, _ = _att_specs(tq, s_len, w, w)
    nr = k_cmp.shape[1]
    cmp_spec = pl.BlockSpec((None, nr, w), lambda bi, i: (bi, 0, 0))
    g_spec = pl.BlockSpec((None, tq, dg.shape[-1]), lambda bi, i: (bi, i, 0))
    consts = [jnp.asarray(ovt, BF16), jnp.asarray(expand, BF16), jnp.asarray(gsel, BF16), jnp.asarray(wbias)]
    return pl.pallas_call(
        functools.partial(_nsa_kernel, tq=tq, n_cmp=n_cmp, n_slc=n_slc, n_sel=min(N_SELECT, n_slc)),
        grid=(b, s_len // tq),
        in_specs=[q_spec, q_spec, g_spec, cmp_spec, cmp_spec, kv_spec, kv_spec, kv_spec, kv_spec]
        + [_const_spec(c.shape) for c in consts],
        out_specs=q_spec,
        out_shape=jax.ShapeDtypeStruct((b, s_len, w), F32),
        scratch_shapes=_strip_scratch(N_HEADS, tq, s_len),
        compiler_params=pltpu.CompilerParams(dimension_semantics=("arbitrary", "arbitrary"),
                                             vmem_limit_bytes=BIG_VMEM_LIMIT),
        name="nsa_attention",
    )(dq, dqr, dg, k_cmp, v_cmp, ks, vs, kw, vw, *consts)


def _mem_kv_kernel(m_ref, g_ref, w_ref, k_ref, v_ref):
    kv = _dot(_rms(m_ref[...], g_ref[...]).astype(BF16), w_ref[...])
    k_ref[...] = kv[:, :BRANCH_WIDTH].astype(BF16)
    v_ref[...] = kv[:, BRANCH_WIDTH:].astype(BF16)


def _mem_kv(mem2, g, w_kv):
    rows, d = mem2.shape
    tm = ROW_TILE
    out = pl.BlockSpec((tm, BRANCH_WIDTH), lambda i: (i, 0))
    return pl.pallas_call(
        _mem_kv_kernel,
        grid=(rows // tm,),
        in_specs=[pl.BlockSpec((tm, d), lambda i: (i, 0)), _const_spec((1, d)), _const_spec(w_kv.shape)],
        out_specs=[out, out],
        out_shape=[jax.ShapeDtypeStruct((rows, BRANCH_WIDTH), BF16)] * 2,
        compiler_params=pltpu.CompilerParams(dimension_semantics=("arbitrary",)),
        name="mem_kv",
    )(mem2, g.astype(F32).reshape(1, d), w_kv.astype(BF16))


def _mem_attn_kernel(q_ref, k_ref, v_ref, o_ref):
    q = q_ref[...]
    k = k_ref[...]
    v = v_ref[...]
    o = jnp.zeros(o_ref.shape, F32)
    for h in range(N_HEADS):
        s = _dot_nt(q * _head_mask(h, BF16), k)
        e = jnp.exp(s - jnp.max(s, axis=-1, keepdims=True))
        oh = _dot(e.astype(BF16), v) / jnp.sum(e, axis=-1, keepdims=True)
        o = jnp.where(_head_mask(h, F32) > 0.5, oh, o)
    o_ref[...] = o


def _mem_attention(q, k, v):
    b, s_len, w = q.shape
    tq = min(MEM_ATT_TILE, s_len)
    q_spec = pl.BlockSpec((None, tq, w), lambda bi, i: (bi, i, 0))
    kv_spec = pl.BlockSpec((None, k.shape[1], w), lambda bi, i: (bi, 0, 0))
    return pl.pallas_call(
        _mem_attn_kernel,
        grid=(b, s_len // tq),
        in_specs=[q_spec, kv_spec, kv_spec],
        out_specs=q_spec,
        out_shape=jax.ShapeDtypeStruct((b, s_len, w), F32),
        compiler_params=pltpu.CompilerParams(dimension_semantics=("arbitrary", "arbitrary")),
        name="mem_attention",
    )(q, k, v)


def _merge_kernel(x_ref, g_ref, oa_ref, ob_ref, oc_ref, od_ref, oe_ref, zs_ref, wm_ref, bm_ref, wb_ref,
                  wo_ref, fg_ref, o_ref, *, final_norm):
    x = x_ref[...]
    d = x.shape[-1]
    w = BRANCH_WIDTH
    h = _rms(x, g_ref[...]).astype(BF16)
    mixed = jnp.zeros(x.shape, F32)
    for n, br_ref in enumerate((oa_ref, ob_ref, oc_ref, od_ref, oe_ref)):
        br = (br_ref[...] * zs_ref[:, n * w:(n + 1) * w]).astype(BF16)
        y = _dot(br, wb_ref[n].astype(BF16))
        gate = jax.nn.sigmoid(_dot(h, wm_ref[:, n * d:(n + 1) * d].astype(BF16)) + bm_ref[:, n * d:(n + 1) * d])
        mixed = mixed + gate * y
    out = x + _dot(mixed.astype(BF16), wo_ref[...].astype(BF16))
    if final_norm:
        out = _rms(out, fg_ref[...])
    o_ref[...] = out


def _merge(x2, g, branches, zs, w_merge, b_merge, w_branch, w_out, layer, final_g, final_norm):
    t, d = x2.shape
    tm = MERGE_ROW_TILE
    row = lambda wd: pl.BlockSpec((tm, wd), lambda i: (i, 0))

    def of_layer(a):
        nd = a.ndim - 1
        return pl.BlockSpec((None,) + a.shape[1:], lambda i: (layer,) + (0,) * nd, pipeline_mode=pl.Buffered(1))

    b_merge = b_merge.astype(F32).reshape(b_merge.shape[0], 1, -1)
    return pl.pallas_call(
        functools.partial(_merge_kernel, final_norm=final_norm),
        grid=(t // tm,),
        in_specs=[row(d), _const_spec((1, d))] + [row(BRANCH_WIDTH)] * N_BRANCHES + [row(zs.shape[1])]
        + [of_layer(w_merge), of_layer(b_merge), of_layer(w_branch), of_layer(w_out), _const_spec((1, d))],
        out_specs=row(d),
        out_shape=jax.ShapeDtypeStruct((t, d), F32),
        compiler_params=pltpu.CompilerParams(dimension_semantics=("arbitrary",),
                                             vmem_limit_bytes=BIG_VMEM_LIMIT),
        name="merge",
    )(x2, g.astype(F32).reshape(1, d), *branches, zs, w_merge.astype(F32), b_merge, w_branch.astype(F32),
      w_out.astype(F32), final_g.astype(F32).reshape(1, d))


def kernel(x, mem, norm_g, w_in, diff_lambda, diff_subln_g, s5_lambda_re, s5_lambda_im, s5_log_dt,
           s5_b_re, s5_b_im, s5_c_re, s5_c_im, s5_d, w_glu, b_glu, nsa_pe, nsa_w1, nsa_w2, mem_norm_g,
           w_mem_kv, w_merge, b_merge, w_branch, w_out, final_g):
    bsz, s_len, d = x.shape
    depth = w_in.shape[0]
    t = bsz * s_len
    w = BRANCH_WIDTH
    tables = _rope_tables(s_len)
    x2 = x.astype(F32).reshape(t, d)
    mem2 = mem.astype(F32).reshape(-1, d)
    for l in range(depth):
        proj = dict(zip([n for n, _, _ in _IN_OUTS],
                        _in_proj(x2, norm_g[l].astype(F32), _in_weights(w_in[l].astype(F32)), tables, s_len)))
        seq = lambda name: proj[name].reshape(bsz, s_len, -1)

        dl = diff_lambda[l].astype(F32)
        lam_init = 0.8 - 0.6 * math.exp(-0.3 * l)
        lam = jnp.exp(jnp.sum(dl[0] * dl[1])) - jnp.exp(jnp.sum(dl[2] * dl[3])) + lam_init
        o_a = _diff_attention(seq("qa"), seq("ka"), seq("va"), lam, diff_subln_g[l], lam_init)

        o_b = _dilated_attention(seq("qb"), seq("kb"), seq("vb"))

        s5p = _s5_params(s5_lambda_re[l], s5_lambda_im[l], s5_log_dt[l], s5_b_re[l], s5_b_im[l],
                         s5_c_re[l], s5_c_im[l])
        o_c = _s5_branch(seq("cu"), s5p, s5_d[l], w_glu[l], b_glu[l])

        k_cmp, v_cmp = _compress(seq("kvc"), nsa_pe[l], nsa_w1[l], nsa_w2[l])
        o_d = _nsa_attention(seq("dq"), seq("dqr"), seq("dg"), k_cmp, v_cmp,
                             seq("ks"), seq("vs"), seq("kw"), seq("vw"))

        k_mem, v_mem = _mem_kv(mem2, mem_norm_g[l], w_mem_kv[l])
        o_e = _mem_attention(seq("eq"), k_mem.reshape(bsz, -1, w), v_mem.reshape(bsz, -1, w))

        branches = [o.reshape(t, w) for o in (o_a, o_b, o_c, o_d, o_e)]
        x2 = _merge(x2, norm_g[l], branches, proj["zs"], w_merge, b_merge, w_branch, w_out, l,
                    final_g, final_norm=(l == depth - 1))
    return x2.reshape(bsz, s_len, d).astype(x.dtype)
```

```python
import functools
import math

import numpy as np
import jax
import jax.numpy as jnp
from jax import lax
from jax.experimental import pallas as pl
from jax.experimental.pallas import tpu as pltpu

F32 = jnp.float32
BF16 = jnp.bfloat16

HEAD_DIM = 64
BRANCH_WIDTH = 256
N_HEADS = 4
N_BRANCHES = 5
DIFF_QK_DIM = 32
DIL_PATTERNS = ((128, 1), (512, 4), (2048, 16))
S5_GROUP = 16
S5_GROUPS = 16
S5_STATE = 64
CMP_BLOCK = 32
CMP_STRIDE = 16
SLC_BLOCK = 64
N_SELECT = 16
WINDOW = 512
ROPE_THETA = 10000.0
RMS_EPS = 1e-6
NEG_INF = -1e30
FORCE_SCORE = 1e9

V7X_VMEM_BYTES = 64 * 1024 * 1024
BIG_VMEM_LIMIT = V7X_VMEM_BYTES - 8 * 1024 * 1024

LANES = 128
ATT_TILE = 512
ROW_TILE = 512
MERGE_ROW_TILE = 256
MEM_ATT_TILE = 2048
S5_CHUNK = 128

_NT = (((1,), (1,)), ((), ()))


def _rms(x, g):
    return x * lax.rsqrt(jnp.mean(x * x, axis=-1, keepdims=True) + RMS_EPS) * g


def _dot(a, b):
    return jnp.dot(a, b, preferred_element_type=F32)


def _dot_nt(a, b):
    return lax.dot_general(a, b, _NT, preferred_element_type=F32)


def _split_hi_lo(x):
    hi = x.astype(BF16)
    lo = (x - hi.astype(F32)).astype(BF16)
    return hi, lo


def _dot_hilo(x, w):
    hi, lo = _split_hi_lo(x)
    return _dot(hi, w) + _dot(lo, w)


def _const_spec(shape):
    n = len(shape)
    return pl.BlockSpec(shape, lambda *_: (0,) * n, pipeline_mode=pl.Buffered(1))


def _head_mask(h, dtype):
    lane = lax.broadcasted_iota(jnp.int32, (1, BRANCH_WIDTH), 1)
    return jnp.where((lane >= h * HEAD_DIM) & (lane < (h + 1) * HEAD_DIM), 1.0, 0.0).astype(dtype)


def _low_half():
    return lax.broadcasted_iota(jnp.int32, (1, LANES), 1) < HEAD_DIM


def _swap_halves(x):
    return pltpu.roll(x, HEAD_DIM, 1)


def _augment_heads(v):
    low = _low_half()
    parts = []
    for j in range(v.shape[1] // LANES):
        pair = v[:, j * LANES:(j + 1) * LANES]
        parts += [jnp.where(low, pair, 1.0), jnp.where(low, 1.0, pair)]
    return jnp.concatenate(parts, axis=1)


def _shared_kv_variants(kv):
    low = _low_half()
    sw = _swap_halves(kv)
    k2 = jnp.concatenate([jnp.where(low, kv, 0.0), jnp.where(low, 0.0, sw)], axis=1)
    v2 = jnp.concatenate([jnp.where(low, sw, 1.0), jnp.where(low, 1.0, kv)], axis=1)
    return k2, v2


def _normalise_pair(acc_even, acc_odd):
    return jnp.where(_low_half(), acc_even / _swap_halves(acc_even), acc_odd / _swap_halves(acc_odd))


_IN_COLS = (("qa", 256), ("ka", 256), ("va", 256), ("qb", 256), ("kb", 256), ("vb", 256), ("cu", 256),
            ("dq", 256), ("kvcs", 256), ("kvwg", 256), ("eq", 256), ("zs", 1280))
_IN_OFFS = dict(zip([n for n, _ in _IN_COLS], np.cumsum([0] + [w for _, w in _IN_COLS])[:-1].tolist()))
_IN_WIDTH = dict(_IN_COLS)
_IN_OUTS = (("qa", 256, BF16), ("ka", 256, BF16), ("va", 512, BF16),
            ("qb", 256, BF16), ("kb", 256, BF16), ("vb", 512, BF16),
            ("cu", 256, F32), ("dq", 256, BF16), ("dqr", 256, BF16), ("kvc", 128, F32),
            ("ks", 256, BF16), ("vs", 256, BF16), ("kw", 256, BF16), ("vw", 256, BF16),
            ("dg", 128, F32), ("eq", 256, BF16), ("zs", 1280, F32))


def _rotate_half(y, group):
    half = group // 2
    lane = lax.broadcasted_iota(jnp.int32, (1, LANES), 1)
    first = (lane & (group - 1)) < half
    parts = []
    for j in range(y.shape[1] // LANES):
        v = y[:, j * LANES:(j + 1) * LANES]
        parts.append(jnp.where(first, pltpu.roll(v, LANES - half, 1), pltpu.roll(v, half, 1)))
    return jnp.concatenate(parts, axis=1)


def _in_proj_kernel(x_ref, g_ref, w_ref, cosa_ref, sina_ref, cosb_ref, sinb_ref, cosk_ref, sink_ref, *out_refs):
    out = dict(zip([n for n, _, _ in _IN_OUTS], out_refs))
    h = _rms(x_ref[...], g_ref[...]).astype(BF16)

    def proj(name):
        off = _IN_OFFS[name]
        return _dot(h, w_ref[:, off:off + _IN_WIDTH[name]])

    def rope(y, cos_ref, sin_ref, group):
        return y * cos_ref[...] + _rotate_half(y, group) * sin_ref[...]

    def put(name, y):
        out[name][...] = y.astype(out[name].dtype)

    put("qa", rope(proj("qa"), cosa_ref, sina_ref, DIFF_QK_DIM))
    put("ka", rope(proj("ka"), cosa_ref, sina_ref, DIFF_QK_DIM))
    put("va", _augment_heads(proj("va")))
    put("qb", rope(proj("qb"), cosb_ref, sinb_ref, HEAD_DIM))
    put("kb", rope(proj("kb"), cosb_ref, sinb_ref, HEAD_DIM))
    put("vb", _augment_heads(proj("vb")))
    put("cu", proj("cu"))
    dq = proj("dq")
    put("dq", dq)
    put("dqr", rope(dq, cosb_ref, sinb_ref, HEAD_DIM))
    kvcs = proj("kvcs")
    kvwg = proj("kvwg")
    put("kvc", kvcs[:, :LANES])
    put("dg", kvwg[:, LANES:])
    for kv, k_name, v_name in ((kvcs[:, LANES:], "ks", "vs"), (kvwg[:, :LANES], "kw", "vw")):
        k2, v2 = _shared_kv_variants(rope(kv, cosk_ref, sink_ref, HEAD_DIM))
        put(k_name, k2)
        put(v_name, v2)
    put("eq", proj("eq"))
    z = proj("zs")
    put("zs", z * jax.nn.sigmoid(z))


def _in_weights(w):
    cut = lambda lo, hi: w[:, lo:hi]
    scale = HEAD_DIM ** -0.5
    n_gate = 3 * N_HEADS
    cols = [cut(0, 768), cut(1024, 1280) * scale, cut(1280, 1792), cut(2048, 2304), cut(2560, 2816) * scale,
            cut(2816, 3200 + n_gate), jnp.zeros((w.shape[0], LANES - n_gate), w.dtype), cut(3468, 3724) * scale,
            cut(768, 1024), cut(1792, 2048), cut(2304, 2560), cut(3212, 3468), cut(3724, 3980)]
    return jnp.concatenate(cols, axis=1).astype(BF16)


def _rope_tables(s_len):
    def table(group, width):
        half = group // 2
        inv_freq = ROPE_THETA ** (-jnp.arange(half, dtype=F32) / half)
        ang = jnp.arange(s_len, dtype=F32)[:, None] * inv_freq[None, :]
        cos = jnp.tile(jnp.cos(ang), (1, width // half))
        sin = jnp.tile(jnp.concatenate([-jnp.sin(ang), jnp.sin(ang)], axis=1), (1, width // group))
        return cos, sin
    cos_a, sin_a = table(DIFF_QK_DIM, BRANCH_WIDTH)
    cos_b, sin_b = table(HEAD_DIM, BRANCH_WIDTH)
    cos_k = jnp.concatenate([cos_b[:, :HEAD_DIM], jnp.ones((s_len, HEAD_DIM), F32)], axis=1)
    sin_k = jnp.concatenate([sin_b[:, :HEAD_DIM], jnp.zeros((s_len, HEAD_DIM), F32)], axis=1)
    return cos_a, sin_a, cos_b, sin_b, cos_k, sin_k


def _in_proj(x2, g, wcat, tables, s_len):
    t, d = x2.shape
    tm = ROW_TILE
    nsb = s_len // tm
    row = lambda w: pl.BlockSpec((tm, w), lambda i: (i, 0))
    tab = lambda a: pl.BlockSpec((tm, a.shape[1]), lambda i: (i % nsb, 0))
    return pl.pallas_call(
        _in_proj_kernel,
        grid=(t // tm,),
        in_specs=[row(d), _const_spec((1, d)), _const_spec(wcat.shape)] + [tab(a) for a in tables],
        out_specs=[row(w) for _, w, _ in _IN_OUTS],
        out_shape=[jax.ShapeDtypeStruct((t, w), dt) for _, w, dt in _IN_OUTS],
        compiler_params=pltpu.CompilerParams(dimension_semantics=("arbitrary",),
                                             vmem_limit_bytes=BIG_VMEM_LIMIT),
        name="in_proj",
    )(x2, g.reshape(1, d), wcat, *tables)


def _lane_fold(x, op):
    parts = [x[:, j * LANES:(j + 1) * LANES] for j in range(x.shape[1] // LANES)]
    return functools.reduce(op, parts)


def _strip_attention(chains, tq, kb_lo, kb_hi, bias_fn, last_bias, scratch, exp_scale=1.0):
    s_ref, m_ref, acc_ref = scratch
    n = len(chains)
    reps = tq // LANES
    c1 = exp_scale * math.log2(math.e)

    def tile(cache, ref, col, width, kb, ntiles=1):
        key = (id(ref), col, width)
        if key not in cache:
            cache[key] = ref[pl.ds(pl.multiple_of(kb * tq, tq), ntiles * tq), col:col + width]
        return cache[key]

    def scores(kb, extra):
        bias = bias_fn(kb) if bias_fn is not None else None
        if extra is not None:
            bias = extra if bias is None else bias + extra
        cache, out = {}, []
        for q, (k_ref, k_col), _ in chains:
            s = _dot_nt(q, tile(cache, k_ref, k_col, q.shape[1], kb)) * c1
            out.append(s if bias is None else s + bias)
        return out

    def tile_pairs(lo, hi, step):
        def two(j, carry):
            step(lo + 2 * j, 2)
            return carry
        cnt = hi - lo
        lax.fori_loop(0, lax.shift_right_arithmetic(cnt, 1), two, 0)

        @pl.when((cnt & 1) == 1)
        def _():
            step(hi - 1, 1)

    m_ref[0:n] = jnp.full((n,) + m_ref.shape[1:], NEG_INF, F32)

    def pass1(kb, ntiles):
        tiles = [scores(kb + j, None) for j in range(ntiles)]
        for i in range(n):
            m = m_ref[i]
            for j in range(ntiles):
                s_ref[i, kb + j] = tiles[j][i]
                m = jnp.maximum(m, _lane_fold(tiles[j][i], jnp.maximum))
            m_ref[i] = m

    tile_pairs(kb_lo, kb_hi - 1, pass1)
    for i, s in enumerate(scores(kb_hi - 1, last_bias)):
        s_ref[i, kb_hi - 1] = s
        m = jnp.max(jnp.maximum(m_ref[i], _lane_fold(s, jnp.maximum)), axis=-1, keepdims=True)
        m_ref[i] = jnp.broadcast_to(m, m_ref.shape[1:])
        acc_ref[i] = jnp.zeros(acc_ref.shape[1:], F32)

    def pass2(kb, ntiles):
        cache = {}
        for i, (_, _, (v_ref, v_col)) in enumerate(chains):
            m = jnp.concatenate([m_ref[i]] * reps, axis=1)
            p = [jnp.exp2(s_ref[i, kb + j] - m).astype(BF16) for j in range(ntiles)]
            p = p[0] if ntiles == 1 else jnp.concatenate(p, axis=1)
            acc_ref[i] += _dot(p, tile(cache, v_ref, v_col, LANES, kb, ntiles))

    tile_pairs(kb_lo, kb_hi, pass2)
    return [acc_ref[i] for i in range(n)]


def _strip_scratch(n, tq, s_len):
    return [pltpu.VMEM((n, s_len // tq, tq, tq), F32), pltpu.VMEM((n, tq, LANES), F32),
            pltpu.VMEM((n, tq, LANES), F32)]


def _causal_bias(tq):
    r = lax.broadcasted_iota(jnp.int32, (tq, tq), 0)
    c = lax.broadcasted_iota(jnp.int32, (tq, tq), 1)
    return jnp.where(c <= r, 0.0, NEG_INF).astype(F32)


def _att_specs(tq, s_len, k_width, v_width):
    q_spec = pl.BlockSpec((None, tq, BRANCH_WIDTH), lambda b, i: (b, i, 0))
    kv = lambda w: pl.BlockSpec((None, s_len, w), lambda b, i: (b, 0, 0))
    return q_spec, kv(k_width), kv(v_width)


def _diff_kernel(lam_ref, q_ref, k_ref, v_ref, g_ref, hm_ref, o_ref, *scratch, tq, out_scale):
    qi = pl.program_id(1)
    q = q_ref[...]
    lane = lax.broadcasted_iota(jnp.int32, (1, BRANCH_WIDTH), 1)
    lam = lam_ref[0]
    chains = []
    for hc in range(2 * N_HEADS):
        lo = hc * DIFF_QK_DIM
        cmask = jnp.where((lane >= lo) & (lane < lo + DIFF_QK_DIM), 1.0, 0.0).astype(BF16)
        chains.append((q * cmask, (k_ref, 0), (v_ref, (hc // 2) * LANES)))
    acc = _strip_attention(chains, tq, 0, qi + 1, None, _causal_bias(tq), scratch,
                           exp_scale=DIFF_QK_DIM ** -0.5)
    halves = []
    for pair in range(N_HEADS // 2):
        even, odd = 4 * pair, 4 * pair + 2
        halves.append(_normalise_pair(acc[even], acc[odd]) - lam * _normalise_pair(acc[even + 1], acc[odd + 1]))
    o = jnp.concatenate(halves, axis=1)
    ms = _dot_hilo(o * o, hm_ref[...])
    o_ref[...] = o * lax.rsqrt(ms + RMS_EPS) * g_ref[...] * out_scale


def _diff_attention(q, k, v_aug, lam, subln_g, lam_init):
    b, s_len, w = q.shape
    tq = ATT_TILE
    q_spec, k_spec, v_spec = _att_specs(tq, s_len, w, v_aug.shape[-1])
    head = np.arange(w) // HEAD_DIM
    hm = jnp.asarray((head[:, None] == head[None, :]) / HEAD_DIM, dtype=BF16)
    g = jnp.tile(subln_g.astype(F32), N_HEADS).reshape(1, w)
    return pl.pallas_call(
        functools.partial(_diff_kernel, tq=tq, out_scale=1.0 - lam_init),
        grid=(b, s_len // tq),
        in_specs=[pl.BlockSpec(memory_space=pltpu.SMEM), q_spec, k_spec, v_spec,
                  _const_spec((1, w)), _const_spec((w, w))],
        out_specs=q_spec,
        out_shape=jax.ShapeDtypeStruct((b, s_len, w), F32),
        scratch_shapes=_strip_scratch(2 * N_HEADS, tq, s_len),
        compiler_params=pltpu.CompilerParams(dimension_semantics=("arbitrary", "arbitrary"),
                                             vmem_limit_bytes=BIG_VMEM_LIMIT),
        name="diff_attention",
    )(lam.reshape(1), q, k, v_aug, g, hm)


def _dil_kernel(q_ref, k_ref, v_ref, bias_ref, o_ref, *scratch, tq):
    qi = pl.program_id(1)
    q = q_ref[...]
    chains = [(q * _head_mask(h, BF16), (k_ref, 0), (v_ref, h * LANES)) for h in range(N_HEADS)]
    acc = _strip_attention(chains, tq, 0, qi + 1, lambda kb: bias_ref[qi - kb], None, scratch)
    o_ref[...] = jnp.concatenate([_normalise_pair(acc[0], acc[1]), _normalise_pair(acc[2], acc[3])], axis=1)


def _dilated_bias(tq, s_len):
    nq = s_len // tq
    d = (np.arange(nq)[:, None, None] * tq + np.arange(tq)[None, :, None] - np.arange(tq)[None, None, :])
    count = np.zeros(d.shape, np.float64)
    for window, dil in DIL_PATTERNS:
        count += (d >= 0) & (d <= window) & (d % dil == 0)
    return np.where(count > 0, np.log2(np.maximum(count, 1.0)), NEG_INF).astype(np.float32)


def _dilated_attention(q, k, v_aug):
    b, s_len, w = q.shape
    tq = ATT_TILE
    q_spec, k_spec, v_spec = _att_specs(tq, s_len, w, v_aug.shape[-1])
    bias = jnp.asarray(_dilated_bias(tq, s_len))
    return pl.pallas_call(
        functools.partial(_dil_kernel, tq=tq),
        grid=(b, s_len // tq),
        in_specs=[q_spec, k_spec, v_spec, _const_spec(bias.shape)],
        out_specs=q_spec,
        out_shape=jax.ShapeDtypeStruct((b, s_len, w), F32),
        scratch_shapes=_strip_scratch(N_HEADS, tq, s_len),
        compiler_params=pltpu.CompilerParams(dimension_semantics=("arbitrary", "arbitrary"),
                                             vmem_limit_bytes=BIG_VMEM_LIMIT),
        name="dilated_attention",
    )(q, k, v_aug, bias)


def _s5_kernel(u_ref, bm_ref, cm_ref, are_ref, aim_ref, d_ref, wg_ref, bg_ref, o_ref, st_ref, *xs_refs,
               ts, nb):
    n = S5_GROUPS * S5_STATE
    nc = n // LANES

    @pl.when(pl.program_id(0) == 0)
    def _():
        st_ref[...] = jnp.zeros(st_ref.shape, F32)

    u = u_ref[...].reshape(nb * ts, u_ref.shape[-1])
    bu = _dot(u.astype(BF16), bm_ref[...])
    for c, x_ref in enumerate(xs_refs):
        for b in range(nb):
            x_ref[pl.ds(b, ts, stride=nb), :] = bu[b * ts:(b + 1) * ts, c * LANES:(c + 1) * LANES]
    a_re = jnp.broadcast_to(are_ref[...], (nb, n))
    a_im = jnp.broadcast_to(aim_ref[...], (nb, n))

    def step(t, carry):
        x_re, x_im = carry
        rows = pl.ds(pl.multiple_of(t * nb, nb), nb)
        bu_re = jnp.concatenate([x_ref[rows, :] for x_ref in xs_refs[:nc]], axis=1)
        bu_im = jnp.concatenate([x_ref[rows, :] for x_ref in xs_refs[nc:]], axis=1)
        n_re = a_re * x_re - a_im * x_im + bu_re
        n_im = a_re * x_im + a_im * x_re + bu_im
        for c in range(nc):
            xs_refs[c][rows, :] = n_re[:, c * LANES:(c + 1) * LANES]
            xs_refs[nc + c][rows, :] = n_im[:, c * LANES:(c + 1) * LANES]
        return n_re, n_im

    x_re, x_im = lax.fori_loop(0, ts, step, (st_ref[:, 0:n], st_ref[:, n:2 * n]))
    st_ref[:, 0:n] = x_re
    st_ref[:, n:2 * n] = x_im

    xs = jnp.concatenate(
        [jnp.concatenate([x_ref[pl.ds(b, ts, stride=nb), :].astype(BF16) for x_ref in xs_refs], axis=1)
         for b in range(nb)], axis=0)
    y = _dot(xs, cm_ref[...]) + d_ref[...] * u
    t = _dot(jax.nn.gelu(y).astype(BF16), wg_ref[...]) + bg_ref[...]
    o_ref[...] = (t[:, :BRANCH_WIDTH] * jax.nn.sigmoid(t[:, BRANCH_WIDTH:])).reshape(o_ref.shape)


def _s5_params(lam_re, lam_im, log_dt, b_re, b_im, c_re, c_im):
    g, n, p = S5_GROUPS, S5_STATE, S5_GROUP
    lr, li = lam_re.astype(F32), lam_im.astype(F32)
    dt = jnp.exp(log_dt.astype(F32))[:, None]
    mag = jnp.exp(lr * dt)
    a_re, a_im = mag * jnp.cos(li * dt), mag * jnp.sin(li * dt)
    den = lr * lr + li * li
    n_re, n_im = a_re - 1.0, a_im
    z_re = (n_re * lr + n_im * li) / den
    z_im = (n_im * lr - n_re * li) / den
    br, bi = b_re.astype(F32), b_im.astype(F32)
    bb_re = z_re[..., None] * br - z_im[..., None] * bi
    bb_im = z_re[..., None] * bi + z_im[..., None] * br
    eye = jnp.eye(g, dtype=F32)
    blockdiag_in = lambda t: jnp.einsum("gnp,gh->gphn", t, eye).reshape(g * p, g * n)
    blockdiag_out = lambda t: jnp.einsum("gpn,gh->gnhp", t, eye).reshape(g * n, g * p)
    bm = jnp.concatenate([blockdiag_in(bb_re), blockdiag_in(bb_im)], axis=1)
    cm = jnp.concatenate([blockdiag_out(c_re.astype(F32)), -blockdiag_out(c_im.astype(F32))], axis=0)
    return bm.astype(BF16), cm.astype(BF16), a_re.reshape(1, g * n), a_im.reshape(1, g * n)


def _s5_branch(u, params, d_skip, w_glu, b_glu):
    nb, s_len, w = u.shape
    assert nb == 8
    ts = S5_CHUNK
    bm, cm, a_re, a_im = params
    n2 = bm.shape[1]
    blk = pl.BlockSpec((nb, ts, w), lambda i: (0, i, 0))
    return pl.pallas_call(
        functools.partial(_s5_kernel, ts=ts, nb=nb),
        grid=(s_len // ts,),
        in_specs=[blk, _const_spec(bm.shape), _const_spec(cm.shape), _const_spec(a_re.shape),
                  _const_spec(a_im.shape), _const_spec((1, w)), _const_spec(w_glu.shape),
                  _const_spec((1, 2 * w))],
        out_specs=blk,
        out_shape=jax.ShapeDtypeStruct((nb, s_len, w), F32),
        scratch_shapes=[pltpu.VMEM((nb, n2), F32)] + [pltpu.VMEM((ts * nb, LANES), F32)] * (n2 // LANES),
        compiler_params=pltpu.CompilerParams(dimension_semantics=("arbitrary",),
                                             vmem_limit_bytes=BIG_VMEM_LIMIT),
        name="s5_scan",
    )(u, bm, cm, a_re, a_im, d_skip.astype(F32).reshape(1, w), w_glu.astype(BF16),
      b_glu.astype(F32).reshape(1, 2 * w))


def _compress_kernel(r_ref, pe_top_ref, pe_bot_ref, w_top_ref, w_bot_ref, w2_ref, k_ref, v_ref):
    r = r_ref[...]
    top = (r + pe_top_ref[...]).astype(BF16)
    nxt = pltpu.roll(r, r.shape[0] - 1, 0)
    bot = (nxt + pe_bot_ref[...]).astype(BF16)
    hid = jax.nn.gelu(_dot(top, w_top_ref[...]) + _dot(bot, w_bot_ref[...]))
    k2, v2 = _shared_kv_variants(_dot(hid.astype(BF16), w2_ref[...]))
    k_ref[...] = k2.astype(BF16)
    v_ref[...] = v2.astype(BF16)


def _compress(kvc, pe, w1, w2):
    b, s_len, _ = kvc.shape
    nr = s_len // CMP_STRIDE
    per = CMP_BLOCK // CMP_STRIDE
    assert per == 2
    hid = w1.shape[-1]
    r = kvc.reshape(b, nr, CMP_STRIDE * 2 * HEAD_DIM)
    w1r = w1.astype(F32).reshape(2, per, CMP_STRIDE, HEAD_DIM, hid)
    per_r = pe.astype(F32).reshape(2, per, CMP_STRIDE, HEAD_DIM)

    def expand(j):
        wk = jnp.pad(w1r[0, j], ((0, 0), (0, HEAD_DIM), (0, hid)))
        wv = jnp.pad(w1r[1, j], ((0, 0), (HEAD_DIM, 0), (hid, 0)))
        return (wk + wv).reshape(CMP_STRIDE * 2 * HEAD_DIM, 2 * hid).astype(BF16)

    pe_rows = [jnp.concatenate([per_r[0, j], per_r[1, j]], axis=-1).reshape(1, -1) for j in range(per)]
    z = jnp.zeros((hid, HEAD_DIM), F32)
    w2f = w2.astype(F32)
    w2x = jnp.concatenate([jnp.concatenate([w2f[0], z], axis=1),
                           jnp.concatenate([z, w2f[1]], axis=1)], axis=0).astype(BF16)
    w_top, w_bot = expand(0), expand(1)
    blk = pl.BlockSpec((None, nr, r.shape[-1]), lambda i: (i, 0, 0))
    out = pl.BlockSpec((None, nr, BRANCH_WIDTH), lambda i: (i, 0, 0))
    return pl.pallas_call(
        _compress_kernel,
        grid=(b,),
        in_specs=[blk, _const_spec(pe_rows[0].shape), _const_spec(pe_rows[1].shape),
                  _const_spec(w_top.shape), _const_spec(w_bot.shape), _const_spec(w2x.shape)],
        out_specs=[out, out],
        out_shape=[jax.ShapeDtypeStruct((b, nr, BRANCH_WIDTH), BF16)] * 2,
        compiler_params=pltpu.CompilerParams(dimension_semantics=("arbitrary",)),
        name="nsa_compress",
    )(r, pe_rows[0], pe_rows[1], w_top, w_bot, w2x)


def _nsa_kernel(dq_ref, dqr_ref, dg_ref, kc_ref, vc_ref, ks_ref, vs_ref, kw_ref, vw_ref,
                ovt_ref, exp_ref, gsel_ref, wbias_ref, o_ref, *scratch, tq, n_cmp, n_slc, n_sel):
    qi = pl.program_id(1)
    t0 = qi * tq
    nr = kc_ref.shape[0]
    low = _low_half()
    q_cols = lambda h: slice((h // 2) * LANES, (h // 2 + 1) * LANES)
    kv_col = lambda h: (h % 2) * LANES

    dq = dq_ref[...]
    tpos = t0 + lax.broadcasted_iota(jnp.int32, (tq, nr), 0)
    ci = lax.broadcasted_iota(jnp.int32, (tq, nr), 1)
    cmask = (ci * CMP_STRIDE + (CMP_BLOCK - 1) <= tpos) & (ci < n_cmp)
    p_sum = jnp.zeros((tq, nr), F32)
    o_heads = []
    for h in range(N_HEADS):
        s = jnp.where(cmask, _dot_nt(dq[:, q_cols(h)], kc_ref[:, kv_col(h):kv_col(h) + LANES]), NEG_INF)
        m = jnp.max(s, axis=-1, keepdims=True)
        e = jnp.where(cmask, jnp.exp(s - m), 0.0)
        p = e / jnp.maximum(jnp.sum(e, axis=-1, keepdims=True), 1e-30)
        p_sum = p_sum + p
        o_heads.append(_dot(p.astype(BF16), vc_ref[:, kv_col(h):kv_col(h) + LANES]))
    o_cmp = jnp.concatenate([jnp.where(low, o_heads[0], o_heads[1]), jnp.where(low, o_heads[2], o_heads[3])],
                            axis=1)

    p_hi, p_lo = _split_hi_lo(p_sum)
    ovt = ovt_ref[...]
    imp_t = _dot_nt(ovt, p_hi) + _dot_nt(ovt, p_lo)
    blk = lax.broadcasted_iota(jnp.int32, (n_slc, tq), 0)
    qblk = lax.shift_right_arithmetic(t0 + lax.broadcasted_iota(jnp.int32, (n_slc, tq), 1),
                                      int(math.log2(SLC_BLOCK)))
    forced = jnp.where(blk == 0, 1, jnp.where(blk == qblk, 1, jnp.where(blk == qblk - 1, 1, 0)))
    score = jnp.where(blk <= qblk, jnp.where(forced > 0, FORCE_SCORE, imp_t), NEG_INF)
    rank = jnp.zeros((n_slc, tq), F32)
    for i in range(n_slc):
        si = score[i:i + 1, :]
        tie = jnp.where(blk > i, 1.0, 0.0)
        rank = rank + jnp.where(si > score, 1.0, jnp.where(si == score, tie, 0.0))
    sel_bias_t = jnp.where(rank < n_sel, jnp.where(score > 0.5 * NEG_INF, 0.0, NEG_INF), NEG_INF)
    pad_rows = exp_ref.shape[1] - n_slc
    sel_bias = jnp.concatenate([sel_bias_t, jnp.full((pad_rows, tq), NEG_INF, F32)], axis=0).T.astype(BF16)

    dqr = dqr_ref[...]
    pairs = lambda acc: jnp.concatenate([_normalise_pair(acc[0], acc[1]), _normalise_pair(acc[2], acc[3])], axis=1)
    chains = [(dqr[:, q_cols(h)], (ks_ref, kv_col(h)), (vs_ref, kv_col(h))) for h in range(N_HEADS)]
    o_slc = pairs(_strip_attention(chains, tq, 0, qi + 1, lambda kb: _dot(sel_bias, exp_ref[kb]),
                                   _causal_bias(tq), scratch))
    chains = [(dqr[:, q_cols(h)], (kw_ref, kv_col(h)), (vw_ref, kv_col(h))) for h in range(N_HEADS)]
    win_lo = jnp.maximum(qi - (wbias_ref.shape[0] - 1), 0)
    o_win = pairs(_strip_attention(chains, tq, win_lo, qi + 1, lambda kb: wbias_ref[qi - kb], None, scratch))

    gates = _dot_hilo(jax.nn.sigmoid(dg_ref[...]), gsel_ref[...])
    w = BRANCH_WIDTH
    o_ref[...] = gates[:, :w] * o_cmp + gates[:, w:2 * w] * o_slc + gates[:, 2 * w:] * o_win


def _nsa_constants(tq, s_len):
    n_cmp = (s_len - CMP_BLOCK) // CMP_STRIDE + 1
    n_slc = s_len // SLC_BLOCK
    nr = s_len // CMP_STRIDE
    c0 = np.arange(n_cmp)[:, None] * CMP_STRIDE
    s0 = np.arange(n_slc)[None, :] * SLC_BLOCK
    overlap = np.clip(np.minimum(c0 + CMP_BLOCK, s0 + SLC_BLOCK) - np.maximum(c0, s0), 0, None) / CMP_STRIDE
    ovt = np.zeros((n_slc, nr), np.float32)
    ovt[:, :n_cmp] = overlap.T
    rows = -(-n_slc // 128) * 128
    expand = np.zeros((s_len // tq, rows, tq), np.float32)
    tok = np.arange(s_len)
    expand[tok // tq, tok // SLC_BLOCK, tok % tq] = 1.0
    gsel = np.zeros((128, 3 * BRANCH_WIDTH), np.float32)
    for h in range(N_HEADS):
        for j in range(3):
            gsel[h * 3 + j, j * BRANCH_WIDTH + h * HEAD_DIM:j * BRANCH_WIDTH + (h + 1) * HEAD_DIM] = 1.0
    nwin = -(-WINDOW // tq) + 1
    d = np.arange(nwin)[:, None, None] * tq + np.arange(tq)[None, :, None] - np.arange(tq)[None, None, :]
    wbias = np.where((d >= 0) & (d < WINDOW), 0.0, NEG_INF).astype(np.float32)
    return n_cmp, n_slc, ovt, expand, gsel, wbias


def _nsa_attention(dq, dqr, dg, k_cmp, v_cmp, ks, vs, kw, vw):
    b, s_len, w = dq.shape
    tq = ATT_TILE
    n_cmp, n_slc, ovt, expand, gsel, wbias = _nsa_constants(tq, s_len)
    q_spec, kv_spec, _ = _att_specs(tq, s_len, w, w)
    nr = k_cmp.shape[1]
    cmp_spec = pl.BlockSpec((None, nr, w), lambda bi, i: (bi, 0, 0))
    g_spec = pl.BlockSpec((None, tq, dg.shape[-1]), lambda bi, i: (bi, i, 0))
    consts = [jnp.asarray(ovt, BF16), jnp.asarray(expand, BF16), jnp.asarray(gsel, BF16), jnp.asarray(wbias)]
    return pl.pallas_call(
        functools.partial(_nsa_kernel, tq=tq, n_cmp=n_cmp, n_slc=n_slc, n_sel=min(N_SELECT, n_slc)),
        grid=(b, s_len // tq),
        in_specs=[q_spec, q_spec, g_spec, cmp_spec, cmp_spec, kv_spec, kv_spec, kv_spec, kv_spec]
        + [_const_spec(c.shape) for c in consts],
        out_specs=q_spec,
        out_shape=jax.ShapeDtypeStruct((b, s_len, w), F32),
        scratch_shapes=_strip_scratch(N_HEADS, tq, s_len),
        compiler_params=pltpu.CompilerParams(dimension_semantics=("arbitrary", "arbitrary"),
                                             vmem_limit_bytes=BIG_VMEM_LIMIT),
        name="nsa_attention",
    )(dq, dqr, dg, k_cmp, v_cmp, ks, vs, kw, vw, *consts)


def _mem_kv_kernel(m_ref, g_ref, w_ref, k_ref, v_ref):
    kv = _dot(_rms(m_ref[...], g_ref[...]).astype(BF16), w_ref[...])
    k_ref[...] = kv[:, :BRANCH_WIDTH].astype(BF16)
    v_ref[...] = kv[:, BRANCH_WIDTH:].astype(BF16)


def _mem_kv(mem2, g, w_kv):
    rows, d = mem2.shape
    tm = ROW_TILE
    out = pl.BlockSpec((tm, BRANCH_WIDTH), lambda i: (i, 0))
    return pl.pallas_call(
        _mem_kv_kernel,
        grid=(rows // tm,),
        in_specs=[pl.BlockSpec((tm, d), lambda i: (i, 0)), _const_spec((1, d)), _const_spec(w_kv.shape)],
        out_specs=[out, out],
        out_shape=[jax.ShapeDtypeStruct((rows, BRANCH_WIDTH), BF16)] * 2,
        compiler_params=pltpu.CompilerParams(dimension_semantics=("arbitrary",)),
        name="mem_kv",
    )(mem2, g.astype(F32).reshape(1, d), w_kv.astype(BF16))


def _mem_attn_kernel(q_ref, k_ref, v_ref, o_ref):
    q = q_ref[...]
    k = k_ref[...]
    v = v_ref[...]
    o = jnp.zeros(o_ref.shape, F32)
    for h in range(N_HEADS):
        s = _dot_nt(q * _head_mask(h, BF16), k)
        e = jnp.exp(s - jnp.max(s, axis=-1, keepdims=True))
        oh = _dot(e.astype(BF16), v) / jnp.sum(e, axis=-1, keepdims=True)
        o = jnp.where(_head_mask(h, F32) > 0.5, oh, o)
    o_ref[...] = o


def _mem_attention(q, k, v):
    b, s_len, w = q.shape
    tq = min(MEM_ATT_TILE, s_len)
    q_spec = pl.BlockSpec((None, tq, w), lambda bi, i: (bi, i, 0))
    kv_spec = pl.BlockSpec((None, k.shape[1], w), lambda bi, i: (bi, 0, 0))
    return pl.pallas_call(
        _mem_attn_kernel,
        grid=(b, s_len // tq),
        in_specs=[q_spec, kv_spec, kv_spec],
        out_specs=q_spec,
        out_shape=jax.ShapeDtypeStruct((b, s_len, w), F32),
        compiler_params=pltpu.CompilerParams(dimension_semantics=("arbitrary", "arbitrary")),
        name="mem_attention",
    )(q, k, v)


def _merge_kernel(x_ref, g_ref, oa_ref, ob_ref, oc_ref, od_ref, oe_ref, zs_ref, wm_ref, bm_ref, wb_ref,
                  wo_ref, fg_ref, o_ref, *, final_norm):
    x = x_ref[...]
    d = x.shape[-1]
    w = BRANCH_WIDTH
    h = _rms(x, g_ref[...]).astype(BF16)
    mixed = jnp.zeros(x.shape, F32)
    for n, br_ref in enumerate((oa_ref, ob_ref, oc_ref, od_ref, oe_ref)):
        br = (br_ref[...] * zs_ref[:, n * w:(n + 1) * w]).astype(BF16)
        y = _dot(br, wb_ref[n].astype(BF16))
        gate = jax.nn.sigmoid(_dot(h, wm_ref[:, n * d:(n + 1) * d].astype(BF16)) + bm_ref[:, n * d:(n + 1) * d])
        mixed = mixed + gate * y
    out = x + _dot(mixed.astype(BF16), wo_ref[...].astype(BF16))
    if final_norm:
        out = _rms(out, fg_ref[...])
    o_ref[...] = out


def _merge(x2, g, branches, zs, w_merge, b_merge, w_branch, w_out, layer, final_g, final_norm):
    t, d = x2.shape
    tm = MERGE_ROW_TILE
    row = lambda wd: pl.BlockSpec((tm, wd), lambda i: (i, 0))

    def of_layer(a):
        nd = a.ndim - 1
        return pl.BlockSpec((None,) + a.shape[1:], lambda i: (layer,) + (0,) * nd, pipeline_mode=pl.Buffered(1))

    b_merge = b_merge.astype(F32).reshape(b_merge.shape[0], 1, -1)
    return pl.pallas_call(
        functools.partial(_merge_kernel, final_norm=final_norm),
        grid=(t // tm,),
        in_specs=[row(d), _const_spec((1, d))] + [row(BRANCH_WIDTH)] * N_BRANCHES + [row(zs.shape[1])]
        + [of_layer(w_merge), of_layer(b_merge), of_layer(w_branch), of_layer(w_out), _const_spec((1, d))],
        out_specs=row(d),
        out_shape=jax.ShapeDtypeStruct((t, d), F32),
        compiler_params=pltpu.CompilerParams(dimension_semantics=("arbitrary",),
                                             vmem_limit_bytes=BIG_VMEM_LIMIT),
        name="merge",
    )(x2, g.astype(F32).reshape(1, d), *branches, zs, w_merge.astype(F32), b_merge, w_branch.astype(F32),
      w_out.astype(F32), final_g.astype(F32).reshape(1, d))


def kernel(x, mem, norm_g, w_in, diff_lambda, diff_subln_g, s5_lambda_re, s5_lambda_im, s5_log_dt,
           s5_b_re, s5_b_im, s5_c_re, s5_c_im, s5_d, w_glu, b_glu, nsa_pe, nsa_w1, nsa_w2, mem_norm_g,
           w_mem_kv, w_merge, b_merge, w_branch, w_out, final_g):
    bsz, s_len, d = x.shape
    depth = w_in.shape[0]
    t = bsz * s_len
    w = BRANCH_WIDTH
    tables = _rope_tables(s_len)
    x2 = x.astype(F32).reshape(t, d)
    mem2 = mem.astype(F32).reshape(-1, d)
    for l in range(depth):
        proj = dict(zip([n for n, _, _ in _IN_OUTS],
                        _in_proj(x2, norm_g[l].astype(F32), _in_weights(w_in[l].astype(F32)), tables, s_len)))
        seq = lambda name: proj[name].reshape(bsz, s_len, -1)

        dl = diff_lambda[l].astype(F32)
        lam_init = 0.8 - 0.6 * math.exp(-0.3 * l)
        lam = jnp.exp(jnp.sum(dl[0] * dl[1])) - jnp.exp(jnp.sum(dl[2] * dl[3])) + lam_init
        o_a = _diff_attention(seq("qa"), seq("ka"), seq("va"), lam, diff_subln_g[l], lam_init)

        o_b = _dilated_attention(seq("qb"), seq("kb"), seq("vb"))

        s5p = _s5_params(s5_lambda_re[l], s5_lambda_im[l], s5_log_dt[l], s5_b_re[l], s5_b_im[l],
                         s5_c_re[l], s5_c_im[l])
        o_c = _s5_branch(seq("cu"), s5p, s5_d[l], w_glu[l], b_glu[l])

        k_cmp, v_cmp = _compress(seq("kvc"), nsa_pe[l], nsa_w1[l], nsa_w2[l])
        o_d = _nsa_attention(seq("dq"), seq("dqr"), seq("dg"), k_cmp, v_cmp,
                             seq("ks"), seq("vs"), seq("kw"), seq("vw"))

        k_mem, v_mem = _mem_kv(mem2, mem_norm_g[l], w_mem_kv[l])
        o_e = _mem_attention(seq("eq"), k_mem.reshape(bsz, -1, w), v_mem.reshape(bsz, -1, w))

        branches = [o.reshape(t, w) for o in (o_a, o_b, o_c, o_d, o_e)]
        x2 = _merge(x2, norm_g[l], branches, proj["zs"], w_merge, b_merge, w_branch, w_out, l,
                    final_g, final_norm=(l == depth - 1))
    return x2.reshape(bsz, s_len, d).astype(x.dtype)
```

```python
import functools
import math

import numpy as np
import jax
import jax.numpy as jnp
from jax import lax
from jax.experimental import pallas as pl
from jax.experimental.pallas import tpu as pltpu

F32 = jnp.float32
BF16 = jnp.bfloat16

HEAD_DIM = 64
BRANCH_WIDTH = 256
N_HEADS = 4
N_BRANCHES = 5
DIFF_QK_DIM = 32
DIL_PATTERNS = ((128, 1), (512, 4), (2048, 16))
S5_GROUP = 16
S5_GROUPS = 16
S5_STATE = 64
CMP_BLOCK = 32
CMP_STRIDE = 16
SLC_BLOCK = 64
N_SELECT = 16
WINDOW = 512
ROPE_THETA = 10000.0
RMS_EPS = 1e-6
NEG_INF = -1e30
FORCE_SCORE = 1e9

V7X_VMEM_BYTES = 64 * 1024 * 1024
BIG_VMEM_LIMIT = V7X_VMEM_BYTES - 8 * 1024 * 1024

LANES = 128
ATT_TILE = 512
ROW_TILE = 512
MERGE_ROW_TILE = 256
MEM_ATT_TILE = 2048
S5_CHUNK = 128

_NT = (((1,), (1,)), ((), ()))


def _rms(x, g):
    return x * lax.rsqrt(jnp.mean(x * x, axis=-1, keepdims=True) + RMS_EPS) * g


def _dot(a, b):
    return jnp.dot(a, b, preferred_element_type=F32)


def _dot_nt(a, b):
    return lax.dot_general(a, b, _NT, preferred_element_type=F32)


def _split_hi_lo(x):
    hi = x.astype(BF16)
    lo = (x - hi.astype(F32)).astype(BF16)
    return hi, lo


def _dot_hilo(x, w):
    hi, lo = _split_hi_lo(x)
    return _dot(hi, w) + _dot(lo, w)


def _const_spec(shape):
    n = len(shape)
    return pl.BlockSpec(shape, lambda *_: (0,) * n, pipeline_mode=pl.Buffered(1))


def _head_mask(h, dtype):
    lane = lax.broadcasted_iota(jnp.int32, (1, BRANCH_WIDTH), 1)
    return jnp.where((lane >= h * HEAD_DIM) & (lane < (h + 1) * HEAD_DIM), 1.0, 0.0).astype(dtype)


def _low_half():
    return lax.broadcasted_iota(jnp.int32, (1, LANES), 1) < HEAD_DIM


def _swap_halves(x):
    return pltpu.roll(x, HEAD_DIM, 1)


def _augment_heads(v):
    low = _low_half()
    parts = []
    for j in range(v.shape[1] // LANES):
        pair = v[:, j * LANES:(j + 1) * LANES]
        parts += [jnp.where(low, pair, 1.0), jnp.where(low, 1.0, pair)]
    return jnp.concatenate(parts, axis=1)


def _shared_kv_variants(kv):
    low = _low_half()
    sw = _swap_halves(kv)
    k2 = jnp.concatenate([jnp.where(low, kv, 0.0), jnp.where(low, 0.0, sw)], axis=1)
    v2 = jnp.concatenate([jnp.where(low, sw, 1.0), jnp.where(low, 1.0, kv)], axis=1)
    return k2, v2


def _normalise_pair(acc_even, acc_odd):
    return jnp.where(_low_half(), acc_even / _swap_halves(acc_even), acc_odd / _swap_halves(acc_odd))


_IN_MAIN_WIDTH = 3072
_IN_SRC = {"qa": (True, 0), "ka": (True, 256), "va": (True, 512), "az": (True, 768),
           "qb": (True, 1024), "kb": (True, 1280), "vb": (True, 1536), "bz": (True, 1792),
           "cu": (True, 2048), "cz": (True, 2304), "dq": (True, 2560), "kvcs": (True, 2816),
           "kvwg": (False, 0), "dz": (False, 256), "eq": (False, 512), "ez": (False, 768)}
_IN_TAIL_WIDTH = 1024
_IN_SCALED = ("qb", "dq", "eq")
_IN_OUTS = (("qa", 256, BF16), ("ka", 256, BF16), ("va", 512, BF16),
            ("qb", 256, BF16), ("kb", 256, BF16), ("vb", 512, BF16),
            ("cu", 256, F32), ("dq", 256, BF16), ("dqr", 256, BF16), ("kvc", 128, F32),
            ("ks", 256, BF16), ("vs", 256, BF16), ("kw", 256, BF16), ("vw", 256, BF16),
            ("dg", 128, F32), ("eq", 256, BF16), ("zs", 1280, F32))


def _rotate_half(y, group):
    half = group // 2
    lane = lax.broadcasted_iota(jnp.int32, (1, LANES), 1)
    first = (lane & (group - 1)) < half
    parts = []
    for j in range(y.shape[1] // LANES):
        v = y[:, j * LANES:(j + 1) * LANES]
        parts.append(jnp.where(first, pltpu.roll(v, LANES - half, 1), pltpu.roll(v, half, 1)))
    return jnp.concatenate(parts, axis=1)


def _in_proj_kernel(x_ref, g_ref, wm_ref, wt_ref, cosa_ref, sina_ref, cosb_ref, sinb_ref, cosk_ref, sink_ref,
                    *out_refs):
    out = dict(zip([n for n, _, _ in _IN_OUTS], out_refs))
    h = _rms(x_ref[...], g_ref[...]).astype(BF16)

    def proj(name):
        main, off = _IN_SRC[name]
        y = _dot(h, (wm_ref if main else wt_ref)[:, off:off + BRANCH_WIDTH].astype(BF16))
        return y * HEAD_DIM ** -0.5 if name in _IN_SCALED else y

    def rope(y, cos_ref, sin_ref, group):
        return y * cos_ref[...] + _rotate_half(y, group) * sin_ref[...]

    def put(name, y):
        out[name][...] = y.astype(out[name].dtype)

    put("qa", rope(proj("qa"), cosa_ref, sina_ref, DIFF_QK_DIM))
    put("ka", rope(proj("ka"), cosa_ref, sina_ref, DIFF_QK_DIM))
    put("va", _augment_heads(proj("va")))
    put("qb", rope(proj("qb"), cosb_ref, sinb_ref, HEAD_DIM))
    put("kb", rope(proj("kb"), cosb_ref, sinb_ref, HEAD_DIM))
    put("vb", _augment_heads(proj("vb")))
    put("cu", proj("cu"))
    dq = proj("dq")
    put("dq", dq)
    put("dqr", rope(dq, cosb_ref, sinb_ref, HEAD_DIM))
    kvcs = proj("kvcs")
    kvwg = proj("kvwg")
    put("kvc", kvcs[:, :LANES])
    put("dg", kvwg[:, LANES:])
    for kv, k_name, v_name in ((kvcs[:, LANES:], "ks", "vs"), (kvwg[:, :LANES], "kw", "vw")):
        k2, v2 = _shared_kv_variants(rope(kv, cosk_ref, sink_ref, HEAD_DIM))
        put(k_name, k2)
        put(v_name, v2)
    put("eq", proj("eq"))
    for n, name in enumerate(("az", "bz", "cz", "dz", "ez")):
        z = proj(name)
        out["zs"][:, n * BRANCH_WIDTH:(n + 1) * BRANCH_WIDTH] = z * jax.nn.sigmoid(z)


def _in_tail(w):
    used = LANES + 3 * N_HEADS
    tail = w[:, _IN_MAIN_WIDTH:]
    return jnp.concatenate([jnp.pad(tail[:, :used], ((0, 0), (0, BRANCH_WIDTH - used))), tail[:, used:]], axis=1)


def _rope_tables(s_len):
    def table(group, width):
        half = group // 2
        inv_freq = ROPE_THETA ** (-jnp.arange(half, dtype=F32) / half)
        ang = jnp.arange(s_len, dtype=F32)[:, None] * inv_freq[None, :]
        cos = jnp.tile(jnp.cos(ang), (1, width // half))
        sin = jnp.tile(jnp.concatenate([-jnp.sin(ang), jnp.sin(ang)], axis=1), (1, width // group))
        return cos, sin
    cos_a, sin_a = table(DIFF_QK_DIM, BRANCH_WIDTH)
    cos_b, sin_b = table(HEAD_DIM, BRANCH_WIDTH)
    cos_k = jnp.concatenate([cos_b[:, :HEAD_DIM], jnp.ones((s_len, HEAD_DIM), F32)], axis=1)
    sin_k = jnp.concatenate([sin_b[:, :HEAD_DIM], jnp.zeros((s_len, HEAD_DIM), F32)], axis=1)
    return cos_a, sin_a, cos_b, sin_b, cos_k, sin_k


def _in_proj(x2, g, w_in, layer, tables, s_len):
    t, d = x2.shape
    tm = ROW_TILE
    nsb = s_len // tm
    w_tail = _in_tail(w_in[layer])
    assert w_tail.shape[1] == _IN_TAIL_WIDTH
    row = lambda w: pl.BlockSpec((tm, w), lambda i: (i, 0))
    tab = lambda a: pl.BlockSpec((tm, a.shape[1]), lambda i: (i % nsb, 0))
    w_main = pl.BlockSpec((None, d, _IN_MAIN_WIDTH), lambda i: (layer, 0, 0), pipeline_mode=pl.Buffered(1))
    return pl.pallas_call(
        _in_proj_kernel,
        grid=(t // tm,),
        in_specs=[row(d), _const_spec((1, d)), w_main, _const_spec(w_tail.shape)] + [tab(a) for a in tables],
        out_specs=[row(w) for _, w, _ in _IN_OUTS],
        out_shape=[jax.ShapeDtypeStruct((t, w), dt) for _, w, dt in _IN_OUTS],
        compiler_params=pltpu.CompilerParams(dimension_semantics=("arbitrary",),
                                             vmem_limit_bytes=BIG_VMEM_LIMIT),
        name="in_proj",
    )(x2, g.reshape(1, d), w_in, w_tail, *tables)


def _lane_fold(x, op):
    parts = [x[:, j * LANES:(j + 1) * LANES] for j in range(x.shape[1] // LANES)]
    return functools.reduce(op, parts)


def _strip_attention(chains, tq, kb_lo, kb_hi, bias_fn, last_bias, scratch, exp_scale=1.0):
    s_ref, m_ref, acc_ref = scratch
    n = len(chains)
    reps = tq // LANES
    c1 = exp_scale * math.log2(math.e)

    def tile(cache, ref, col, width, kb, ntiles=1):
        key = (id(ref), col, width)
        if key not in cache:
            cache[key] = ref[pl.ds(pl.multiple_of(kb * tq, tq), ntiles * tq), col:col + width]
        return cache[key]

    def scores(kb, extra):
        bias = bias_fn(kb) if bias_fn is not None else None
        if extra is not None:
            bias = extra if bias is None else bias + extra
        cache, out = {}, []
        for q, (k_ref, k_col), _ in chains:
            s = _dot_nt(q, tile(cache, k_ref, k_col, q.shape[1], kb)) * c1
            out.append(s if bias is None else s + bias)
        return out

    def tile_pairs(lo, hi, step):
        def two(j, carry):
            step(lo + 2 * j, 2)
            return carry
        cnt = hi - lo
        lax.fori_loop(0, lax.shift_right_arithmetic(cnt, 1), two, 0)

        @pl.when((cnt & 1) == 1)
        def _():
            step(hi - 1, 1)

    m_ref[0:n] = jnp.full((n,) + m_ref.shape[1:], NEG_INF, F32)

    def pass1(kb, ntiles):
        tiles = [scores(kb + j, None) for j in range(ntiles)]
        for i in range(n):
            m = m_ref[i]
            for j in range(ntiles):
                s_ref[i, kb + j] = tiles[j][i]
                m = jnp.maximum(m, _lane_fold(tiles[j][i], jnp.maximum))
            m_ref[i] = m

    tile_pairs(kb_lo, kb_hi - 1, pass1)
    for i, s in enumerate(scores(kb_hi - 1, last_bias)):
        s_ref[i, kb_hi - 1] = s
        m = jnp.max(jnp.maximum(m_ref[i], _lane_fold(s, jnp.maximum)), axis=-1, keepdims=True)
        m_ref[i] = jnp.broadcast_to(m, m_ref.shape[1:])
        acc_ref[i] = jnp.zeros(acc_ref.shape[1:], F32)

    def pass2(kb, ntiles):
        cache = {}
        for i, (_, _, (v_ref, v_col)) in enumerate(chains):
            m = jnp.concatenate([m_ref[i]] * reps, axis=1)
            p = [jnp.exp2(s_ref[i, kb + j] - m).astype(BF16) for j in range(ntiles)]
            p = p[0] if ntiles == 1 else jnp.concatenate(p, axis=1)
            acc_ref[i] += _dot(p, tile(cache, v_ref, v_col, LANES, kb, ntiles))

    tile_pairs(kb_lo, kb_hi, pass2)
    return [acc_ref[i] for i in range(n)]


def _strip_scratch(n, tq, s_len):
    return [pltpu.VMEM((n, s_len // tq, tq, tq), F32), pltpu.VMEM((n, tq, LANES), F32),
            pltpu.VMEM((n, tq, LANES), F32)]


def _causal_bias(tq):
    r = lax.broadcasted_iota(jnp.int32, (tq, tq), 0)
    c = lax.broadcasted_iota(jnp.int32, (tq, tq), 1)
    return jnp.where(c <= r, 0.0, NEG_INF).astype(F32)


def _att_specs(tq, s_len, k_width, v_width):
    q_spec = pl.BlockSpec((None, tq, BRANCH_WIDTH), lambda b, i: (b, i, 0))
    kv = lambda w: pl.BlockSpec((None, s_len, w), lambda b, i: (b, 0, 0))
    return q_spec, kv(k_width), kv(v_width)


def _diff_kernel(lam_ref, q_ref, k_ref, v_ref, g_ref, hm_ref, o_ref, *scratch, tq, out_scale):
    qi = pl.program_id(1)
    q = q_ref[...]
    lane = lax.broadcasted_iota(jnp.int32, (1, BRANCH_WIDTH), 1)
    lam = lam_ref[0]
    chains = []
    for hc in range(2 * N_HEADS):
        lo = hc * DIFF_QK_DIM
        cmask = jnp.where((lane >= lo) & (lane < lo + DIFF_QK_DIM), 1.0, 0.0).astype(BF16)
        chains.append((q * cmask, (k_ref, 0), (v_ref, (hc // 2) * LANES)))
    acc = _strip_attention(chains, tq, 0, qi + 1, None, _causal_bias(tq), scratch,
                           exp_scale=DIFF_QK_DIM ** -0.5)
    halves = []
    for pair in range(N_HEADS // 2):
        even, odd = 4 * pair, 4 * pair + 2
        halves.append(_normalise_pair(acc[even], acc[odd]) - lam * _normalise_pair(acc[even + 1], acc[odd + 1]))
    o = jnp.concatenate(halves, axis=1)
    ms = _dot_hilo(o * o, hm_ref[...])
    o_ref[...] = o * lax.rsqrt(ms + RMS_EPS) * g_ref[...] * out_scale


def _diff_attention(q, k, v_aug, lam, subln_g, lam_init):
    b, s_len, w = q.shape
    tq = ATT_TILE
    q_spec, k_spec, v_spec = _att_specs(tq, s_len, w, v_aug.shape[-1])
    head = np.arange(w) // HEAD_DIM
    hm = jnp.asarray((head[:, None] == head[None, :]) / HEAD_DIM, dtype=BF16)
    g = jnp.tile(subln_g.astype(F32), N_HEADS).reshape(1, w)
    return pl.pallas_call(
        functools.partial(_diff_kernel, tq=tq, out_scale=1.0 - lam_init),
        grid=(b, s_len // tq),
        in_specs=[pl.BlockSpec(memory_space=pltpu.SMEM), q_spec, k_spec, v_spec,
                  _const_spec((1, w)), _const_spec((w, w))],
        out_specs=q_spec,
        out_shape=jax.ShapeDtypeStruct((b, s_len, w), F32),
        scratch_shapes=_strip_scratch(2 * N_HEADS, tq, s_len),
        compiler_params=pltpu.CompilerParams(dimension_semantics=("arbitrary", "arbitrary"),
                                             vmem_limit_bytes=BIG_VMEM_LIMIT),
        name="diff_attention",
    )(lam.reshape(1), q, k, v_aug, g, hm)


def _dil_kernel(q_ref, k_ref, v_ref, bias_ref, o_ref, *scratch, tq):
    qi = pl.program_id(1)
    q = q_ref[...]
    chains = [(q * _head_mask(h, BF16), (k_ref, 0), (v_ref, h * LANES)) for h in range(N_HEADS)]
    acc = _strip_attention(chains, tq, 0, qi + 1, lambda kb: bias_ref[qi - kb], None, scratch)
    o_ref[...] = jnp.concatenate([_normalise_pair(acc[0], acc[1]), _normalise_pair(acc[2], acc[3])], axis=1)


def _dilated_bias(tq, s_len):
    nq = s_len // tq
    d = (np.arange(nq)[:, None, None] * tq + np.arange(tq)[None, :, None] - np.arange(tq)[None, None, :])
    count = np.zeros(d.shape, np.float64)
    for window, dil in DIL_PATTERNS:
        count += (d >= 0) & (d <= window) & (d % dil == 0)
    return np.where(count > 0, np.log2(np.maximum(count, 1.0)), NEG_INF).astype(np.float32)


def _dilated_attention(q, k, v_aug):
    b, s_len, w = q.shape
    tq = ATT_TILE
    q_spec, k_spec, v_spec = _att_specs(tq, s_len, w, v_aug.shape[-1])
    bias = jnp.asarray(_dilated_bias(tq, s_len))
    return pl.pallas_call(
        functools.partial(_dil_kernel, tq=tq),
        grid=(b, s_len // tq),
        in_specs=[q_spec, k_spec, v_spec, _const_spec(bias.shape)],
        out_specs=q_spec,
        out_shape=jax.ShapeDtypeStruct((b, s_len, w), F32),
        scratch_shapes=_strip_scratch(N_HEADS, tq, s_len),
        compiler_params=pltpu.CompilerParams(dimension_semantics=("arbitrary", "arbitrary"),
                                             vmem_limit_bytes=BIG_VMEM_LIMIT),
        name="dilated_attention",
    )(q, k, v_aug, bias)


def _s5_kernel(u_ref, bm_ref, cm_ref, are_ref, aim_ref, d_ref, wg_ref, bg_ref, o_ref, st_ref, *xs_refs,
               ts, nb):
    n = S5_GROUPS * S5_STATE
    nc = n // LANES

    @pl.when(pl.program_id(0) == 0)
    def _():
        st_ref[...] = jnp.zeros(st_ref.shape, F32)

    u = u_ref[...].reshape(nb * ts, u_ref.shape[-1])
    bu = _dot(u.astype(BF16), bm_ref[...])
    for c, x_ref in enumerate(xs_refs):
        for b in range(nb):
            x_ref[pl.ds(b, ts, stride=nb), :] = bu[b * ts:(b + 1) * ts, c * LANES:(c + 1) * LANES]
    a_re = jnp.broadcast_to(are_ref[...], (nb, n))
    a_im = jnp.broadcast_to(aim_ref[...], (nb, n))

    def step(t, carry):
        x_re, x_im = carry
        rows = pl.ds(pl.multiple_of(t * nb, nb), nb)
        bu_re = jnp.concatenate([x_ref[rows, :] for x_ref in xs_refs[:nc]], axis=1)
        bu_im = jnp.concatenate([x_ref[rows, :] for x_ref in xs_refs[nc:]], axis=1)
        n_re = a_re * x_re - a_im * x_im + bu_re
        n_im = a_re * x_im + a_im * x_re + bu_im
        for c in range(nc):
            xs_refs[c][rows, :] = n_re[:, c * LANES:(c + 1) * LANES]
            xs_refs[nc + c][rows, :] = n_im[:, c * LANES:(c + 1) * LANES]
        return n_re, n_im

    x_re, x_im = lax.fori_loop(0, ts, step, (st_ref[:, 0:n], st_ref[:, n:2 * n]))
    st_ref[:, 0:n] = x_re
    st_ref[:, n:2 * n] = x_im

    xs = jnp.concatenate(
        [jnp.concatenate([x_ref[pl.ds(b, ts, stride=nb), :].astype(BF16) for x_ref in xs_refs], axis=1)
         for b in range(nb)], axis=0)
    y = _dot(xs, cm_ref[...]) + d_ref[...] * u
    t = _dot(jax.nn.gelu(y).astype(BF16), wg_ref[...]) + bg_ref[...]
    o_ref[...] = (t[:, :BRANCH_WIDTH] * jax.nn.sigmoid(t[:, BRANCH_WIDTH:])).reshape(o_ref.shape)


def _s5_params(lam_re, lam_im, log_dt, b_re, b_im, c_re, c_im):
    g, n, p = S5_GROUPS, S5_STATE, S5_GROUP
    lr, li = lam_re.astype(F32), lam_im.astype(F32)
    dt = jnp.exp(log_dt.astype(F32))[:, None]
    mag = jnp.exp(lr * dt)
    a_re, a_im = mag * jnp.cos(li * dt), mag * jnp.sin(li * dt)
    den = lr * lr + li * li
    n_re, n_im = a_re - 1.0, a_im
    z_re = (n_re * lr + n_im * li) / den
    z_im = (n_im * lr - n_re * li) / den
    br, bi = b_re.astype(F32), b_im.astype(F32)
    bb_re = z_re[..., None] * br - z_im[..., None] * bi
    bb_im = z_re[..., None] * bi + z_im[..., None] * br
    eye = jnp.eye(g, dtype=F32)
    blockdiag_in = lambda t: jnp.einsum("gnp,gh->gphn", t, eye).reshape(g * p, g * n)
    blockdiag_out = lambda t: jnp.einsum("gpn,gh->gnhp", t, eye).reshape(g * n, g * p)
    bm = jnp.concatenate([blockdiag_in(bb_re), blockdiag_in(bb_im)], axis=1)
    cm = jnp.concatenate([blockdiag_out(c_re.astype(F32)), -blockdiag_out(c_im.astype(F32))], axis=0)
    return bm.astype(BF16), cm.astype(BF16), a_re.reshape(1, g * n), a_im.reshape(1, g * n)


def _s5_branch(u, params, d_skip, w_glu, b_glu):
    nb, s_len, w = u.shape
    assert nb == 8
    ts = S5_CHUNK
    bm, cm, a_re, a_im = params
    n2 = bm.shape[1]
    blk = pl.BlockSpec((nb, ts, w), lambda i: (0, i, 0))
    return pl.pallas_call(
        functools.partial(_s5_kernel, ts=ts, nb=nb),
        grid=(s_len // ts,),
        in_specs=[blk, _const_spec(bm.shape), _const_spec(cm.shape), _const_spec(a_re.shape),
                  _const_spec(a_im.shape), _const_spec((1, w)), _const_spec(w_glu.shape),
                  _const_spec((1, 2 * w))],
        out_specs=blk,
        out_shape=jax.ShapeDtypeStruct((nb, s_len, w), F32),
        scratch_shapes=[pltpu.VMEM((nb, n2), F32)] + [pltpu.VMEM((ts * nb, LANES), F32)] * (n2 // LANES),
        compiler_params=pltpu.CompilerParams(dimension_semantics=("arbitrary",),
                                             vmem_limit_bytes=BIG_VMEM_LIMIT),
        name="s5_scan",
    )(u, bm, cm, a_re, a_im, d_skip.astype(F32).reshape(1, w), w_glu.astype(BF16),
      b_glu.astype(F32).reshape(1, 2 * w))


def _compress_kernel(r_ref, pe_top_ref, pe_bot_ref, w_top_ref, w_bot_ref, w2_ref, k_ref, v_ref):
    r = r_ref[...]
    top = (r + pe_top_ref[...]).astype(BF16)
    nxt = pltpu.roll(r, r.shape[0] - 1, 0)
    bot = (nxt + pe_bot_ref[...]).astype(BF16)
    hid = jax.nn.gelu(_dot(top, w_top_ref[...]) + _dot(bot, w_bot_ref[...]))
    k2, v2 = _shared_kv_variants(_dot(hid.astype(BF16), w2_ref[...]))
    k_ref[...] = k2.astype(BF16)
    v_ref[...] = v2.astype(BF16)


def _compress(kvc, pe, w1, w2):
    b, s_len, _ = kvc.shape
    nr = s_len // CMP_STRIDE
    per = CMP_BLOCK // CMP_STRIDE
    assert per == 2
    hid = w1.shape[-1]
    r = kvc.reshape(b, nr, CMP_STRIDE * 2 * HEAD_DIM)
    w1r = w1.astype(F32).reshape(2, per, CMP_STRIDE, HEAD_DIM, hid)
    per_r = pe.astype(F32).reshape(2, per, CMP_STRIDE, HEAD_DIM)

    def expand(j):
        wk = jnp.pad(w1r[0, j], ((0, 0), (0, HEAD_DIM), (0, hid)))
        wv = jnp.pad(w1r[1, j], ((0, 0), (HEAD_DIM, 0), (hid, 0)))
        return (wk + wv).reshape(CMP_STRIDE * 2 * HEAD_DIM, 2 * hid).astype(BF16)

    pe_rows = [jnp.concatenate([per_r[0, j], per_r[1, j]], axis=-1).reshape(1, -1) for j in range(per)]
    z = jnp.zeros((hid, HEAD_DIM), F32)
    w2f = w2.astype(F32)
    w2x = jnp.concatenate([jnp.concatenate([w2f[0], z], axis=1),
                           jnp.concatenate([z, w2f[1]], axis=1)], axis=0).astype(BF16)
    w_top, w_bot = expand(0), expand(1)
    blk = pl.BlockSpec((None, nr, r.shape[-1]), lambda i: (i, 0, 0))
    out = pl.BlockSpec((None, nr, BRANCH_WIDTH), lambda i: (i, 0, 0))
    return pl.pallas_call(
        _compress_kernel,
        grid=(b,),
        in_specs=[blk, _const_spec(pe_rows[0].shape), _const_spec(pe_rows[1].shape),
                  _const_spec(w_top.shape), _const_spec(w_bot.shape), _const_spec(w2x.shape)],
        out_specs=[out, out],
        out_shape=[jax.ShapeDtypeStruct((b, nr, BRANCH_WIDTH), BF16)] * 2,
        compiler_params=pltpu.CompilerParams(dimension_semantics=("arbitrary",)),
        name="nsa_compress",
    )(r, pe_rows[0], pe_rows[1], w_top, w_bot, w2x)


def _nsa_kernel(dq_ref, dqr_ref, dg_ref, kc_ref, vc_ref, ks_ref, vs_ref, kw_ref, vw_ref,
                ovt_ref, exp_ref, gsel_ref, wbias_ref, o_ref, *scratch, tq, n_cmp, n_slc, n_sel):
    qi = pl.program_id(1)
    t0 = qi * tq
    nr = kc_ref.shape[0]
    low = _low_half()
    q_cols = lambda h: slice((h // 2) * LANES, (h // 2 + 1) * LANES)
    kv_col = lambda h: (h % 2) * LANES

    dq = dq_ref[...]
    tpos = t0 + lax.broadcasted_iota(jnp.int32, (tq, nr), 0)
    ci = lax.broadcasted_iota(jnp.int32, (tq, nr), 1)
    cmask = (ci * CMP_STRIDE + (CMP_BLOCK - 1) <= tpos) & (ci < n_cmp)
    p_sum = jnp.zeros((tq, nr), F32)
    o_heads = []
    for h in range(N_HEADS):
        s = jnp.where(cmask, _dot_nt(dq[:, q_cols(h)], kc_ref[:, kv_col(h):kv_col(h) + LANES]), NEG_INF)
        m = jnp.max(s, axis=-1, keepdims=True)
        e = jnp.where(cmask, jnp.exp(s - m), 0.0)
        p = e / jnp.maximum(jnp.sum(e, axis=-1, keepdims=True), 1e-30)
        p_sum = p_sum + p
        o_heads.append(_dot(p.astype(BF16), vc_ref[:, kv_col(h):kv_col(h) + LANES]))
    o_cmp = jnp.concatenate([jnp.where(low, o_heads[0], o_heads[1]), jnp.where(low, o_heads[2], o_heads[3])],
                            axis=1)

    p_hi, p_lo = _split_hi_lo(p_sum)
    ovt = ovt_ref[...]
    imp_t = _dot_nt(ovt, p_hi) + _dot_nt(ovt, p_lo)
    blk = lax.broadcasted_iota(jnp.int32, (n_slc, tq), 0)
    qblk = lax.shift_right_arithmetic(t0 + lax.broadcasted_iota(jnp.int32, (n_slc, tq), 1),
                                      int(math.log2(SLC_BLOCK)))
    forced = jnp.where(blk == 0, 1, jnp.where(blk == qblk, 1, jnp.where(blk == qblk - 1, 1, 0)))
    score = jnp.where(blk <= qblk, jnp.where(forced > 0, FORCE_SCORE, imp_t), NEG_INF)
    rank = jnp.zeros((n_slc, tq), F32)
    for i in range(n_slc):
        si = score[i:i + 1, :]
        tie = jnp.where(blk > i, 1.0, 0.0)
        rank = rank + jnp.where(si > score, 1.0, jnp.where(si == score, tie, 0.0))
    sel_bias_t = jnp.where(rank < n_sel, jnp.where(score > 0.5 * NEG_INF, 0.0, NEG_INF), NEG_INF)
    pad_rows = exp_ref.shape[1] - n_slc
    sel_bias = jnp.concatenate([sel_bias_t, jnp.full((pad_rows, tq), NEG_INF, F32)], axis=0).T.astype(BF16)

    dqr = dqr_ref[...]
    pairs = lambda acc: jnp.concatenate([_normalise_pair(acc[0], acc[1]), _normalise_pair(acc[2], acc[3])], axis=1)
    chains = [(dqr[:, q_cols(h)], (ks_ref, kv_col(h)), (vs_ref, kv_col(h))) for h in range(N_HEADS)]
    o_slc = pairs(_strip_attention(chains, tq, 0, qi + 1, lambda kb: _dot(sel_bias, exp_ref[kb]),
                                   _causal_bias(tq), scratch))
    chains = [(dqr[:, q_cols(h)], (kw_ref, kv_col(h)), (vw_ref, kv_col(h))) for h in range(N_HEADS)]
    win_lo = jnp.maximum(qi - (wbias_ref.shape[0] - 1), 0)
    o_win = pairs(_strip_attention(chains, tq, win_lo, qi + 1, lambda kb: wbias_ref[qi - kb], None, scratch))

    gates = _dot_hilo(jax.nn.sigmoid(dg_ref[...]), gsel_ref[...])
    w = BRANCH_WIDTH
    o_ref[...] = gates[:, :w] * o_cmp + gates[:, w:2 * w] * o_slc + gates[:, 2 * w:] * o_win


def _nsa_constants(tq, s_len):
    n_cmp = (s_len - CMP_BLOCK) // CMP_STRIDE + 1
    n_slc = s_len // SLC_BLOCK
    nr = s_len // CMP_STRIDE
    c0 = np.arange(n_cmp)[:, None] * CMP_STRIDE
    s0 = np.arange(n_slc)[None, :] * SLC_BLOCK
    overlap = np.clip(np.minimum(c0 + CMP_BLOCK, s0 + SLC_BLOCK) - np.maximum(c0, s0), 0, None) / CMP_STRIDE
    ovt = np.zeros((n_slc, nr), np.float32)
    ovt[:, :n_cmp] = overlap.T
    rows = -(-n_slc // 128) * 128
    expand = np.zeros((s_len // tq, rows, tq), np.float32)
    tok = np.arange(s_len)
    expand[tok // tq, tok // SLC_BLOCK, tok % tq] = 1.0
    gsel = np.zeros((128, 3 * BRANCH_WIDTH), np.float32)
    for h in range(N_HEADS):
        for j in range(3):
            gsel[h * 3 + j, j * BRANCH_WIDTH + h * HEAD_DIM:j * BRANCH_WIDTH + (h + 1) * HEAD_DIM] = 1.0
    nwin = -(-WINDOW // tq) + 1
    d = np.arange(nwin)[:, None, None] * tq + np.arange(tq)[None, :, None] - np.arange(tq)[None, None, :]
    wbias = np.where((d >= 0) & (d < WINDOW), 0.0, NEG_INF).astype(np.float32)
    return n_cmp, n_slc, ovt, expand, gsel, wbias


def _nsa_attention(dq, dqr, dg, k_cmp, v_cmp, ks, vs, kw, vw):
    b, s_len, w = dq.shape
    tq = ATT_TILE
    n_cmp, n_slc, ovt, expand, gsel, wbias = _nsa_constants(tq, s_len)
    q_spec, kv_spec, _ = _att_specs(tq, s_len, w, w)
    nr = k_cmp.shape[1]
    cmp_spec = pl.BlockSpec((None, nr, w), lambda bi, i: (bi, 0, 0))
    g_spec = pl.BlockSpec((None, tq, dg.shape[-1]), lambda bi, i: (bi, i, 0))
    consts = [jnp.asarray(ovt, BF16), jnp.asarray(expand, BF16), jnp.asarray(gsel, BF16), jnp.asarray(wbias)]
    return pl.pallas_call(
        functools.partial(_nsa_kernel, tq=tq, n_cmp=n_cmp, n_slc=n_slc, n_sel=min(N_SELECT, n_slc)),
        grid=(b, s_len // tq),
        in_specs=[q_spec, q_spec, g_spec, cmp_spec, cmp_spec, kv_spec, kv_spec, kv_spec, kv_spec]
        + [_const_spec(c.shape) for c in consts],
        out_specs=q_spec,
        out_shape=jax.ShapeDtypeStruct((b, s_len, w), F32),
        scratch_shapes=_strip_scratch(N_HEADS, tq, s_len),
        compiler_params=pltpu.CompilerParams(dimension_semantics=("arbitrary", "arbitrary"),
                                             vmem_limit_bytes=BIG_VMEM_LIMIT),
        name="nsa_attention",
    )(dq, dqr, dg, k_cmp, v_cmp, ks, vs, kw, vw, *consts)


def _mem_kv_kernel(m_ref, g_ref, w_ref, k_ref, v_ref):
    kv = _dot(_rms(m_ref[...], g_ref[...]).astype(BF16), w_ref[...])
    k_ref[...] = kv[:, :BRANCH_WIDTH].astype(BF16)
    v_ref[...] = kv[:, BRANCH_WIDTH:].astype(BF16)


def _mem_kv(mem2, g, w_kv):
    rows, d = mem2.shape
    tm = ROW_TILE
    out = pl.BlockSpec((tm, BRANCH_WIDTH), lambda i: (i, 0))
    return pl.pallas_call(
        _mem_kv_kernel,
        grid=(rows // tm,),
        in_specs=[pl.BlockSpec((tm, d), lambda i: (i, 0)), _const_spec((1, d)), _const_spec(w_kv.shape)],
        out_specs=[out, out],
        out_shape=[jax.ShapeDtypeStruct((rows, BRANCH_WIDTH), BF16)] * 2,
        compiler_params=pltpu.CompilerParams(dimension_semantics=("arbitrary",)),
        name="mem_kv",
    )(mem2, g.astype(F32).reshape(1, d), w_kv.astype(BF16))


def _mem_attn_kernel(q_ref, k_ref, v_ref, o_ref):
    q = q_ref[...]
    k = k_ref[...]
    v = v_ref[...]
    o = jnp.zeros(o_ref.shape, F32)
    for h in range(N_HEADS):
        s = _dot_nt(q * _head_mask(h, BF16), k)
        e = jnp.exp(s - jnp.max(s, axis=-1, keepdims=True))
        oh = _dot(e.astype(BF16), v) / jnp.sum(e, axis=-1, keepdims=True)
        o = jnp.where(_head_mask(h, F32) > 0.5, oh, o)
    o_ref[...] = o


def _mem_attention(q, k, v):
    b, s_len, w = q.shape
    tq = min(MEM_ATT_TILE, s_len)
    q_spec = pl.BlockSpec((None, tq, w), lambda bi, i: (bi, i, 0))
    kv_spec = pl.BlockSpec((None, k.shape[1], w), lambda bi, i: (bi, 0, 0))
    return pl.pallas_call(
        _mem_attn_kernel,
        grid=(b, s_len // tq),
        in_specs=[q_spec, kv_spec, kv_spec],
        out_specs=q_spec,
        out_shape=jax.ShapeDtypeStruct((b, s_len, w), F32),
        compiler_params=pltpu.CompilerParams(dimension_semantics=("arbitrary", "arbitrary")),
        name="mem_attention",
    )(q, k, v)


def _merge_kernel(x_ref, g_ref, oa_ref, ob_ref, oc_ref, od_ref, oe_ref, zs_ref, wm_ref, bm_ref, wb_ref,
                  wo_ref, fg_ref, o_ref, *, final_norm):
    x = x_ref[...]
    d = x.shape[-1]
    w = BRANCH_WIDTH
    h = _rms(x, g_ref[...]).astype(BF16)
    mixed = jnp.zeros(x.shape, F32)
    for n, br_ref in enumerate((oa_ref, ob_ref, oc_ref, od_ref, oe_ref)):
        br = (br_ref[...] * zs_ref[:, n * w:(n + 1) * w]).astype(BF16)
        y = _dot(br, wb_ref[n].astype(BF16))
        gate = jax.nn.sigmoid(_dot(h, wm_ref[:, n * d:(n + 1) * d].astype(BF16)) + bm_ref[:, n * d:(n + 1) * d])
        mixed = mixed + gate * y
    out = x + _dot(mixed.astype(BF16), wo_ref[...].astype(BF16))
    if final_norm:
        out = _rms(out, fg_ref[...])
    o_ref[...] = out


def _merge(x2, g, branches, zs, w_merge, b_merge, w_branch, w_out, layer, final_g, final_norm):
    t, d = x2.shape
    tm = MERGE_ROW_TILE
    row = lambda wd: pl.BlockSpec((tm, wd), lambda i: (i, 0))

    def of_layer(a):
        nd = a.ndim - 1
        return pl.BlockSpec((None,) + a.shape[1:], lambda i: (layer,) + (0,) * nd, pipeline_mode=pl.Buffered(1))

    b_merge = b_merge.astype(F32).reshape(b_merge.shape[0], 1, -1)
    return pl.pallas_call(
        functools.partial(_merge_kernel, final_norm=final_norm),
        grid=(t // tm,),
        in_specs=[row(d), _const_spec((1, d))] + [row(BRANCH_WIDTH)] * N_BRANCHES + [row(zs.shape[1])]
        + [of_layer(w_merge), of_layer(b_merge), of_layer(w_branch), of_layer(w_out), _const_spec((1, d))],
        out_specs=row(d),
        out_shape=jax.ShapeDtypeStruct((t, d), F32),
        compiler_params=pltpu.CompilerParams(dimension_semantics=("arbitrary",),
                                             vmem_limit_bytes=BIG_VMEM_LIMIT),
        name="merge",
    )(x2, g.astype(F32).reshape(1, d), *branches, zs, w_merge.astype(F32), b_merge, w_branch.astype(F32),
      w_out.astype(F32), final_g.astype(F32).reshape(1, d))


def kernel(x, mem, norm_g, w_in, diff_lambda, diff_subln_g, s5_lambda_re, s5_lambda_im, s5_log_dt,
           s5_b_re, s5_b_im, s5_c_re, s5_c_im, s5_d, w_glu, b_glu, nsa_pe, nsa_w1, nsa_w2, mem_norm_g,
           w_mem_kv, w_merge, b_merge, w_branch, w_out, final_g):
    bsz, s_len, d = x.shape
    depth = w_in.shape[0]
    t = bsz * s_len
    w = BRANCH_WIDTH
    tables = _rope_tables(s_len)
    x2 = x.astype(F32).reshape(t, d)
    mem2 = mem.astype(F32).reshape(-1, d)
    for l in range(depth):
        proj = dict(zip([n for n, _, _ in _IN_OUTS],
                        _in_proj(x2, norm_g[l].astype(F32), w_in.astype(F32), l, tables, s_len)))
        seq = lambda name: proj[name].reshape(bsz, s_len, -1)

        dl = diff_lambda[l].astype(F32)
        lam_init = 0.8 - 0.6 * math.exp(-0.3 * l)
        lam = jnp.exp(jnp.sum(dl[0] * dl[1])) - jnp.exp(jnp.sum(dl[2] * dl[3])) + lam_init
        o_a = _diff_attention(seq("qa"), seq("ka"), seq("va"), lam, diff_subln_g[l], lam_init)

        o_b = _dilated_attention(seq("qb"), seq("kb"), seq("vb"))

        s5p = _s5_params(s5_lambda_re[l], s5_lambda_im[l], s5_log_dt[l], s5_b_re[l], s5_b_im[l],
                         s5_c_re[l], s5_c_im[l])
        o_c = _s5_branch(seq("cu"), s5p, s5_d[l], w_glu[l], b_glu[l])

        k_cmp, v_cmp = _compress(seq("kvc"), nsa_pe[l], nsa_w1[l], nsa_w2[l])
        o_d = _nsa_attention(seq("dq"), seq("dqr"), seq("dg"), k_cmp, v_cmp,
                             seq("ks"), seq("vs"), seq("kw"), seq("vw"))

        k_mem, v_mem = _mem_kv(mem2, mem_norm_g[l], w_mem_kv[l])
        o_e = _mem_attention(seq("eq"), k_mem.reshape(bsz, -1, w), v_mem.reshape(bsz, -1, w))

        branches = [o.reshape(t, w) for o in (o_a, o_b, o_c, o_d, o_e)]
        x2 = _merge(x2, norm_g[l], branches, proj["zs"], w_merge, b_merge, w_branch, w_out, l,
                    final_g, final_norm=(l == depth - 1))
    return x2.reshape(bsz, s_len, d).astype(x.dtype)
```

```python
import functools
import math

import numpy as np
import jax
import jax.numpy as jnp
from jax import lax
from jax.experimental import pallas as pl
from jax.experimental.pallas import tpu as pltpu

F32 = jnp.float32
BF16 = jnp.bfloat16

HEAD_DIM = 64
BRANCH_WIDTH = 256
N_HEADS = 4
N_BRANCHES = 5
DIFF_QK_DIM = 32
DIL_PATTERNS = ((128, 1), (512, 4), (2048, 16))
S5_GROUP = 16
S5_GROUPS = 16
S5_STATE = 64
CMP_BLOCK = 32
CMP_STRIDE = 16
SLC_BLOCK = 64
N_SELECT = 16
WINDOW = 512
ROPE_THETA = 10000.0
RMS_EPS = 1e-6
NEG_INF = -1e30
FORCE_SCORE = 1e9

V7X_VMEM_BYTES = 64 * 1024 * 1024
BIG_VMEM_LIMIT = V7X_VMEM_BYTES - 8 * 1024 * 1024

LANES = 128
ATT_TILE = 512
ROW_TILE = 512
MERGE_ROW_TILE = 256
MEM_ATT_TILE = 2048
S5_CHUNK = 128

_NT = (((1,), (1,)), ((), ()))


def _rms(x, g):
    return x * lax.rsqrt(jnp.mean(x * x, axis=-1, keepdims=True) + RMS_EPS) * g


def _dot(a, b):
    return jnp.dot(a, b, preferred_element_type=F32)


def _dot_nt(a, b):
    return lax.dot_general(a, b, _NT, preferred_element_type=F32)


def _split_hi_lo(x):
    hi = x.astype(BF16)
    lo = (x - hi.astype(F32)).astype(BF16)
    return hi, lo


def _dot_hilo(x, w):
    hi, lo = _split_hi_lo(x)
    return _dot(hi, w) + _dot(lo, w)


def _const_spec(shape):
    n = len(shape)
    return pl.BlockSpec(shape, lambda *_: (0,) * n, pipeline_mode=pl.Buffered(1))


def _head_mask(h, dtype):
    lane = lax.broadcasted_iota(jnp.int32, (1, BRANCH_WIDTH), 1)
    return jnp.where((lane >= h * HEAD_DIM) & (lane < (h + 1) * HEAD_DIM), 1.0, 0.0).astype(dtype)


def _low_half():
    return lax.broadcasted_iota(jnp.int32, (1, LANES), 1) < HEAD_DIM


def _swap_halves(x):
    return pltpu.roll(x, HEAD_DIM, 1)


def _augment_heads(v):
    low = _low_half()
    parts = []
    for j in range(v.shape[1] // LANES):
        pair = v[:, j * LANES:(j + 1) * LANES]
        parts += [jnp.where(low, pair, 1.0), jnp.where(low, 1.0, pair)]
    return jnp.concatenate(parts, axis=1)


def _shared_kv_variants(kv):
    low = _low_half()
    sw = _swap_halves(kv)
    k2 = jnp.concatenate([jnp.where(low, kv, 0.0), jnp.where(low, 0.0, sw)], axis=1)
    v2 = jnp.concatenate([jnp.where(low, sw, 1.0), jnp.where(low, 1.0, kv)], axis=1)
    return k2, v2


def _normalise_pair(acc_even, acc_odd):
    return jnp.where(_low_half(), acc_even / _swap_halves(acc_even), acc_odd / _swap_halves(acc_odd))


_IN_MAIN_WIDTH = 3072
_IN_SRC = {"qa": (True, 0), "ka": (True, 256), "va": (True, 512), "az": (True, 768),
           "qb": (True, 1024), "kb": (True, 1280), "vb": (True, 1536), "bz": (True, 1792),
           "cu": (True, 2048), "cz": (True, 2304), "dq": (True, 2560), "kvcs": (True, 2816),
           "kvwg": (False, 0), "dz": (False, 256), "eq": (False, 512), "ez": (False, 768)}
_IN_TAIL_WIDTH = 1024
_IN_SCALED = ("qb", "dq", "eq")
_IN_OUTS = (("qa", 256, BF16), ("ka", 256, BF16), ("va", 512, BF16),
            ("qb", 256, BF16), ("kb", 256, BF16), ("vb", 512, BF16),
            ("cu", 256, F32), ("dq", 256, BF16), ("dqr", 256, BF16), ("kvc", 128, F32),
            ("ks", 256, BF16), ("vs", 256, BF16), ("kw", 256, BF16), ("vw", 256, BF16),
            ("dg", 128, F32), ("eq", 256, BF16), ("zs", 1280, F32))


def _rotate_half(y, group):
    half = group // 2
    lane = lax.broadcasted_iota(jnp.int32, (1, LANES), 1)
    first = (lane & (group - 1)) < half
    parts = []
    for j in range(y.shape[1] // LANES):
        v = y[:, j * LANES:(j + 1) * LANES]
        parts.append(jnp.where(first, pltpu.roll(v, LANES - half, 1), pltpu.roll(v, half, 1)))
    return jnp.concatenate(parts, axis=1)


def _in_proj_kernel(x_ref, g_ref, wm_ref, wt_ref, cosa_ref, sina_ref, cosb_ref, sinb_ref, cosk_ref, sink_ref,
                    *out_refs):
    out = dict(zip([n for n, _, _ in _IN_OUTS], out_refs))
    h = _rms(x_ref[...], g_ref[...]).astype(BF16)

    def proj(name):
        main, off = _IN_SRC[name]
        y = _dot(h, (wm_ref if main else wt_ref)[:, off:off + BRANCH_WIDTH].astype(BF16))
        return y * HEAD_DIM ** -0.5 if name in _IN_SCALED else y

    def rope(y, cos_ref, sin_ref, group):
        return y * cos_ref[...] + _rotate_half(y, group) * sin_ref[...]

    def put(name, y):
        out[name][...] = y.astype(out[name].dtype)

    put("qa", rope(proj("qa"), cosa_ref, sina_ref, DIFF_QK_DIM))
    put("ka", rope(proj("ka"), cosa_ref, sina_ref, DIFF_QK_DIM))
    put("va", _augment_heads(proj("va")))
    put("qb", rope(proj("qb"), cosb_ref, sinb_ref, HEAD_DIM))
    put("kb", rope(proj("kb"), cosb_ref, sinb_ref, HEAD_DIM))
    put("vb", _augment_heads(proj("vb")))
    put("cu", proj("cu"))
    dq = proj("dq")
    put("dq", dq)
    put("dqr", rope(dq, cosb_ref, sinb_ref, HEAD_DIM))
    kvcs = proj("kvcs")
    kvwg = proj("kvwg")
    put("kvc", kvcs[:, :LANES])
    put("dg", kvwg[:, LANES:])
    for kv, k_name, v_name in ((kvcs[:, LANES:], "ks", "vs"), (kvwg[:, :LANES], "kw", "vw")):
        k2, v2 = _shared_kv_variants(rope(kv, cosk_ref, sink_ref, HEAD_DIM))
        put(k_name, k2)
        put(v_name, v2)
    put("eq", proj("eq"))
    for n, name in enumerate(("az", "bz", "cz", "dz", "ez")):
        z = proj(name)
        out["zs"][:, n * BRANCH_WIDTH:(n + 1) * BRANCH_WIDTH] = z * jax.nn.sigmoid(z)


def _in_tail(w):
    used = LANES + 3 * N_HEADS
    tail = w[:, _IN_MAIN_WIDTH:]
    return jnp.concatenate([jnp.pad(tail[:, :used], ((0, 0), (0, BRANCH_WIDTH - used))), tail[:, used:]], axis=1)


def _rope_tables(s_len):
    def table(group, width):
        half = group // 2
        inv_freq = ROPE_THETA ** (-jnp.arange(half, dtype=F32) / half)
        ang = jnp.arange(s_len, dtype=F32)[:, None] * inv_freq[None, :]
        cos = jnp.tile(jnp.cos(ang), (1, width // half))
        sin = jnp.tile(jnp.concatenate([-jnp.sin(ang), jnp.sin(ang)], axis=1), (1, width // group))
        return cos, sin
    cos_a, sin_a = table(DIFF_QK_DIM, BRANCH_WIDTH)
    cos_b, sin_b = table(HEAD_DIM, BRANCH_WIDTH)
    cos_k = jnp.concatenate([cos_b[:, :HEAD_DIM], jnp.ones((s_len, HEAD_DIM), F32)], axis=1)
    sin_k = jnp.concatenate([sin_b[:, :HEAD_DIM], jnp.zeros((s_len, HEAD_DIM), F32)], axis=1)
    return cos_a, sin_a, cos_b, sin_b, cos_k, sin_k


def _in_proj(x2, g, w_in, layer, tables, s_len):
    t, d = x2.shape
    tm = ROW_TILE
    nsb = s_len // tm
    w_l = w_in[layer]
    w_tail = _in_tail(w_l)
    assert w_tail.shape[1] == _IN_TAIL_WIDTH
    row = lambda w: pl.BlockSpec((tm, w), lambda i: (i, 0))
    tab = lambda a: pl.BlockSpec((tm, a.shape[1]), lambda i: (i % nsb, 0))
    w_main = pl.BlockSpec((d, _IN_MAIN_WIDTH), lambda i: (0, 0), pipeline_mode=pl.Buffered(1))
    return pl.pallas_call(
        _in_proj_kernel,
        grid=(t // tm,),
        in_specs=[row(d), _const_spec((1, d)), w_main, _const_spec(w_tail.shape)] + [tab(a) for a in tables],
        out_specs=[row(w) for _, w, _ in _IN_OUTS],
        out_shape=[jax.ShapeDtypeStruct((t, w), dt) for _, w, dt in _IN_OUTS],
        compiler_params=pltpu.CompilerParams(dimension_semantics=("arbitrary",),
                                             vmem_limit_bytes=BIG_VMEM_LIMIT),
        name="in_proj",
    )(x2, g.reshape(1, d), w_l, w_tail, *tables)


def _lane_fold(x, op):
    parts = [x[:, j * LANES:(j + 1) * LANES] for j in range(x.shape[1] // LANES)]
    return functools.reduce(op, parts)


def _strip_attention(chains, tq, kb_lo, kb_hi, bias_fn, last_bias, scratch, exp_scale=1.0):
    s_ref, m_ref, acc_ref = scratch
    n = len(chains)
    reps = tq // LANES
    c1 = exp_scale * math.log2(math.e)

    def tile(cache, ref, col, width, kb, ntiles=1):
        key = (id(ref), col, width)
        if key not in cache:
            cache[key] = ref[pl.ds(pl.multiple_of(kb * tq, tq), ntiles * tq), col:col + width]
        return cache[key]

    def scores(kb, extra):
        bias = bias_fn(kb) if bias_fn is not None else None
        if extra is not None:
            bias = extra if bias is None else bias + extra
        cache, out = {}, []
        for q, (k_ref, k_col), _ in chains:
            s = _dot_nt(q, tile(cache, k_ref, k_col, q.shape[1], kb)) * c1
            out.append(s if bias is None else s + bias)
        return out

    def tile_pairs(lo, hi, step):
        def two(j, carry):
            step(lo + 2 * j, 2)
            return carry
        cnt = hi - lo
        lax.fori_loop(0, lax.shift_right_arithmetic(cnt, 1), two, 0)

        @pl.when((cnt & 1) == 1)
        def _():
            step(hi - 1, 1)

    m_ref[0:n] = jnp.full((n,) + m_ref.shape[1:], NEG_INF, F32)

    def pass1(kb, ntiles):
        tiles = [scores(kb + j, None) for j in range(ntiles)]
        for i in range(n):
            m = m_ref[i]
            for j in range(ntiles):
                s_ref[i, kb + j] = tiles[j][i]
                m = jnp.maximum(m, _lane_fold(tiles[j][i], jnp.maximum))
            m_ref[i] = m

    tile_pairs(kb_lo, kb_hi - 1, pass1)
    for i, s in enumerate(scores(kb_hi - 1, last_bias)):
        s_ref[i, kb_hi - 1] = s
        m = jnp.max(jnp.maximum(m_ref[i], _lane_fold(s, jnp.maximum)), axis=-1, keepdims=True)
        m_ref[i] = jnp.broadcast_to(m, m_ref.shape[1:])
        acc_ref[i] = jnp.zeros(acc_ref.shape[1:], F32)

    def pass2(kb, ntiles):
        cache = {}
        for i, (_, _, (v_ref, v_col)) in enumerate(chains):
            m = jnp.concatenate([m_ref[i]] * reps, axis=1)
            p = [jnp.exp2(s_ref[i, kb + j] - m).astype(BF16) for j in range(ntiles)]
            p = p[0] if ntiles == 1 else jnp.concatenate(p, axis=1)
            acc_ref[i] += _dot(p, tile(cache, v_ref, v_col, LANES, kb, ntiles))

    tile_pairs(kb_lo, kb_hi, pass2)
    return [acc_ref[i] for i in range(n)]


def _strip_scratch(n, tq, s_len):
    return [pltpu.VMEM((n, s_len // tq, tq, tq), F32), pltpu.VMEM((n, tq, LANES), F32),
            pltpu.VMEM((n, tq, LANES), F32)]


def _causal_bias(tq):
    r = lax.broadcasted_iota(jnp.int32, (tq, tq), 0)
    c = lax.broadcasted_iota(jnp.int32, (tq, tq), 1)
    return jnp.where(c <= r, 0.0, NEG_INF).astype(F32)


def _att_specs(tq, s_len, k_width, v_width):
    q_spec = pl.BlockSpec((None, tq, BRANCH_WIDTH), lambda b, i: (b, i, 0))
    kv = lambda w: pl.BlockSpec((None, s_len, w), lambda b, i: (b, 0, 0))
    return q_spec, kv(k_width), kv(v_width)


def _diff_kernel(lam_ref, q_ref, k_ref, v_ref, g_ref, hm_ref, o_ref, *scratch, tq, out_scale):
    qi = pl.program_id(1)
    q = q_ref[...]
    lane = lax.broadcasted_iota(jnp.int32, (1, BRANCH_WIDTH), 1)
    lam = lam_ref[0]
    chains = []
    for hc in range(2 * N_HEADS):
        lo = hc * DIFF_QK_DIM
        cmask = jnp.where((lane >= lo) & (lane < lo + DIFF_QK_DIM), 1.0, 0.0).astype(BF16)
        chains.append((q * cmask, (k_ref, 0), (v_ref, (hc // 2) * LANES)))
    acc = _strip_attention(chains, tq, 0, qi + 1, None, _causal_bias(tq), scratch,
                           exp_scale=DIFF_QK_DIM ** -0.5)
    halves = []
    for pair in range(N_HEADS // 2):
        even, odd = 4 * pair, 4 * pair + 2
        halves.append(_normalise_pair(acc[even], acc[odd]) - lam * _normalise_pair(acc[even + 1], acc[odd + 1]))
    o = jnp.concatenate(halves, axis=1)
    ms = _dot_hilo(o * o, hm_ref[...])
    o_ref[...] = o * lax.rsqrt(ms + RMS_EPS) * g_ref[...] * out_scale


def _diff_attention(q, k, v_aug, lam, subln_g, lam_init):
    b, s_len, w = q.shape
    tq = ATT_TILE
    q_spec, k_spec, v_spec = _att_specs(tq, s_len, w, v_aug.shape[-1])
    head = np.arange(w) // HEAD_DIM
    hm = jnp.asarray((head[:, None] == head[None, :]) / HEAD_DIM, dtype=BF16)
    g = jnp.tile(subln_g.astype(F32), N_HEADS).reshape(1, w)
    return pl.pallas_call(
        functools.partial(_diff_kernel, tq=tq, out_scale=1.0 - lam_init),
        grid=(b, s_len // tq),
        in_specs=[pl.BlockSpec(memory_space=pltpu.SMEM), q_spec, k_spec, v_spec,
                  _const_spec((1, w)), _const_spec((w, w))],
        out_specs=q_spec,
        out_shape=jax.ShapeDtypeStruct((b, s_len, w), F32),
        scratch_shapes=_strip_scratch(2 * N_HEADS, tq, s_len),
        compiler_params=pltpu.CompilerParams(dimension_semantics=("arbitrary", "arbitrary"),
                                             vmem_limit_bytes=BIG_VMEM_LIMIT),
        name="diff_attention",
    )(lam.reshape(1), q, k, v_aug, g, hm)


def _dil_kernel(q_ref, k_ref, v_ref, bias_ref, o_ref, *scratch, tq):
    qi = pl.program_id(1)
    q = q_ref[...]
    chains = [(q * _head_mask(h, BF16), (k_ref, 0), (v_ref, h * LANES)) for h in range(N_HEADS)]
    acc = _strip_attention(chains, tq, 0, qi + 1, lambda kb: bias_ref[qi - kb], None, scratch)
    o_ref[...] = jnp.concatenate([_normalise_pair(acc[0], acc[1]), _normalise_pair(acc[2], acc[3])], axis=1)


def _dilated_bias(tq, s_len):
    nq = s_len // tq
    d = (np.arange(nq)[:, None, None] * tq + np.arange(tq)[None, :, None] - np.arange(tq)[None, None, :])
    count = np.zeros(d.shape, np.float64)
    for window, dil in DIL_PATTERNS:
        count += (d >= 0) & (d <= window) & (d % dil == 0)
    return np.where(count > 0, np.log2(np.maximum(count, 1.0)), NEG_INF).astype(np.float32)


def _dilated_attention(q, k, v_aug):
    b, s_len, w = q.shape
    tq = ATT_TILE
    q_spec, k_spec, v_spec = _att_specs(tq, s_len, w, v_aug.shape[-1])
    bias = jnp.asarray(_dilated_bias(tq, s_len))
    return pl.pallas_call(
        functools.partial(_dil_kernel, tq=tq),
        grid=(b, s_len // tq),
        in_specs=[q_spec, k_spec, v_spec, _const_spec(bias.shape)],
        out_specs=q_spec,
        out_shape=jax.ShapeDtypeStruct((b, s_len, w), F32),
        scratch_shapes=_strip_scratch(N_HEADS, tq, s_len),
        compiler_params=pltpu.CompilerParams(dimension_semantics=("arbitrary", "arbitrary"),
                                             vmem_limit_bytes=BIG_VMEM_LIMIT),
        name="dilated_attention",
    )(q, k, v_aug, bias)


def _s5_kernel(u_ref, bm_ref, cm_ref, are_ref, aim_ref, d_ref, wg_ref, bg_ref, o_ref, st_ref, *xs_refs,
               ts, nb):
    n = S5_GROUPS * S5_STATE
    nc = n // LANES

    @pl.when(pl.program_id(0) == 0)
    def _():
        st_ref[...] = jnp.zeros(st_ref.shape, F32)

    u = u_ref[...].reshape(nb * ts, u_ref.shape[-1])
    bu = _dot(u.astype(BF16), bm_ref[...])
    for c, x_ref in enumerate(xs_refs):
        for b in range(nb):
            x_ref[pl.ds(b, ts, stride=nb), :] = bu[b * ts:(b + 1) * ts, c * LANES:(c + 1) * LANES]
    a_re = jnp.broadcast_to(are_ref[...], (nb, n))
    a_im = jnp.broadcast_to(aim_ref[...], (nb, n))

    def step(t, carry):
        x_re, x_im = carry
        rows = pl.ds(pl.multiple_of(t * nb, nb), nb)
        bu_re = jnp.concatenate([x_ref[rows, :] for x_ref in xs_refs[:nc]], axis=1)
        bu_im = jnp.concatenate([x_ref[rows, :] for x_ref in xs_refs[nc:]], axis=1)
        n_re = a_re * x_re - a_im * x_im + bu_re
        n_im = a_re * x_im + a_im * x_re + bu_im
        for c in range(nc):
            xs_refs[c][rows, :] = n_re[:, c * LANES:(c + 1) * LANES]
            xs_refs[nc + c][rows, :] = n_im[:, c * LANES:(c + 1) * LANES]
        return n_re, n_im

    x_re, x_im = lax.fori_loop(0, ts, step, (st_ref[:, 0:n], st_ref[:, n:2 * n]))
    st_ref[:, 0:n] = x_re
    st_ref[:, n:2 * n] = x_im

    xs = jnp.concatenate(
        [jnp.concatenate([x_ref[pl.ds(b, ts, stride=nb), :].astype(BF16) for x_ref in xs_refs], axis=1)
         for b in range(nb)], axis=0)
    y = _dot(xs, cm_ref[...]) + d_ref[...] * u
    t = _dot(jax.nn.gelu(y).astype(BF16), wg_ref[...]) + bg_ref[...]
    o_ref[...] = (t[:, :BRANCH_WIDTH] * jax.nn.sigmoid(t[:, BRANCH_WIDTH:])).reshape(o_ref.shape)


def _s5_params(lam_re, lam_im, log_dt, b_re, b_im, c_re, c_im):
    g, n, p = S5_GROUPS, S5_STATE, S5_GROUP
    lr, li = lam_re.astype(F32), lam_im.astype(F32)
    dt = jnp.exp(log_dt.astype(F32))[:, None]
    mag = jnp.exp(lr * dt)
    a_re, a_im = mag * jnp.cos(li * dt), mag * jnp.sin(li * dt)
    den = lr * lr + li * li
    n_re, n_im = a_re - 1.0, a_im
    z_re = (n_re * lr + n_im * li) / den
    z_im = (n_im * lr - n_re * li) / den
    br, bi = b_re.astype(F32), b_im.astype(F32)
    bb_re = z_re[..., None] * br - z_im[..., None] * bi
    bb_im = z_re[..., None] * bi + z_im[..., None] * br
    eye = jnp.eye(g, dtype=F32)
    blockdiag_in = lambda t: jnp.einsum("gnp,gh->gphn", t, eye).reshape(g * p, g * n)
    blockdiag_out = lambda t: jnp.einsum("gpn,gh->gnhp", t, eye).reshape(g * n, g * p)
    bm = jnp.concatenate([blockdiag_in(bb_re), blockdiag_in(bb_im)], axis=1)
    cm = jnp.concatenate([blockdiag_out(c_re.astype(F32)), -blockdiag_out(c_im.astype(F32))], axis=0)
    return bm.astype(BF16), cm.astype(BF16), a_re.reshape(1, g * n), a_im.reshape(1, g * n)


def _s5_branch(u, params, d_skip, w_glu, b_glu):
    nb, s_len, w = u.shape
    assert nb == 8
    ts = S5_CHUNK
    bm, cm, a_re, a_im = params
    n2 = bm.shape[1]
    blk = pl.BlockSpec((nb, ts, w), lambda i: (0, i, 0))
    return pl.pallas_call(
        functools.partial(_s5_kernel, ts=ts, nb=nb),
        grid=(s_len // ts,),
        in_specs=[blk, _const_spec(bm.shape), _const_spec(cm.shape), _const_spec(a_re.shape),
                  _const_spec(a_im.shape), _const_spec((1, w)), _const_spec(w_glu.shape),
                  _const_spec((1, 2 * w))],
        out_specs=blk,
        out_shape=jax.ShapeDtypeStruct((nb, s_len, w), F32),
        scratch_shapes=[pltpu.VMEM((nb, n2), F32)] + [pltpu.VMEM((ts * nb, LANES), F32)] * (n2 // LANES),
        compiler_params=pltpu.CompilerParams(dimension_semantics=("arbitrary",),
                                             vmem_limit_bytes=BIG_VMEM_LIMIT),
        name="s5_scan",
    )(u, bm, cm, a_re, a_im, d_skip.astype(F32).reshape(1, w), w_glu.astype(BF16),
      b_glu.astype(F32).reshape(1, 2 * w))


def _compress_kernel(r_ref, pe_top_ref, pe_bot_ref, w_top_ref, w_bot_ref, w2_ref, k_ref, v_ref):
    r = r_ref[...]
    top = (r + pe_top_ref[...]).astype(BF16)
    nxt = pltpu.roll(r, r.shape[0] - 1, 0)
    bot = (nxt + pe_bot_ref[...]).astype(BF16)
    hid = jax.nn.gelu(_dot(top, w_top_ref[...]) + _dot(bot, w_bot_ref[...]))
    k2, v2 = _shared_kv_variants(_dot(hid.astype(BF16), w2_ref[...]))
    k_ref[...] = k2.astype(BF16)
    v_ref[...] = v2.astype(BF16)


def _compress(kvc, pe, w1, w2):
    b, s_len, _ = kvc.shape
    nr = s_len // CMP_STRIDE
    per = CMP_BLOCK // CMP_STRIDE
    assert per == 2
    hid = w1.shape[-1]
    r = kvc.reshape(b, nr, CMP_STRIDE * 2 * HEAD_DIM)
    w1r = w1.astype(F32).reshape(2, per, CMP_STRIDE, HEAD_DIM, hid)
    per_r = pe.astype(F32).reshape(2, per, CMP_STRIDE, HEAD_DIM)

    def expand(j):
        wk = jnp.pad(w1r[0, j], ((0, 0), (0, HEAD_DIM), (0, hid)))
        wv = jnp.pad(w1r[1, j], ((0, 0), (HEAD_DIM, 0), (hid, 0)))
        return (wk + wv).reshape(CMP_STRIDE * 2 * HEAD_DIM, 2 * hid).astype(BF16)

    pe_rows = [jnp.concatenate([per_r[0, j], per_r[1, j]], axis=-1).reshape(1, -1) for j in range(per)]
    z = jnp.zeros((hid, HEAD_DIM), F32)
    w2f = w2.astype(F32)
    w2x = jnp.concatenate([jnp.concatenate([w2f[0], z], axis=1),
                           jnp.concatenate([z, w2f[1]], axis=1)], axis=0).astype(BF16)
    w_top, w_bot = expand(0), expand(1)
    blk = pl.BlockSpec((None, nr, r.shape[-1]), lambda i: (i, 0, 0))
    out = pl.BlockSpec((None, nr, BRANCH_WIDTH), lambda i: (i, 0, 0))
    return pl.pallas_call(
        _compress_kernel,
        grid=(b,),
        in_specs=[blk, _const_spec(pe_rows[0].shape), _const_spec(pe_rows[1].shape),
                  _const_spec(w_top.shape), _const_spec(w_bot.shape), _const_spec(w2x.shape)],
        out_specs=[out, out],
        out_shape=[jax.ShapeDtypeStruct((b, nr, BRANCH_WIDTH), BF16)] * 2,
        compiler_params=pltpu.CompilerParams(dimension_semantics=("arbitrary",)),
        name="nsa_compress",
    )(r, pe_rows[0], pe_rows[1], w_top, w_bot, w2x)


def _nsa_kernel(dq_ref, dqr_ref, dg_ref, kc_ref, vc_ref, ks_ref, vs_ref, kw_ref, vw_ref,
                ovt_ref, exp_ref, gsel_ref, wbias_ref, o_ref, *scratch, tq, n_cmp, n_slc, n_sel):
    qi = pl.program_id(1)
    t0 = qi * tq
    nr = kc_ref.shape[0]
    low = _low_half()
    q_cols = lambda h: slice((h // 2) * LANES, (h // 2 + 1) * LANES)
    kv_col = lambda h: (h % 2) * LANES

    dq = dq_ref[...]
    tpos = t0 + lax.broadcasted_iota(jnp.int32, (tq, nr), 0)
    ci = lax.broadcasted_iota(jnp.int32, (tq, nr), 1)
    cmask = (ci * CMP_STRIDE + (CMP_BLOCK - 1) <= tpos) & (ci < n_cmp)
    p_sum = jnp.zeros((tq, nr), F32)
    o_heads = []
    for h in range(N_HEADS):
        s = jnp.where(cmask, _dot_nt(dq[:, q_cols(h)], kc_ref[:, kv_col(h):kv_col(h) + LANES]), NEG_INF)
        m = jnp.max(s, axis=-1, keepdims=True)
        e = jnp.where(cmask, jnp.exp(s - m), 0.0)
        p = e / jnp.maximum(jnp.sum(e, axis=-1, keepdims=True), 1e-30)
        p_sum = p_sum + p
        o_heads.append(_dot(p.astype(BF16), vc_ref[:, kv_col(h):kv_col(h) + LANES]))
    o_cmp = jnp.concatenate([jnp.where(low, o_heads[0], o_heads[1]), jnp.where(low, o_heads[2], o_heads[3])],
                            axis=1)

    p_hi, p_lo = _split_hi_lo(p_sum)
    ovt = ovt_ref[...]
    imp_t = _dot_nt(ovt, p_hi) + _dot_nt(ovt, p_lo)
    blk = lax.broadcasted_iota(jnp.int32, (n_slc, tq), 0)
    qblk = lax.shift_right_arithmetic(t0 + lax.broadcasted_iota(jnp.int32, (n_slc, tq), 1),
                                      int(math.log2(SLC_BLOCK)))
    forced = jnp.where(blk == 0, 1, jnp.where(blk == qblk, 1, jnp.where(blk == qblk - 1, 1, 0)))
    score = jnp.where(blk <= qblk, jnp.where(forced > 0, FORCE_SCORE, imp_t), NEG_INF)
    rank = jnp.zeros((n_slc, tq), F32)
    for i in range(n_slc):
        si = score[i:i + 1, :]
        tie = jnp.where(blk > i, 1.0, 0.0)
        rank = rank + jnp.where(si > score, 1.0, jnp.where(si == score, tie, 0.0))
    sel_bias_t = jnp.where(rank < n_sel, jnp.where(score > 0.5 * NEG_INF, 0.0, NEG_INF), NEG_INF)
    pad_rows = exp_ref.shape[1] - n_slc
    sel_bias = jnp.concatenate([sel_bias_t, jnp.full((pad_rows, tq), NEG_INF, F32)], axis=0).T.astype(BF16)

    dqr = dqr_ref[...]
    pairs = lambda acc: jnp.concatenate([_normalise_pair(acc[0], acc[1]), _normalise_pair(acc[2], acc[3])], axis=1)
    chains = [(dqr[:, q_cols(h)], (ks_ref, kv_col(h)), (vs_ref, kv_col(h))) for h in range(N_HEADS)]
    o_slc = pairs(_strip_attention(chains, tq, 0, qi + 1, lambda kb: _dot(sel_bias, exp_ref[kb]),
                                   _causal_bias(tq), scratch))
    chains = [(dqr[:, q_cols(h)], (kw_ref, kv_col(h)), (vw_ref, kv_col(h))) for h in range(N_HEADS)]
    win_lo = jnp.maximum(qi - (wbias_ref.shape[0] - 1), 0)
    o_win = pairs(_strip_attention(chains, tq, win_lo, qi + 1, lambda kb: wbias_ref[qi - kb], None, scratch))

    gates = _dot_hilo(jax.nn.sigmoid(dg_ref[...]), gsel_ref[...])
    w = BRANCH_WIDTH
    o_ref[...] = gates[:, :w] * o_cmp + gates[:, w:2 * w] * o_slc + gates[:, 2 * w:] * o_win


def _nsa_constants(tq, s_len):
    n_cmp = (s_len - CMP_BLOCK) // CMP_STRIDE + 1
    n_slc = s_len // SLC_BLOCK
    nr = s_len // CMP_STRIDE
    c0 = np.arange(n_cmp)[:, None] * CMP_STRIDE
    s0 = np.arange(n_slc)[None, :] * SLC_BLOCK
    overlap = np.clip(np.minimum(c0 + CMP_BLOCK, s0 + SLC_BLOCK) - np.maximum(c0, s0), 0, None) / CMP_STRIDE
    ovt = np.zeros((n_slc, nr), np.float32)
    ovt[:, :n_cmp] = overlap.T
    rows = -(-n_slc // 128) * 128
    expand = np.zeros((s_len // tq, rows, tq), np.float32)
    tok = np.arange(s_len)
    expand[tok // tq, tok // SLC_BLOCK, tok % tq] = 1.0
    gsel = np.zeros((128, 3 * BRANCH_WIDTH), np.float32)
    for h in range(N_HEADS):
        for j in range(3):
            gsel[h * 3 + j, j * BRANCH_WIDTH + h * HEAD_DIM:j * BRANCH_WIDTH + (h + 1) * HEAD_DIM] = 1.0
    nwin = -(-WINDOW // tq) + 1
    d = np.arange(nwin)[:, None, None] * tq + np.arange(tq)[None, :, None] - np.arange(tq)[None, None, :]
    wbias = np.where((d >= 0) & (d < WINDOW), 0.0, NEG_INF).astype(np.float32)
    return n_cmp, n_slc, ovt, expand, gsel, wbias


def _nsa_attention(dq, dqr, dg, k_cmp, v_cmp, ks, vs, kw, vw):
    b, s_len, w = dq.shape
    tq = ATT_TILE
    n_cmp, n_slc, ovt, expand, gsel, wbias = _nsa_constants(tq, s_len)
    q_spec, kv_spec, _ = _att_specs(tq, s_len, w, w)
    nr = k_cmp.shape[1]
    cmp_spec = pl.BlockSpec((None, nr, w), lambda bi, i: (bi, 0, 0))
    g_spec = pl.BlockSpec((None, tq, dg.shape[-1]), lambda bi, i: (bi, i, 0))
    consts = [jnp.asarray(ovt, BF16), jnp.asarray(expand, BF16), jnp.asarray(gsel, BF16), jnp.asarray(wbias)]
    return pl.pallas_call(
        functools.partial(_nsa_kernel, tq=tq, n_cmp=n_cmp, n_slc=n_slc, n_sel=min(N_SELECT, n_slc)),
        grid=(b, s_len // tq),
        in_specs=[q_spec, q_spec, g_spec, cmp_spec, cmp_spec, kv_spec, kv_spec, kv_spec, kv_spec]
        + [_const_spec(c.shape) for c in consts],
        out_specs=q_spec,
        out_shape=jax.ShapeDtypeStruct((b, s_len, w), F32),
        scratch_shapes=_strip_scratch(N_HEADS, tq, s_len),
        compiler_params=pltpu.CompilerParams(dimension_semantics=("arbitrary", "arbitrary"),
                                             vmem_limit_bytes=BIG_VMEM_LIMIT),
        name="nsa_attention",
    )(dq, dqr, dg, k_cmp, v_cmp, ks, vs, kw, vw, *consts)


def _mem_kv_kernel(m_ref, g_ref, w_ref, k_ref, v_ref):
    kv = _dot(_rms(m_ref[...], g_ref[...]).astype(BF16), w_ref[...])
    k_ref[...] = kv[:, :BRANCH_WIDTH].astype(BF16)
    v_ref[...] = kv[:, BRANCH_WIDTH:].astype(BF16)


def _mem_kv(mem2, g, w_kv):
    rows, d = mem2.shape
    tm = ROW_TILE
    out = pl.BlockSpec((tm, BRANCH_WIDTH), lambda i: (i, 0))
    return pl.pallas_call(
        _mem_kv_kernel,
        grid=(rows // tm,),
        in_specs=[pl.BlockSpec((tm, d), lambda i: (i, 0)), _const_spec((1, d)), _const_spec(w_kv.shape)],
        out_specs=[out, out],
        out_shape=[jax.ShapeDtypeStruct((rows, BRANCH_WIDTH), BF16)] * 2,
        compiler_params=pltpu.CompilerParams(dimension_semantics=("arbitrary",)),
        name="mem_kv",
    )(mem2, g.astype(F32).reshape(1, d), w_kv.astype(BF16))


def _mem_attn_kernel(q_ref, k_ref, v_ref, o_ref):
    q = q_ref[...]
    k = k_ref[...]
    v = v_ref[...]
    o = jnp.zeros(o_ref.shape, F32)
    for h in range(N_HEADS):
        s = _dot_nt(q * _head_mask(h, BF16), k)
        e = jnp.exp(s - jnp.max(s, axis=-1, keepdims=True))
        oh = _dot(e.astype(BF16), v) / jnp.sum(e, axis=-1, keepdims=True)
        o = jnp.where(_head_mask(h, F32) > 0.5, oh, o)
    o_ref[...] = o


def _mem_attention(q, k, v):
    b, s_len, w = q.shape
    tq = min(MEM_ATT_TILE, s_len)
    q_spec = pl.BlockSpec((None, tq, w), lambda bi, i: (bi, i, 0))
    kv_spec = pl.BlockSpec((None, k.shape[1], w), lambda bi, i: (bi, 0, 0))
    return pl.pallas_call(
        _mem_attn_kernel,
        grid=(b, s_len // tq),
        in_specs=[q_spec, kv_spec, kv_spec],
        out_specs=q_spec,
        out_shape=jax.ShapeDtypeStruct((b, s_len, w), F32),
        compiler_params=pltpu.CompilerParams(dimension_semantics=("arbitrary", "arbitrary")),
        name="mem_attention",
    )(q, k, v)


def _merge_kernel(x_ref, g_ref, oa_ref, ob_ref, oc_ref, od_ref, oe_ref, zs_ref, wm_ref, bm_ref, wb_ref,
                  wo_ref, fg_ref, o_ref, *, final_norm):
    x = x_ref[...]
    d = x.shape[-1]
    w = BRANCH_WIDTH
    h = _rms(x, g_ref[...]).astype(BF16)
    mixed = jnp.zeros(x.shape, F32)
    for n, br_ref in enumerate((oa_ref, ob_ref, oc_ref, od_ref, oe_ref)):
        br = (br_ref[...] * zs_ref[:, n * w:(n + 1) * w]).astype(BF16)
        y = _dot(br, wb_ref[n].astype(BF16))
        gate = jax.nn.sigmoid(_dot(h, wm_ref[:, n * d:(n + 1) * d].astype(BF16)) + bm_ref[:, n * d:(n + 1) * d])
        mixed = mixed + gate * y
    out = x + _dot(mixed.astype(BF16), wo_ref[...].astype(BF16))
    if final_norm:
        out = _rms(out, fg_ref[...])
    o_ref[...] = out


def _merge(x2, g, branches, zs, w_merge, b_merge, w_branch, w_out, layer, final_g, final_norm):
    t, d = x2.shape
    tm = MERGE_ROW_TILE
    row = lambda wd: pl.BlockSpec((tm, wd), lambda i: (i, 0))

    def of_layer(a):
        nd = a.ndim - 1
        return pl.BlockSpec((None,) + a.shape[1:], lambda i: (layer,) + (0,) * nd, pipeline_mode=pl.Buffered(1))

    b_merge = b_merge.astype(F32).reshape(b_merge.shape[0], 1, -1)
    return pl.pallas_call(
        functools.partial(_merge_kernel, final_norm=final_norm),
        grid=(t // tm,),
        in_specs=[row(d), _const_spec((1, d))] + [row(BRANCH_WIDTH)] * N_BRANCHES + [row(zs.shape[1])]
        + [of_layer(w_merge), of_layer(b_merge), of_layer(w_branch), of_layer(w_out), _const_spec((1, d))],
        out_specs=row(d),
        out_shape=jax.ShapeDtypeStruct((t, d), F32),
        compiler_params=pltpu.CompilerParams(dimension_semantics=("arbitrary",),
                                             vmem_limit_bytes=BIG_VMEM_LIMIT),
        name="merge",
    )(x2, g.astype(F32).reshape(1, d), *branches, zs, w_merge.astype(F32), b_merge, w_branch.astype(F32),
      w_out.astype(F32), final_g.astype(F32).reshape(1, d))


def kernel(x, mem, norm_g, w_in, diff_lambda, diff_subln_g, s5_lambda_re, s5_lambda_im, s5_log_dt,
           s5_b_re, s5_b_im, s5_c_re, s5_c_im, s5_d, w_glu, b_glu, nsa_pe, nsa_w1, nsa_w2, mem_norm_g,
           w_mem_kv, w_merge, b_merge, w_branch, w_out, final_g):
    bsz, s_len, d = x.shape
    depth = w_in.shape[0]
    t = bsz * s_len
    w = BRANCH_WIDTH
    tables = _rope_tables(s_len)
    x2 = x.astype(F32).reshape(t, d)
    mem2 = mem.astype(F32).reshape(-1, d)
    for l in range(depth):
        proj = dict(zip([n for n, _, _ in _IN_OUTS],
                        _in_proj(x2, norm_g[l].astype(F32), w_in.astype(F32), l, tables, s_len)))
        seq = lambda name: proj[name].reshape(bsz, s_len, -1)

        dl = diff_lambda[l].astype(F32)
        lam_init = 0.8 - 0.6 * math.exp(-0.3 * l)
        lam = jnp.exp(jnp.sum(dl[0] * dl[1])) - jnp.exp(jnp.sum(dl[2] * dl[3])) + lam_init
        o_a = _diff_attention(seq("qa"), seq("ka"), seq("va"), lam, diff_subln_g[l], lam_init)

        o_b = _dilated_attention(seq("qb"), seq("kb"), seq("vb"))

        s5p = _s5_params(s5_lambda_re[l], s5_lambda_im[l], s5_log_dt[l], s5_b_re[l], s5_b_im[l],
                         s5_c_re[l], s5_c_im[l])
        o_c = _s5_branch(seq("cu"), s5p, s5_d[l], w_glu[l], b_glu[l])

        k_cmp, v_cmp = _compress(seq("kvc"), nsa_pe[l], nsa_w1[l], nsa_w2[l])
        o_d = _nsa_attention(seq("dq"), seq("dqr"), seq("dg"), k_cmp, v_cmp,
                             seq("ks"), seq("vs"), seq("kw"), seq("vw"))

        k_mem, v_mem = _mem_kv(mem2, mem_norm_g[l], w_mem_kv[l])
        o_e = _mem_attention(seq("eq"), k_mem.reshape(bsz, -1, w), v_mem.reshape(bsz, -1, w))

        branches = [o.reshape(t, w) for o in (o_a, o_b, o_c, o_d, o_e)]
        x2 = _merge(x2, norm_g[l], branches, proj["zs"], w_merge, b_merge, w_branch, w_out, l,
                    final_g, final_norm=(l == depth - 1))
    return x2.reshape(bsz, s_len, d).astype(x.dtype)
```

```python
import functools
import math

import numpy as np
import jax
import jax.numpy as jnp
from jax import lax
from jax.experimental import pallas as pl
from jax.experimental.pallas import tpu as pltpu

F32 = jnp.float32
BF16 = jnp.bfloat16

HEAD_DIM = 64
BRANCH_WIDTH = 256
N_HEADS = 4
N_BRANCHES = 5
DIFF_QK_DIM = 32
DIL_PATTERNS = ((128, 1), (512, 4), (2048, 16))
S5_GROUP = 16
S5_GROUPS = 16
S5_STATE = 64
CMP_BLOCK = 32
CMP_STRIDE = 16
SLC_BLOCK = 64
N_SELECT = 16
WINDOW = 512
ROPE_THETA = 10000.0
RMS_EPS = 1e-6
NEG_INF = -1e30
FORCE_SCORE = 1e9

V7X_VMEM_BYTES = 64 * 1024 * 1024
BIG_VMEM_LIMIT = V7X_VMEM_BYTES - 8 * 1024 * 1024

LANES = 128
ATT_TILE = 512
ROW_TILE = 512
MERGE_ROW_TILE = 512
MEM_ATT_TILE = 2048
S5_CHUNK = 128

_NT = (((1,), (1,)), ((), ()))


def _rms(x, g):
    return x * lax.rsqrt(jnp.mean(x * x, axis=-1, keepdims=True) + RMS_EPS) * g


def _dot(a, b):
    return jnp.dot(a, b, preferred_element_type=F32)


def _dot_nt(a, b):
    return lax.dot_general(a, b, _NT, preferred_element_type=F32)


def _split_hi_lo(x):
    hi = x.astype(BF16)
    lo = (x - hi.astype(F32)).astype(BF16)
    return hi, lo


def _dot_hilo(x, w):
    hi, lo = _split_hi_lo(x)
    return _dot(hi, w) + _dot(lo, w)


def _const_spec(shape):
    n = len(shape)
    return pl.BlockSpec(shape, lambda *_: (0,) * n, pipeline_mode=pl.Buffered(1))


def _head_mask(h, dtype):
    lane = lax.broadcasted_iota(jnp.int32, (1, BRANCH_WIDTH), 1)
    return jnp.where((lane >= h * HEAD_DIM) & (lane < (h + 1) * HEAD_DIM), 1.0, 0.0).astype(dtype)


def _low_half():
    return lax.broadcasted_iota(jnp.int32, (1, LANES), 1) < HEAD_DIM


def _swap_halves(x):
    return pltpu.roll(x, HEAD_DIM, 1)


def _augment_heads(v):
    low = _low_half()
    parts = []
    for j in range(v.shape[1] // LANES):
        pair = v[:, j * LANES:(j + 1) * LANES]
        parts += [jnp.where(low, pair, 1.0), jnp.where(low, 1.0, pair)]
    return jnp.concatenate(parts, axis=1)


def _shared_kv_variants(kv):
    low = _low_half()
    sw = _swap_halves(kv)
    k2 = jnp.concatenate([jnp.where(low, kv, 0.0), jnp.where(low, 0.0, sw)], axis=1)
    v2 = jnp.concatenate([jnp.where(low, sw, 1.0), jnp.where(low, 1.0, kv)], axis=1)
    return k2, v2


def _normalise_pair(acc_even, acc_odd):
    return jnp.where(_low_half(), acc_even / _swap_halves(acc_even), acc_odd / _swap_halves(acc_odd))


_IN_MAIN_WIDTH = 3072
_IN_SRC = {"qa": (True, 0), "ka": (True, 256), "va": (True, 512), "az": (True, 768),
           "qb": (True, 1024), "kb": (True, 1280), "vb": (True, 1536), "bz": (True, 1792),
           "cu": (True, 2048), "cz": (True, 2304), "dq": (True, 2560), "kvcs": (True, 2816),
           "kvwg": (False, 0), "dz": (False, 256), "eq": (False, 512), "ez": (False, 768)}
_IN_TAIL_WIDTH = 1024
_IN_SCALED = ("qb", "dq", "eq")
_IN_OUTS = (("qa", 256, BF16), ("ka", 256, BF16), ("va", 512, BF16),
            ("qb", 256, BF16), ("kb", 256, BF16), ("vb", 512, BF16),
            ("cu", 256, F32), ("dq", 256, BF16), ("dqr", 256, BF16), ("kvc", 128, F32),
            ("ks", 256, BF16), ("vs", 256, BF16), ("kw", 256, BF16), ("vw", 256, BF16),
            ("dg", 128, F32), ("eq", 256, BF16), ("zs", 1280, F32))


def _rotate_half(y, group):
    half = group // 2
    lane = lax.broadcasted_iota(jnp.int32, (1, LANES), 1)
    first = (lane & (group - 1)) < half
    parts = []
    for j in range(y.shape[1] // LANES):
        v = y[:, j * LANES:(j + 1) * LANES]
        parts.append(jnp.where(first, pltpu.roll(v, LANES - half, 1), pltpu.roll(v, half, 1)))
    return jnp.concatenate(parts, axis=1)


def _in_proj_kernel(x_ref, g_ref, wm_ref, wt_ref, cosa_ref, sina_ref, cosb_ref, sinb_ref, cosk_ref, sink_ref,
                    *out_refs):
    out = dict(zip([n for n, _, _ in _IN_OUTS], out_refs))
    h = _rms(x_ref[...], g_ref[...]).astype(BF16)

    def proj(name):
        main, off = _IN_SRC[name]
        y = _dot(h, (wm_ref if main else wt_ref)[:, off:off + BRANCH_WIDTH].astype(BF16))
        return y * HEAD_DIM ** -0.5 if name in _IN_SCALED else y

    def rope(y, cos_ref, sin_ref, group):
        return y * cos_ref[...] + _rotate_half(y, group) * sin_ref[...]

    def put(name, y):
        out[name][...] = y.astype(out[name].dtype)

    put("qa", rope(proj("qa"), cosa_ref, sina_ref, DIFF_QK_DIM))
    put("ka", rope(proj("ka"), cosa_ref, sina_ref, DIFF_QK_DIM))
    put("va", _augment_heads(proj("va")))
    put("qb", rope(proj("qb"), cosb_ref, sinb_ref, HEAD_DIM))
    put("kb", rope(proj("kb"), cosb_ref, sinb_ref, HEAD_DIM))
    put("vb", _augment_heads(proj("vb")))
    put("cu", proj("cu"))
    dq = proj("dq")
    put("dq", dq)
    put("dqr", rope(dq, cosb_ref, sinb_ref, HEAD_DIM))
    kvcs = proj("kvcs")
    kvwg = proj("kvwg")
    put("kvc", kvcs[:, :LANES])
    put("dg", kvwg[:, LANES:])
    for kv, k_name, v_name in ((kvcs[:, LANES:], "ks", "vs"), (kvwg[:, :LANES], "kw", "vw")):
        k2, v2 = _shared_kv_variants(rope(kv, cosk_ref, sink_ref, HEAD_DIM))
        put(k_name, k2)
        put(v_name, v2)
    put("eq", proj("eq"))
    for n, name in enumerate(("az", "bz", "cz", "dz", "ez")):
        z = proj(name)
        out["zs"][:, n * BRANCH_WIDTH:(n + 1) * BRANCH_WIDTH] = z * jax.nn.sigmoid(z)


def _in_tail(w):
    used = LANES + 3 * N_HEADS
    tail = w[:, _IN_MAIN_WIDTH:]
    return jnp.concatenate([jnp.pad(tail[:, :used], ((0, 0), (0, BRANCH_WIDTH - used))), tail[:, used:]], axis=1)


def _rope_tables(s_len):
    def table(group, width):
        half = group // 2
        inv_freq = ROPE_THETA ** (-jnp.arange(half, dtype=F32) / half)
        ang = jnp.arange(s_len, dtype=F32)[:, None] * inv_freq[None, :]
        cos = jnp.tile(jnp.cos(ang), (1, width // half))
        sin = jnp.tile(jnp.concatenate([-jnp.sin(ang), jnp.sin(ang)], axis=1), (1, width // group))
        return cos, sin
    cos_a, sin_a = table(DIFF_QK_DIM, BRANCH_WIDTH)
    cos_b, sin_b = table(HEAD_DIM, BRANCH_WIDTH)
    cos_k = jnp.concatenate([cos_b[:, :HEAD_DIM], jnp.ones((s_len, HEAD_DIM), F32)], axis=1)
    sin_k = jnp.concatenate([sin_b[:, :HEAD_DIM], jnp.zeros((s_len, HEAD_DIM), F32)], axis=1)
    return cos_a, sin_a, cos_b, sin_b, cos_k, sin_k


def _in_proj(x2, g, w_in, layer, tables, s_len):
    t, d = x2.shape
    tm = ROW_TILE
    nsb = s_len // tm
    w_tail = _in_tail(w_in[layer])
    assert w_tail.shape[1] == _IN_TAIL_WIDTH
    row = lambda w: pl.BlockSpec((tm, w), lambda i: (i, 0))
    tab = lambda a: pl.BlockSpec((tm, a.shape[1]), lambda i: (i % nsb, 0))
    w_main = pl.BlockSpec((None, d, _IN_MAIN_WIDTH), lambda i: (layer, 0, 0), pipeline_mode=pl.Buffered(1))
    return pl.pallas_call(
        _in_proj_kernel,
        grid=(t // tm,),
        in_specs=[row(d), _const_spec((1, d)), w_main, _const_spec(w_tail.shape)] + [tab(a) for a in tables],
        out_specs=[row(w) for _, w, _ in _IN_OUTS],
        out_shape=[jax.ShapeDtypeStruct((t, w), dt) for _, w, dt in _IN_OUTS],
        compiler_params=pltpu.CompilerParams(dimension_semantics=("arbitrary",),
                                             vmem_limit_bytes=BIG_VMEM_LIMIT),
        name="in_proj",
    )(x2, g.reshape(1, d), w_in, w_tail, *tables)


def _lane_fold(x, op):
    parts = [x[:, j * LANES:(j + 1) * LANES] for j in range(x.shape[1] // LANES)]
    return functools.reduce(op, parts)


def _strip_attention(chains, tq, kb_lo, kb_hi, bias_fn, last_bias, scratch, exp_scale=1.0):
    s_ref, m_ref, acc_ref = scratch
    n = len(chains)
    reps = tq // LANES
    c1 = exp_scale * math.log2(math.e)

    def tile(cache, ref, col, width, kb, ntiles=1):
        key = (id(ref), col, width)
        if key not in cache:
            cache[key] = ref[pl.ds(pl.multiple_of(kb * tq, tq), ntiles * tq), col:col + width]
        return cache[key]

    def scores(kb, extra):
        bias = bias_fn(kb) if bias_fn is not None else None
        if extra is not None:
            bias = extra if bias is None else bias + extra
        cache, out = {}, []
        for q, (k_ref, k_col), _ in chains:
            s = _dot_nt(q, tile(cache, k_ref, k_col, q.shape[1], kb)) * c1
            out.append(s if bias is None else s + bias)
        return out

    def tile_pairs(lo, hi, step):
        def two(j, carry):
            step(lo + 2 * j, 2)
            return carry
        cnt = hi - lo
        lax.fori_loop(0, lax.shift_right_arithmetic(cnt, 1), two, 0)

        @pl.when((cnt & 1) == 1)
        def _():
            step(hi - 1, 1)

    m_ref[0:n] = jnp.full((n,) + m_ref.shape[1:], NEG_INF, F32)

    def pass1(kb, ntiles):
        tiles = [scores(kb + j, None) for j in range(ntiles)]
        for i in range(n):
            m = m_ref[i]
            for j in range(ntiles):
                s_ref[i, kb + j] = tiles[j][i]
                m = jnp.maximum(m, _lane_fold(tiles[j][i], jnp.maximum))
            m_ref[i] = m

    tile_pairs(kb_lo, kb_hi - 1, pass1)
    for i, s in enumerate(scores(kb_hi - 1, last_bias)):
        s_ref[i, kb_hi - 1] = s
        m = jnp.max(jnp.maximum(m_ref[i], _lane_fold(s, jnp.maximum)), axis=-1, keepdims=True)
        m_ref[i] = jnp.broadcast_to(m, m_ref.shape[1:])
        acc_ref[i] = jnp.zeros(acc_ref.shape[1:], F32)

    def pass2(kb, ntiles):
        cache = {}
        for i, (_, _, (v_ref, v_col)) in enumerate(chains):
            m = jnp.concatenate([m_ref[i]] * reps, axis=1)
            p = [jnp.exp2(s_ref[i, kb + j] - m).astype(BF16) for j in range(ntiles)]
            p = p[0] if ntiles == 1 else jnp.concatenate(p, axis=1)
            acc_ref[i] += _dot(p, tile(cache, v_ref, v_col, LANES, kb, ntiles))

    tile_pairs(kb_lo, kb_hi, pass2)
    return [acc_ref[i] for i in range(n)]


def _strip_scratch(n, tq, s_len):
    return [pltpu.VMEM((n, s_len // tq, tq, tq), F32), pltpu.VMEM((n, tq, LANES), F32),
            pltpu.VMEM((n, tq, LANES), F32)]


def _causal_bias(tq):
    r = lax.broadcasted_iota(jnp.int32, (tq, tq), 0)
    c = lax.broadcasted_iota(jnp.int32, (tq, tq), 1)
    return jnp.where(c <= r, 0.0, NEG_INF).astype(F32)


def _att_specs(tq, s_len, k_width, v_width):
    q_spec = pl.BlockSpec((None, tq, BRANCH_WIDTH), lambda b, i: (b, i, 0))
    kv = lambda w: pl.BlockSpec((None, s_len, w), lambda b, i: (b, 0, 0))
    return q_spec, kv(k_width), kv(v_width)


def _diff_kernel(lam_ref, q_ref, k_ref, v_ref, g_ref, hm_ref, o_ref, *scratch, tq, out_scale):
    qi = pl.program_id(1)
    q = q_ref[...]
    lane = lax.broadcasted_iota(jnp.int32, (1, BRANCH_WIDTH), 1)
    lam = lam_ref[0]
    chains = []
    for hc in range(2 * N_HEADS):
        lo = hc * DIFF_QK_DIM
        cmask = jnp.where((lane >= lo) & (lane < lo + DIFF_QK_DIM), 1.0, 0.0).astype(BF16)
        chains.append((q * cmask, (k_ref, 0), (v_ref, (hc // 2) * LANES)))
    acc = _strip_attention(chains, tq, 0, qi + 1, None, _causal_bias(tq), scratch,
                           exp_scale=DIFF_QK_DIM ** -0.5)
    halves = []
    for pair in range(N_HEADS // 2):
        even, odd = 4 * pair, 4 * pair + 2
        halves.append(_normalise_pair(acc[even], acc[odd]) - lam * _normalise_pair(acc[even + 1], acc[odd + 1]))
    o = jnp.concatenate(halves, axis=1)
    ms = _dot_hilo(o * o, hm_ref[...])
    o_ref[...] = o * lax.rsqrt(ms + RMS_EPS) * g_ref[...] * out_scale


def _diff_attention(q, k, v_aug, lam, subln_g, lam_init):
    b, s_len, w = q.shape
    tq = ATT_TILE
    q_spec, k_spec, v_spec = _att_specs(tq, s_len, w, v_aug.shape[-1])
    head = np.arange(w) // HEAD_DIM
    hm = jnp.asarray((head[:, None] == head[None, :]) / HEAD_DIM, dtype=BF16)
    g = jnp.tile(subln_g.astype(F32), N_HEADS).reshape(1, w)
    return pl.pallas_call(
        functools.partial(_diff_kernel, tq=tq, out_scale=1.0 - lam_init),
        grid=(b, s_len // tq),
        in_specs=[pl.BlockSpec(memory_space=pltpu.SMEM), q_spec, k_spec, v_spec,
                  _const_spec((1, w)), _const_spec((w, w))],
        out_specs=q_spec,
        out_shape=jax.ShapeDtypeStruct((b, s_len, w), F32),
        scratch_shapes=_strip_scratch(2 * N_HEADS, tq, s_len),
        compiler_params=pltpu.CompilerParams(dimension_semantics=("arbitrary", "arbitrary"),
                                             vmem_limit_bytes=BIG_VMEM_LIMIT),
        name="diff_attention",
    )(lam.reshape(1), q, k, v_aug, g, hm)


def _dil_kernel(q_ref, k_ref, v_ref, bias_ref, o_ref, *scratch, tq):
    qi = pl.program_id(1)
    q = q_ref[...]
    chains = [(q * _head_mask(h, BF16), (k_ref, 0), (v_ref, h * LANES)) for h in range(N_HEADS)]
    acc = _strip_attention(chains, tq, 0, qi + 1, lambda kb: bias_ref[qi - kb], None, scratch)
    o_ref[...] = jnp.concatenate([_normalise_pair(acc[0], acc[1]), _normalise_pair(acc[2], acc[3])], axis=1)


def _dilated_bias(tq, s_len):
    nq = s_len // tq
    d = (np.arange(nq)[:, None, None] * tq + np.arange(tq)[None, :, None] - np.arange(tq)[None, None, :])
    count = np.zeros(d.shape, np.float64)
    for window, dil in DIL_PATTERNS:
        count += (d >= 0) & (d <= window) & (d % dil == 0)
    return np.where(count > 0, np.log2(np.maximum(count, 1.0)), NEG_INF).astype(np.float32)


def _dilated_attention(q, k, v_aug):
    b, s_len, w = q.shape
    tq = ATT_TILE
    q_spec, k_spec, v_spec = _att_specs(tq, s_len, w, v_aug.shape[-1])
    bias = jnp.asarray(_dilated_bias(tq, s_len))
    return pl.pallas_call(
        functools.partial(_dil_kernel, tq=tq),
        grid=(b, s_len // tq),
        in_specs=[q_spec, k_spec, v_spec, _const_spec(bias.shape)],
        out_specs=q_spec,
        out_shape=jax.ShapeDtypeStruct((b, s_len, w), F32),
        scratch_shapes=_strip_scratch(N_HEADS, tq, s_len),
        compiler_params=pltpu.CompilerParams(dimension_semantics=("arbitrary", "arbitrary"),
                                             vmem_limit_bytes=BIG_VMEM_LIMIT),
        name="dilated_attention",
    )(q, k, v_aug, bias)


def _s5_kernel(u_ref, bm_ref, cm_ref, are_ref, aim_ref, d_ref, wg_ref, bg_ref, o_ref, st_ref, *xs_refs,
               ts, nb):
    n = S5_GROUPS * S5_STATE
    nc = n // LANES

    @pl.when(pl.program_id(0) == 0)
    def _():
        st_ref[...] = jnp.zeros(st_ref.shape, F32)

    u = u_ref[...].reshape(nb * ts, u_ref.shape[-1])
    bu = _dot(u.astype(BF16), bm_ref[...])
    for c, x_ref in enumerate(xs_refs):
        for b in range(nb):
            x_ref[pl.ds(b, ts, stride=nb), :] = bu[b * ts:(b + 1) * ts, c * LANES:(c + 1) * LANES]
    a_re = jnp.broadcast_to(are_ref[...], (nb, n))
    a_im = jnp.broadcast_to(aim_ref[...], (nb, n))

    def step(t, carry):
        x_re, x_im = carry
        rows = pl.ds(pl.multiple_of(t * nb, nb), nb)
        bu_re = jnp.concatenate([x_ref[rows, :] for x_ref in xs_refs[:nc]], axis=1)
        bu_im = jnp.concatenate([x_ref[rows, :] for x_ref in xs_refs[nc:]], axis=1)
        n_re = a_re * x_re - a_im * x_im + bu_re
        n_im = a_re * x_im + a_im * x_re + bu_im
        for c in range(nc):
            xs_refs[c][rows, :] = n_re[:, c * LANES:(c + 1) * LANES]
            xs_refs[nc + c][rows, :] = n_im[:, c * LANES:(c + 1) * LANES]
        return n_re, n_im

    x_re, x_im = lax.fori_loop(0, ts, step, (st_ref[:, 0:n], st_ref[:, n:2 * n]))
    st_ref[:, 0:n] = x_re
    st_ref[:, n:2 * n] = x_im

    xs = jnp.concatenate(
        [jnp.concatenate([x_ref[pl.ds(b, ts, stride=nb), :].astype(BF16) for x_ref in xs_refs], axis=1)
         for b in range(nb)], axis=0)
    y = _dot(xs, cm_ref[...]) + d_ref[...] * u
    t = _dot(jax.nn.gelu(y).astype(BF16), wg_ref[...]) + bg_ref[...]
    o_ref[...] = (t[:, :BRANCH_WIDTH] * jax.nn.sigmoid(t[:, BRANCH_WIDTH:])).reshape(o_ref.shape)


def _s5_params(lam_re, lam_im, log_dt, b_re, b_im, c_re, c_im):
    g, n, p = S5_GROUPS, S5_STATE, S5_GROUP
    lr, li = lam_re.astype(F32), lam_im.astype(F32)
    dt = jnp.exp(log_dt.astype(F32))[:, None]
    mag = jnp.exp(lr * dt)
    a_re, a_im = mag * jnp.cos(li * dt), mag * jnp.sin(li * dt)
    den = lr * lr + li * li
    n_re, n_im = a_re - 1.0, a_im
    z_re = (n_re * lr + n_im * li) / den
    z_im = (n_im * lr - n_re * li) / den
    br, bi = b_re.astype(F32), b_im.astype(F32)
    bb_re = z_re[..., None] * br - z_im[..., None] * bi
    bb_im = z_re[..., None] * bi + z_im[..., None] * br
    eye = jnp.eye(g, dtype=F32)
    blockdiag_in = lambda t: jnp.einsum("gnp,gh->gphn", t, eye).reshape(g * p, g * n)
    blockdiag_out = lambda t: jnp.einsum("gpn,gh->gnhp", t, eye).reshape(g * n, g * p)
    bm = jnp.concatenate([blockdiag_in(bb_re), blockdiag_in(bb_im)], axis=1)
    cm = jnp.concatenate([blockdiag_out(c_re.astype(F32)), -blockdiag_out(c_im.astype(F32))], axis=0)
    return bm.astype(BF16), cm.astype(BF16), a_re.reshape(1, g * n), a_im.reshape(1, g * n)


def _s5_branch(u, params, d_skip, w_glu, b_glu):
    nb, s_len, w = u.shape
    assert nb == 8
    ts = S5_CHUNK
    bm, cm, a_re, a_im = params
    n2 = bm.shape[1]
    blk = pl.BlockSpec((nb, ts, w), lambda i: (0, i, 0))
    return pl.pallas_call(
        functools.partial(_s5_kernel, ts=ts, nb=nb),
        grid=(s_len // ts,),
        in_specs=[blk, _const_spec(bm.shape), _const_spec(cm.shape), _const_spec(a_re.shape),
                  _const_spec(a_im.shape), _const_spec((1, w)), _const_spec(w_glu.shape),
                  _const_spec((1, 2 * w))],
        out_specs=blk,
        out_shape=jax.ShapeDtypeStruct((nb, s_len, w), F32),
        scratch_shapes=[pltpu.VMEM((nb, n2), F32)] + [pltpu.VMEM((ts * nb, LANES), F32)] * (n2 // LANES),
        compiler_params=pltpu.CompilerParams(dimension_semantics=("arbitrary",),
                                             vmem_limit_bytes=BIG_VMEM_LIMIT),
        name="s5_scan",
    )(u, bm, cm, a_re, a_im, d_skip.astype(F32).reshape(1, w), w_glu.astype(BF16),
      b_glu.astype(F32).reshape(1, 2 * w))


def _compress_kernel(r_ref, pe_top_ref, pe_bot_ref, w_top_ref, w_bot_ref, w2_ref, k_ref, v_ref):
    r = r_ref[...]
    top = (r + pe_top_ref[...]).astype(BF16)
    nxt = pltpu.roll(r, r.shape[0] - 1, 0)
    bot = (nxt + pe_bot_ref[...]).astype(BF16)
    hid = jax.nn.gelu(_dot(top, w_top_ref[...]) + _dot(bot, w_bot_ref[...]))
    k2, v2 = _shared_kv_variants(_dot(hid.astype(BF16), w2_ref[...]))
    k_ref[...] = k2.astype(BF16)
    v_ref[...] = v2.astype(BF16)


def _compress(kvc, pe, w1, w2):
    b, s_len, _ = kvc.shape
    nr = s_len // CMP_STRIDE
    per = CMP_BLOCK // CMP_STRIDE
    assert per == 2
    hid = w1.shape[-1]
    r = kvc.reshape(b, nr, CMP_STRIDE * 2 * HEAD_DIM)
    w1r = w1.astype(F32).reshape(2, per, CMP_STRIDE, HEAD_DIM, hid)
    per_r = pe.astype(F32).reshape(2, per, CMP_STRIDE, HEAD_DIM)

    def expand(j):
        wk = jnp.pad(w1r[0, j], ((0, 0), (0, HEAD_DIM), (0, hid)))
        wv = jnp.pad(w1r[1, j], ((0, 0), (HEAD_DIM, 0), (hid, 0)))
        return (wk + wv).reshape(CMP_STRIDE * 2 * HEAD_DIM, 2 * hid).astype(BF16)

    pe_rows = [jnp.concatenate([per_r[0, j], per_r[1, j]], axis=-1).reshape(1, -1) for j in range(per)]
    z = jnp.zeros((hid, HEAD_DIM), F32)
    w2f = w2.astype(F32)
    w2x = jnp.concatenate([jnp.concatenate([w2f[0], z], axis=1),
                           jnp.concatenate([z, w2f[1]], axis=1)], axis=0).astype(BF16)
    w_top, w_bot = expand(0), expand(1)
    blk = pl.BlockSpec((None, nr, r.shape[-1]), lambda i: (i, 0, 0))
    out = pl.BlockSpec((None, nr, BRANCH_WIDTH), lambda i: (i, 0, 0))
    return pl.pallas_call(
        _compress_kernel,
        grid=(b,),
        in_specs=[blk, _const_spec(pe_rows[0].shape), _const_spec(pe_rows[1].shape),
                  _const_spec(w_top.shape), _const_spec(w_bot.shape), _const_spec(w2x.shape)],
        out_specs=[out, out],
        out_shape=[jax.ShapeDtypeStruct((b, nr, BRANCH_WIDTH), BF16)] * 2,
        compiler_params=pltpu.CompilerParams(dimension_semantics=("arbitrary",)),
        name="nsa_compress",
    )(r, pe_rows[0], pe_rows[1], w_top, w_bot, w2x)


def _nsa_kernel(dq_ref, dqr_ref, dg_ref, kc_ref, vc_ref, ks_ref, vs_ref, kw_ref, vw_ref,
                ovt_ref, exp_ref, gsel_ref, wbias_ref, o_ref, *scratch, tq, n_cmp, n_slc, n_sel):
    qi = pl.program_id(1)
    t0 = qi * tq
    nr = kc_ref.shape[0]
    low = _low_half()
    q_cols = lambda h: slice((h // 2) * LANES, (h // 2 + 1) * LANES)
    kv_col = lambda h: (h % 2) * LANES

    dq = dq_ref[...]
    tpos = t0 + lax.broadcasted_iota(jnp.int32, (tq, nr), 0)
    ci = lax.broadcasted_iota(jnp.int32, (tq, nr), 1)
    cmask = (ci * CMP_STRIDE + (CMP_BLOCK - 1) <= tpos) & (ci < n_cmp)
    p_sum = jnp.zeros((tq, nr), F32)
    o_heads = []
    for h in range(N_HEADS):
        s = jnp.where(cmask, _dot_nt(dq[:, q_cols(h)], kc_ref[:, kv_col(h):kv_col(h) + LANES]), NEG_INF)
        m = jnp.max(s, axis=-1, keepdims=True)
        e = jnp.where(cmask, jnp.exp(s - m), 0.0)
        p = e / jnp.maximum(jnp.sum(e, axis=-1, keepdims=True), 1e-30)
        p_sum = p_sum + p
        o_heads.append(_dot(p.astype(BF16), vc_ref[:, kv_col(h):kv_col(h) + LANES]))
    o_cmp = jnp.concatenate([jnp.where(low, o_heads[0], o_heads[1]), jnp.where(low, o_heads[2], o_heads[3])],
                            axis=1)

    p_hi, p_lo = _split_hi_lo(p_sum)
    ovt = ovt_ref[...]
    imp_t = _dot_nt(ovt, p_hi) + _dot_nt(ovt, p_lo)
    blk = lax.broadcasted_iota(jnp.int32, (n_slc, tq), 0)
    qblk = lax.shift_right_arithmetic(t0 + lax.broadcasted_iota(jnp.int32, (n_slc, tq), 1),
                                      int(math.log2(SLC_BLOCK)))
    forced = jnp.where(blk == 0, 1, jnp.where(blk == qblk, 1, jnp.where(blk == qblk - 1, 1, 0)))
    score = jnp.where(blk <= qblk, jnp.where(forced > 0, FORCE_SCORE, imp_t), NEG_INF)
    rank = jnp.zeros((n_slc, tq), F32)
    for i in range(n_slc):
        si = score[i:i + 1, :]
        tie = jnp.where(blk > i, 1.0, 0.0)
        rank = rank + jnp.where(si > score, 1.0, jnp.where(si == score, tie, 0.0))
    sel_bias_t = jnp.where(rank < n_sel, jnp.where(score > 0.5 * NEG_INF, 0.0, NEG_INF), NEG_INF)
    pad_rows = exp_ref.shape[1] - n_slc
    sel_bias = jnp.concatenate([sel_bias_t, jnp.full((pad_rows, tq), NEG_INF, F32)], axis=0).T.astype(BF16)

    dqr = dqr_ref[...]
    pairs = lambda acc: jnp.concatenate([_normalise_pair(acc[0], acc[1]), _normalise_pair(acc[2], acc[3])], axis=1)
    chains = [(dqr[:, q_cols(h)], (ks_ref, kv_col(h)), (vs_ref, kv_col(h))) for h in range(N_HEADS)]
    o_slc = pairs(_strip_attention(chains, tq, 0, qi + 1, lambda kb: _dot(sel_bias, exp_ref[kb]),
                                   _causal_bias(tq), scratch))
    chains = [(dqr[:, q_cols(h)], (kw_ref, kv_col(h)), (vw_ref, kv_col(h))) for h in range(N_HEADS)]
    win_lo = jnp.maximum(qi - (wbias_ref.shape[0] - 1), 0)
    o_win = pairs(_strip_attention(chains, tq, win_lo, qi + 1, lambda kb: wbias_ref[qi - kb], None, scratch))

    gates = _dot_hilo(jax.nn.sigmoid(dg_ref[...]), gsel_ref[...])
    w = BRANCH_WIDTH
    o_ref[...] = gates[:, :w] * o_cmp + gates[:, w:2 * w] * o_slc + gates[:, 2 * w:] * o_win


def _nsa_constants(tq, s_len):
    n_cmp = (s_len - CMP_BLOCK) // CMP_STRIDE + 1
    n_slc = s_len // SLC_BLOCK
    nr = s_len // CMP_STRIDE
    c0 = np.arange(n_cmp)[:, None] * CMP_STRIDE
    s0 = np.arange(n_slc)[None, :] * SLC_BLOCK
    overlap = np.clip(np.minimum(c0 + CMP_BLOCK, s0 + SLC_BLOCK) - np.maximum(c0, s0), 0, None) / CMP_STRIDE
    ovt = np.zeros((n_slc, nr), np.float32)
    ovt[:, :n_cmp] = overlap.T
    rows = -(-n_slc // 128) * 128
    expand = np.zeros((s_len // tq, rows, tq), np.float32)
    tok = np.arange(s_len)
    expand[tok // tq, tok // SLC_BLOCK, tok % tq] = 1.0
    gsel = np.zeros((128, 3 * BRANCH_WIDTH), np.float32)
    for h in range(N_HEADS):
        for j in range(3):
            gsel[h * 3 + j, j * BRANCH_WIDTH + h * HEAD_DIM:j * BRANCH_WIDTH + (h + 1) * HEAD_DIM] = 1.0
    nwin = -(-WINDOW // tq) + 1
    d = np.arange(nwin)[:, None, None] * tq + np.arange(tq)[None, :, None] - np.arange(tq)[None, None, :]
    wbias = np.where((d >= 0) & (d < WINDOW), 0.0, NEG_INF).astype(np.float32)
    return n_cmp, n_slc, ovt, expand, gsel, wbias


def _nsa_attention(dq, dqr, dg, k_cmp, v_cmp, ks, vs, kw, vw):
    b, s_len, w = dq.shape
    tq = ATT_TILE
    n_cmp, n_slc, ovt, expand, gsel, wbias = _nsa_constants(tq, s_len)
    q_spec, kv_spec, _ = _att_specs(tq, s_len, w, w)
    nr = k_cmp.shape[1]
    cmp_spec = pl.BlockSpec((None, nr, w), lambda bi, i: (bi, 0, 0))
    g_spec = pl.BlockSpec((None, tq, dg.shape[-1]), lambda bi, i: (bi, i, 0))
    consts = [jnp.asarray(ovt, BF16), jnp.asarray(expand, BF16), jnp.asarray(gsel, BF16), jnp.asarray(wbias)]
    return pl.pallas_call(
        functools.partial(_nsa_kernel, tq=tq, n_cmp=n_cmp, n_slc=n_slc, n_sel=min(N_SELECT, n_slc)),
        grid=(b, s_len // tq),
        in_specs=[q_spec, q_spec, g_spec, cmp_spec, cmp_spec, kv_spec, kv_spec, kv_spec, kv_spec]
        + [_const_spec(c.shape) for c in consts],
        out_specs=q_spec,
        out_shape=jax.ShapeDtypeStruct((b, s_len, w), F32),
        scratch_shapes=_strip_scratch(N_HEADS, tq, s_len),
        compiler_params=pltpu.CompilerParams(dimension_semantics=("arbitrary", "arbitrary"),
                                             vmem_limit_bytes=BIG_VMEM_LIMIT),
        name="nsa_attention",
    )(dq, dqr, dg, k_cmp, v_cmp, ks, vs, kw, vw, *consts)


def _mem_kv_kernel(m_ref, g_ref, w_ref, k_ref, v_ref):
    kv = _dot(_rms(m_ref[...], g_ref[...]).astype(BF16), w_ref[...])
    k_ref[...] = kv[:, :BRANCH_WIDTH].astype(BF16)
    v_ref[...] = kv[:, BRANCH_WIDTH:].astype(BF16)


def _mem_kv(mem2, g, w_kv):
    rows, d = mem2.shape
    tm = ROW_TILE
    out = pl.BlockSpec((tm, BRANCH_WIDTH), lambda i: (i, 0))
    return pl.pallas_call(
        _mem_kv_kernel,
        grid=(rows // tm,),
        in_specs=[pl.BlockSpec((tm, d), lambda i: (i, 0)), _const_spec((1, d)), _const_spec(w_kv.shape)],
        out_specs=[out, out],
        out_shape=[jax.ShapeDtypeStruct((rows, BRANCH_WIDTH), BF16)] * 2,
        compiler_params=pltpu.CompilerParams(dimension_semantics=("arbitrary",)),
        name="mem_kv",
    )(mem2, g.astype(F32).reshape(1, d), w_kv.astype(BF16))


def _mem_attn_kernel(q_ref, k_ref, v_ref, o_ref):
    q = q_ref[...]
    k = k_ref[...]
    v = v_ref[...]
    o = jnp.zeros(o_ref.shape, F32)
    for h in range(N_HEADS):
        s = _dot_nt(q * _head_mask(h, BF16), k)
        e = jnp.exp(s - jnp.max(s, axis=-1, keepdims=True))
        oh = _dot(e.astype(BF16), v) / jnp.sum(e, axis=-1, keepdims=True)
        o = jnp.where(_head_mask(h, F32) > 0.5, oh, o)
    o_ref[...] = o


def _mem_attention(q, k, v):
    b, s_len, w = q.shape
    tq = min(MEM_ATT_TILE, s_len)
    q_spec = pl.BlockSpec((None, tq, w), lambda bi, i: (bi, i, 0))
    kv_spec = pl.BlockSpec((None, k.shape[1], w), lambda bi, i: (bi, 0, 0))
    return pl.pallas_call(
        _mem_attn_kernel,
        grid=(b, s_len // tq),
        in_specs=[q_spec, kv_spec, kv_spec],
        out_specs=q_spec,
        out_shape=jax.ShapeDtypeStruct((b, s_len, w), F32),
        compiler_params=pltpu.CompilerParams(dimension_semantics=("arbitrary", "arbitrary")),
        name="mem_attention",
    )(q, k, v)


def _merge_kernel(x_ref, g_ref, oa_ref, ob_ref, oc_ref, od_ref, oe_ref, zs_ref, wm_ref, bm_ref, wb_ref,
                  wo_ref, fg_ref, o_ref, *, final_norm):
    x = x_ref[...]
    d = x.shape[-1]
    w = BRANCH_WIDTH
    h = _rms(x, g_ref[...]).astype(BF16)
    mixed = jnp.zeros(x.shape, F32)
    for n, br_ref in enumerate((oa_ref, ob_ref, oc_ref, od_ref, oe_ref)):
        br = (br_ref[...] * zs_ref[:, n * w:(n + 1) * w]).astype(BF16)
        y = _dot(br, wb_ref[n].astype(BF16))
        gate = jax.nn.sigmoid(_dot(h, wm_ref[:, n * d:(n + 1) * d].astype(BF16)) + bm_ref[:, n * d:(n + 1) * d])
        mixed = mixed + gate * y
    out = x + _dot(mixed.astype(BF16), wo_ref[...].astype(BF16))
    if final_norm:
        out = _rms(out, fg_ref[...])
    o_ref[...] = out


def _merge(x2, g, branches, zs, w_merge, b_merge, w_branch, w_out, layer, final_g, final_norm):
    t, d = x2.shape
    tm = MERGE_ROW_TILE
    row = lambda wd: pl.BlockSpec((tm, wd), lambda i: (i, 0))

    def of_layer(a):
        nd = a.ndim - 1
        return pl.BlockSpec((None,) + a.shape[1:], lambda i: (layer,) + (0,) * nd, pipeline_mode=pl.Buffered(1))

    b_merge = b_merge.astype(F32).reshape(b_merge.shape[0], 1, -1)
    return pl.pallas_call(
        functools.partial(_merge_kernel, final_norm=final_norm),
        grid=(t // tm,),
        in_specs=[row(d), _const_spec((1, d))] + [row(BRANCH_WIDTH)] * N_BRANCHES + [row(zs.shape[1])]
        + [of_layer(w_merge), of_layer(b_merge), of_layer(w_branch), of_layer(w_out), _const_spec((1, d))],
        out_specs=row(d),
        out_shape=jax.ShapeDtypeStruct((t, d), F32),
        compiler_params=pltpu.CompilerParams(dimension_semantics=("arbitrary",),
                                             vmem_limit_bytes=BIG_VMEM_LIMIT),
        name="merge",
    )(x2, g.astype(F32).reshape(1, d), *branches, zs, w_merge.astype(F32), b_merge, w_branch.astype(F32),
      w_out.astype(F32), final_g.astype(F32).reshape(1, d))


def kernel(x, mem, norm_g, w_in, diff_lambda, diff_subln_g, s5_lambda_re, s5_lambda_im, s5_log_dt,
           s5_b_re, s5_b_im, s5_c_re, s5_c_im, s5_d, w_glu, b_glu, nsa_pe, nsa_w1, nsa_w2, mem_norm_g,
           w_mem_kv, w_merge, b_merge, w_branch, w_out, final_g):
    bsz, s_len, d = x.shape
    depth = w_in.shape[0]
    t = bsz * s_len
    w = BRANCH_WIDTH
    tables = _rope_tables(s_len)
    x2 = x.astype(F32).reshape(t, d)
    mem2 = mem.astype(F32).reshape(-1, d)
    for l in range(depth):
        proj = dict(zip([n for n, _, _ in _IN_OUTS],
                        _in_proj(x2, norm_g[l].astype(F32), w_in.astype(F32), l, tables, s_len)))
        seq = lambda name: proj[name].reshape(bsz, s_len, -1)

        dl = diff_lambda[l].astype(F32)
        lam_init = 0.8 - 0.6 * math.exp(-0.3 * l)
        lam = jnp.exp(jnp.sum(dl[0] * dl[1])) - jnp.exp(jnp.sum(dl[2] * dl[3])) + lam_init
        o_a = _diff_attention(seq("qa"), seq("ka"), seq("va"), lam, diff_subln_g[l], lam_init)

        o_b = _dilated_attention(seq("qb"), seq("kb"), seq("vb"))

        s5p = _s5_params(s5_lambda_re[l], s5_lambda_im[l], s5_log_dt[l], s5_b_re[l], s5_b_im[l],
                         s5_c_re[l], s5_c_im[l])
        o_c = _s5_branch(seq("cu"), s5p, s5_d[l], w_glu[l], b_glu[l])

        k_cmp, v_cmp = _compress(seq("kvc"), nsa_pe[l], nsa_w1[l], nsa_w2[l])
        o_d = _nsa_attention(seq("dq"), seq("dqr"), seq("dg"), k_cmp, v_cmp,
                             seq("ks"), seq("vs"), seq("kw"), seq("vw"))

        k_mem, v_mem = _mem_kv(mem2, mem_norm_g[l], w_mem_kv[l])
        o_e = _mem_attention(seq("eq"), k_mem.reshape(bsz, -1, w), v_mem.reshape(bsz, -1, w))

        branches = [o.reshape(t, w) for o in (o_a, o_b, o_c, o_d, o_e)]
        x2 = _merge(x2, norm_g[l], branches, proj["zs"], w_merge, b_merge, w_branch, w_out, l,
                    final_g, final_norm=(l == depth - 1))
    return x2.reshape(bsz, s_len, d).astype(x.dtype)
```

```python
import functools
import math

import numpy as np
import jax
import jax.numpy as jnp
from jax import lax
from jax.experimental import pallas as pl
from jax.experimental.pallas import tpu as pltpu

F32 = jnp.float32
BF16 = jnp.bfloat16

HEAD_DIM = 64
BRANCH_WIDTH = 256
N_HEADS = 4
N_BRANCHES = 5
DIFF_QK_DIM = 32
DIL_PATTERNS = ((128, 1), (512, 4), (2048, 16))
S5_GROUP = 16
S5_GROUPS = 16
S5_STATE = 64
CMP_BLOCK = 32
CMP_STRIDE = 16
SLC_BLOCK = 64
N_SELECT = 16
WINDOW = 512
ROPE_THETA = 10000.0
RMS_EPS = 1e-6
NEG_INF = -1e30
FORCE_SCORE = 1e9

V7X_VMEM_BYTES = 64 * 1024 * 1024
BIG_VMEM_LIMIT = V7X_VMEM_BYTES - 8 * 1024 * 1024

LANES = 128
ATT_TILE = 512
ROW_TILE = 512
MERGE_ROW_TILE = 512
MEM_ATT_TILE = 2048
S5_CHUNK = 128

_NT = (((1,), (1,)), ((), ()))


def _rms(x, g):
    return x * lax.rsqrt(jnp.mean(x * x, axis=-1, keepdims=True) + RMS_EPS) * g


def _dot(a, b):
    return jnp.dot(a, b, preferred_element_type=F32)


def _dot_nt(a, b):
    return lax.dot_general(a, b, _NT, preferred_element_type=F32)


def _split_hi_lo(x):
    hi = x.astype(BF16)
    lo = (x - hi.astype(F32)).astype(BF16)
    return hi, lo


def _dot_hilo(x, w):
    hi, lo = _split_hi_lo(x)
    return _dot(hi, w) + _dot(lo, w)


def _const_spec(shape):
    n = len(shape)
    return pl.BlockSpec(shape, lambda *_: (0,) * n, pipeline_mode=pl.Buffered(1))


def _head_mask(h, dtype):
    lane = lax.broadcasted_iota(jnp.int32, (1, BRANCH_WIDTH), 1)
    return jnp.where((lane >= h * HEAD_DIM) & (lane < (h + 1) * HEAD_DIM), 1.0, 0.0).astype(dtype)


def _low_half():
    return lax.broadcasted_iota(jnp.int32, (1, LANES), 1) < HEAD_DIM


def _swap_halves(x):
    return pltpu.roll(x, HEAD_DIM, 1)


def _augment_heads(v):
    low = _low_half()
    parts = []
    for j in range(v.shape[1] // LANES):
        pair = v[:, j * LANES:(j + 1) * LANES]
        parts += [jnp.where(low, pair, 1.0), jnp.where(low, 1.0, pair)]
    return jnp.concatenate(parts, axis=1)


def _shared_kv_variants(kv):
    low = _low_half()
    sw = _swap_halves(kv)
    k2 = jnp.concatenate([jnp.where(low, kv, 0.0), jnp.where(low, 0.0, sw)], axis=1)
    v2 = jnp.concatenate([jnp.where(low, sw, 1.0), jnp.where(low, 1.0, kv)], axis=1)
    return k2, v2


def _normalise_pair(acc_even, acc_odd):
    return jnp.where(_low_half(), acc_even / _swap_halves(acc_even), acc_odd / _swap_halves(acc_odd))


_IN_MAIN_WIDTH = 3072
_IN_SRC = {"qa": (True, 0), "ka": (True, 256), "va": (True, 512), "az": (True, 768),
           "qb": (True, 1024), "kb": (True, 1280), "vb": (True, 1536), "bz": (True, 1792),
           "cu": (True, 2048), "cz": (True, 2304), "dq": (True, 2560), "kvcs": (True, 2816),
           "kvwg": (False, 0), "dz": (False, 256), "eq": (False, 512), "ez": (False, 768)}
_IN_TAIL_WIDTH = 1024
_IN_SCALED = ("qb", "dq", "eq")
_IN_OUTS = (("qa", 256, BF16), ("ka", 256, BF16), ("va", 512, BF16),
            ("qb", 256, BF16), ("kb", 256, BF16), ("vb", 512, BF16),
            ("cu", 256, F32), ("dq", 256, BF16), ("dqr", 256, BF16), ("kvc", 128, F32),
            ("ks", 256, BF16), ("vs", 256, BF16), ("kw", 256, BF16), ("vw", 256, BF16),
            ("dg", 128, F32), ("eq", 256, BF16), ("zs", 1280, F32))


def _rotate_half(y, group):
    half = group // 2
    lane = lax.broadcasted_iota(jnp.int32, (1, LANES), 1)
    first = (lane & (group - 1)) < half
    parts = []
    for j in range(y.shape[1] // LANES):
        v = y[:, j * LANES:(j + 1) * LANES]
        parts.append(jnp.where(first, pltpu.roll(v, LANES - half, 1), pltpu.roll(v, half, 1)))
    return jnp.concatenate(parts, axis=1)


def _in_proj_kernel(x_ref, g_ref, wm_ref, wt_ref, cosa_ref, sina_ref, cosb_ref, sinb_ref, cosk_ref, sink_ref,
                    *out_refs):
    out = dict(zip([n for n, _, _ in _IN_OUTS], out_refs))
    h = _rms(x_ref[...], g_ref[...]).astype(BF16)

    def proj(name):
        main, off = _IN_SRC[name]
        y = _dot(h, (wm_ref if main else wt_ref)[:, off:off + BRANCH_WIDTH].astype(BF16))
        return y * HEAD_DIM ** -0.5 if name in _IN_SCALED else y

    def rope(y, cos_ref, sin_ref, group):
        return y * cos_ref[...] + _rotate_half(y, group) * sin_ref[...]

    def put(name, y):
        out[name][...] = y.astype(out[name].dtype)

    put("qa", rope(proj("qa"), cosa_ref, sina_ref, DIFF_QK_DIM))
    put("ka", rope(proj("ka"), cosa_ref, sina_ref, DIFF_QK_DIM))
    put("va", _augment_heads(proj("va")))
    put("qb", rope(proj("qb"), cosb_ref, sinb_ref, HEAD_DIM))
    put("kb", rope(proj("kb"), cosb_ref, sinb_ref, HEAD_DIM))
    put("vb", _augment_heads(proj("vb")))
    put("cu", proj("cu"))
    dq = proj("dq")
    put("dq", dq)
    put("dqr", rope(dq, cosb_ref, sinb_ref, HEAD_DIM))
    kvcs = proj("kvcs")
    kvwg = proj("kvwg")
    put("kvc", kvcs[:, :LANES])
    put("dg", kvwg[:, LANES:])
    for kv, k_name, v_name in ((kvcs[:, LANES:], "ks", "vs"), (kvwg[:, :LANES], "kw", "vw")):
        k2, v2 = _shared_kv_variants(rope(kv, cosk_ref, sink_ref, HEAD_DIM))
        put(k_name, k2)
        put(v_name, v2)
    put("eq", proj("eq"))
    for n, name in enumerate(("az", "bz", "cz", "dz", "ez")):
        z = proj(name)
        out["zs"][:, n * BRANCH_WIDTH:(n + 1) * BRANCH_WIDTH] = z * jax.nn.sigmoid(z)


def _in_tail(w):
    used = LANES + 3 * N_HEADS
    tail = w[:, _IN_MAIN_WIDTH:]
    return jnp.concatenate([jnp.pad(tail[:, :used], ((0, 0), (0, BRANCH_WIDTH - used))), tail[:, used:]], axis=1)


def _rope_tables(s_len):
    def table(group, width):
        half = group // 2
        inv_freq = ROPE_THETA ** (-jnp.arange(half, dtype=F32) / half)
        ang = jnp.arange(s_len, dtype=F32)[:, None] * inv_freq[None, :]
        cos = jnp.tile(jnp.cos(ang), (1, width // half))
        sin = jnp.tile(jnp.concatenate([-jnp.sin(ang), jnp.sin(ang)], axis=1), (1, width // group))
        return cos, sin
    cos_a, sin_a = table(DIFF_QK_DIM, BRANCH_WIDTH)
    cos_b, sin_b = table(HEAD_DIM, BRANCH_WIDTH)
    cos_k = jnp.concatenate([cos_b[:, :HEAD_DIM], jnp.ones((s_len, HEAD_DIM), F32)], axis=1)
    sin_k = jnp.concatenate([sin_b[:, :HEAD_DIM], jnp.zeros((s_len, HEAD_DIM), F32)], axis=1)
    return cos_a, sin_a, cos_b, sin_b, cos_k, sin_k


def _in_proj(x2, g, w_in, layer, tables, s_len):
    t, d = x2.shape
    tm = ROW_TILE
    nsb = s_len // tm
    w_tail = _in_tail(w_in[layer])
    assert w_tail.shape[1] == _IN_TAIL_WIDTH
    row = lambda w: pl.BlockSpec((tm, w), lambda i: (i, 0))
    tab = lambda a: pl.BlockSpec((tm, a.shape[1]), lambda i: (i % nsb, 0))
    w_main = pl.BlockSpec((None, d, _IN_MAIN_WIDTH), lambda i: (layer, 0, 0), pipeline_mode=pl.Buffered(1))
    return pl.pallas_call(
        _in_proj_kernel,
        grid=(t // tm,),
        in_specs=[row(d), _const_spec((1, d)), w_main, _const_spec(w_tail.shape)] + [tab(a) for a in tables],
        out_specs=[row(w) for _, w, _ in _IN_OUTS],
        out_shape=[jax.ShapeDtypeStruct((t, w), dt) for _, w, dt in _IN_OUTS],
        compiler_params=pltpu.CompilerParams(dimension_semantics=("arbitrary",),
                                             vmem_limit_bytes=BIG_VMEM_LIMIT),
        name="in_proj",
    )(x2, g.reshape(1, d), w_in, w_tail, *tables)


def _lane_fold(x, op):
    parts = [x[:, j * LANES:(j + 1) * LANES] for j in range(x.shape[1] // LANES)]
    return functools.reduce(op, parts)


def _strip_attention(chains, tq, kb_lo, kb_hi, bias_fn, last_bias, scratch, exp_scale=1.0):
    s_ref, m_ref, acc_ref = scratch
    n = len(chains)
    reps = tq // LANES
    c1 = exp_scale * math.log2(math.e)

    def tile(cache, ref, col, width, kb, ntiles=1):
        key = (id(ref), col, width)
        if key not in cache:
            cache[key] = ref[pl.ds(pl.multiple_of(kb * tq, tq), ntiles * tq), col:col + width]
        return cache[key]

    def scores(kb, extra):
        bias = bias_fn(kb) if bias_fn is not None else None
        if extra is not None:
            bias = extra if bias is None else bias + extra
        cache, out = {}, []
        for q, (k_ref, k_col), _ in chains:
            s = _dot_nt(q, tile(cache, k_ref, k_col, q.shape[1], kb)) * c1
            out.append(s if bias is None else s + bias)
        return out

    def tile_pairs(lo, hi, step):
        def two(j, carry):
            step(lo + 2 * j, 2)
            return carry
        cnt = hi - lo
        lax.fori_loop(0, lax.shift_right_arithmetic(cnt, 1), two, 0)

        @pl.when((cnt & 1) == 1)
        def _():
            step(hi - 1, 1)

    m_ref[0:n] = jnp.full((n,) + m_ref.shape[1:], NEG_INF, F32)

    def pass1(kb, ntiles):
        tiles = [scores(kb + j, None) for j in range(ntiles)]
        for i in range(n):
            m = m_ref[i]
            for j in range(ntiles):
                s_ref[i, kb + j] = tiles[j][i]
                m = jnp.maximum(m, _lane_fold(tiles[j][i], jnp.maximum))
            m_ref[i] = m

    tile_pairs(kb_lo, kb_hi - 1, pass1)
    for i, s in enumerate(scores(kb_hi - 1, last_bias)):
        s_ref[i, kb_hi - 1] = s
        m = jnp.max(jnp.maximum(m_ref[i], _lane_fold(s, jnp.maximum)), axis=-1, keepdims=True)
        m_ref[i] = jnp.broadcast_to(m, m_ref.shape[1:])
        acc_ref[i] = jnp.zeros(acc_ref.shape[1:], F32)

    def pass2(kb, ntiles):
        cache = {}
        for i, (_, _, (v_ref, v_col)) in enumerate(chains):
            m = jnp.concatenate([m_ref[i]] * reps, axis=1)
            p = [jnp.exp2(s_ref[i, kb + j] - m).astype(BF16) for j in range(ntiles)]
            p = p[0] if ntiles == 1 else jnp.concatenate(p, axis=1)
            acc_ref[i] += _dot(p, tile(cache, v_ref, v_col, LANES, kb, ntiles))

    tile_pairs(kb_lo, kb_hi, pass2)
    return [acc_ref[i] for i in range(n)]


def _strip_scratch(n, tq, s_len):
    return [pltpu.VMEM((n, s_len // tq, tq, tq), F32), pltpu.VMEM((n, tq, LANES), F32),
            pltpu.VMEM((n, tq, LANES), F32)]


def _causal_bias(tq):
    r = lax.broadcasted_iota(jnp.int32, (tq, tq), 0)
    c = lax.broadcasted_iota(jnp.int32, (tq, tq), 1)
    return jnp.where(c <= r, 0.0, NEG_INF).astype(F32)


def _att_specs(tq, s_len, k_width, v_width):
    q_spec = pl.BlockSpec((None, tq, BRANCH_WIDTH), lambda b, i: (b, i, 0))
    kv = lambda w: pl.BlockSpec((None, s_len, w), lambda b, i: (b, 0, 0))
    return q_spec, kv(k_width), kv(v_width)


def _diff_kernel(lam_ref, q_ref, k_ref, v_ref, g_ref, hm_ref, o_ref, *scratch, tq, out_scale):
    qi = pl.program_id(1)
    q = q_ref[...]
    lane = lax.broadcasted_iota(jnp.int32, (1, BRANCH_WIDTH), 1)
    lam = lam_ref[0]
    chains = []
    for hc in range(2 * N_HEADS):
        lo = hc * DIFF_QK_DIM
        cmask = jnp.where((lane >= lo) & (lane < lo + DIFF_QK_DIM), 1.0, 0.0).astype(BF16)
        chains.append((q * cmask, (k_ref, 0), (v_ref, (hc // 2) * LANES)))
    acc = _strip_attention(chains, tq, 0, qi + 1, None, _causal_bias(tq), scratch,
                           exp_scale=DIFF_QK_DIM ** -0.5)
    halves = []
    for pair in range(N_HEADS // 2):
        even, odd = 4 * pair, 4 * pair + 2
        halves.append(_normalise_pair(acc[even], acc[odd]) - lam * _normalise_pair(acc[even + 1], acc[odd + 1]))
    o = jnp.concatenate(halves, axis=1)
    ms = _dot_hilo(o * o, hm_ref[...])
    o_ref[...] = o * lax.rsqrt(ms + RMS_EPS) * g_ref[...] * out_scale


def _diff_attention(q, k, v_aug, lam, subln_g, lam_init):
    b, s_len, w = q.shape
    tq = ATT_TILE
    q_spec, k_spec, v_spec = _att_specs(tq, s_len, w, v_aug.shape[-1])
    head = np.arange(w) // HEAD_DIM
    hm = jnp.asarray((head[:, None] == head[None, :]) / HEAD_DIM, dtype=BF16)
    g = jnp.tile(subln_g.astype(F32), N_HEADS).reshape(1, w)
    return pl.pallas_call(
        functools.partial(_diff_kernel, tq=tq, out_scale=1.0 - lam_init),
        grid=(b, s_len // tq),
        in_specs=[pl.BlockSpec(memory_space=pltpu.SMEM), q_spec, k_spec, v_spec,
                  _const_spec((1, w)), _const_spec((w, w))],
        out_specs=q_spec,
        out_shape=jax.ShapeDtypeStruct((b, s_len, w), F32),
        scratch_shapes=_strip_scratch(2 * N_HEADS, tq, s_len),
        compiler_params=pltpu.CompilerParams(dimension_semantics=("arbitrary", "arbitrary"),
                                             vmem_limit_bytes=BIG_VMEM_LIMIT),
        name="diff_attention",
    )(lam.reshape(1), q, k, v_aug, g, hm)


def _dil_kernel(q_ref, k_ref, v_ref, bias_ref, o_ref, *scratch, tq):
    qi = pl.program_id(1)
    q = q_ref[...]
    chains = [(q * _head_mask(h, BF16), (k_ref, 0), (v_ref, h * LANES)) for h in range(N_HEADS)]
    acc = _strip_attention(chains, tq, 0, qi + 1, lambda kb: bias_ref[qi - kb], None, scratch)
    o_ref[...] = jnp.concatenate([_normalise_pair(acc[0], acc[1]), _normalise_pair(acc[2], acc[3])], axis=1)


def _dilated_bias(tq, s_len):
    nq = s_len // tq
    d = (np.arange(nq)[:, None, None] * tq + np.arange(tq)[None, :, None] - np.arange(tq)[None, None, :])
    count = np.zeros(d.shape, np.float64)
    for window, dil in DIL_PATTERNS:
        count += (d >= 0) & (d <= window) & (d % dil == 0)
    return np.where(count > 0, np.log2(np.maximum(count, 1.0)), NEG_INF).astype(np.float32)


def _dilated_attention(q, k, v_aug):
    b, s_len, w = q.shape
    tq = ATT_TILE
    q_spec, k_spec, v_spec = _att_specs(tq, s_len, w, v_aug.shape[-1])
    bias = jnp.asarray(_dilated_bias(tq, s_len))
    return pl.pallas_call(
        functools.partial(_dil_kernel, tq=tq),
        grid=(b, s_len // tq),
        in_specs=[q_spec, k_spec, v_spec, _const_spec(bias.shape)],
        out_specs=q_spec,
        out_shape=jax.ShapeDtypeStruct((b, s_len, w), F32),
        scratch_shapes=_strip_scratch(N_HEADS, tq, s_len),
        compiler_params=pltpu.CompilerParams(dimension_semantics=("arbitrary", "arbitrary"),
                                             vmem_limit_bytes=BIG_VMEM_LIMIT),
        name="dilated_attention",
    )(q, k, v_aug, bias)


def _s5_kernel(u_ref, bm_ref, cm_ref, are_ref, aim_ref, d_ref, wg_ref, bg_ref, o_ref, st_ref, *xs_refs,
               ts, nb):
    n = S5_GROUPS * S5_STATE
    nc = n // LANES

    @pl.when(pl.program_id(0) == 0)
    def _():
        st_ref[...] = jnp.zeros(st_ref.shape, F32)

    u = u_ref[...].reshape(nb * ts, u_ref.shape[-1])
    bu = _dot(u.astype(BF16), bm_ref[...])
    for c, x_ref in enumerate(xs_refs):
        for b in range(nb):
            x_ref[pl.ds(b, ts, stride=nb), :] = bu[b * ts:(b + 1) * ts, c * LANES:(c + 1) * LANES]
    a_re = jnp.broadcast_to(are_ref[...], (nb, n))
    a_im = jnp.broadcast_to(aim_ref[...], (nb, n))

    def step(t, carry):
        x_re, x_im = carry
        rows = pl.ds(pl.multiple_of(t * nb, nb), nb)
        bu_re = jnp.concatenate([x_ref[rows, :] for x_ref in xs_refs[:nc]], axis=1)
        bu_im = jnp.concatenate([x_ref[rows, :] for x_ref in xs_refs[nc:]], axis=1)
        n_re = a_re * x_re - a_im * x_im + bu_re
        n_im = a_re * x_im + a_im * x_re + bu_im
        for c in range(nc):
            xs_refs[c][rows, :] = n_re[:, c * LANES:(c + 1) * LANES]
            xs_refs[nc + c][rows, :] = n_im[:, c * LANES:(c + 1) * LANES]
        return n_re, n_im

    x_re, x_im = lax.fori_loop(0, ts, step, (st_ref[:, 0:n], st_ref[:, n:2 * n]))
    st_ref[:, 0:n] = x_re
    st_ref[:, n:2 * n] = x_im

    xs = jnp.concatenate(
        [jnp.concatenate([x_ref[pl.ds(b, ts, stride=nb), :].astype(BF16) for x_ref in xs_refs], axis=1)
         for b in range(nb)], axis=0)
    y = _dot(xs, cm_ref[...]) + d_ref[...] * u
    t = _dot(jax.nn.gelu(y).astype(BF16), wg_ref[...]) + bg_ref[...]
    o_ref[...] = (t[:, :BRANCH_WIDTH] * jax.nn.sigmoid(t[:, BRANCH_WIDTH:])).reshape(o_ref.shape)


def _s5_params(lam_re, lam_im, log_dt, b_re, b_im, c_re, c_im):
    g, n, p = S5_GROUPS, S5_STATE, S5_GROUP
    lr, li = lam_re.astype(F32), lam_im.astype(F32)
    dt = jnp.exp(log_dt.astype(F32))[:, None]
    mag = jnp.exp(lr * dt)
    a_re, a_im = mag * jnp.cos(li * dt), mag * jnp.sin(li * dt)
    den = lr * lr + li * li
    n_re, n_im = a_re - 1.0, a_im
    z_re = (n_re * lr + n_im * li) / den
    z_im = (n_im * lr - n_re * li) / den
    br, bi = b_re.astype(F32), b_im.astype(F32)
    bb_re = z_re[..., None] * br - z_im[..., None] * bi
    bb_im = z_re[..., None] * bi + z_im[..., None] * br
    eye = jnp.eye(g, dtype=F32)
    blockdiag_in = lambda t: jnp.einsum("gnp,gh->gphn", t, eye).reshape(g * p, g * n)
    blockdiag_out = lambda t: jnp.einsum("gpn,gh->gnhp", t, eye).reshape(g * n, g * p)
    bm = jnp.concatenate([blockdiag_in(bb_re), blockdiag_in(bb_im)], axis=1)
    cm = jnp.concatenate([blockdiag_out(c_re.astype(F32)), -blockdiag_out(c_im.astype(F32))], axis=0)
    return bm.astype(BF16), cm.astype(BF16), a_re.reshape(1, g * n), a_im.reshape(1, g * n)


def _s5_branch(u, params, d_skip, w_glu, b_glu):
    nb, s_len, w = u.shape
    assert nb == 8
    ts = S5_CHUNK
    bm, cm, a_re, a_im = params
    n2 = bm.shape[1]
    blk = pl.BlockSpec((nb, ts, w), lambda i: (0, i, 0))
    return pl.pallas_call(
        functools.partial(_s5_kernel, ts=ts, nb=nb),
        grid=(s_len // ts,),
        in_specs=[blk, _const_spec(bm.shape), _const_spec(cm.shape), _const_spec(a_re.shape),
                  _const_spec(a_im.shape), _const_spec((1, w)), _const_spec(w_glu.shape),
                  _const_spec((1, 2 * w))],
        out_specs=blk,
        out_shape=jax.ShapeDtypeStruct((nb, s_len, w), F32),
        scratch_shapes=[pltpu.VMEM((nb, n2), F32)] + [pltpu.VMEM((ts * nb, LANES), F32)] * (n2 // LANES),
        compiler_params=pltpu.CompilerParams(dimension_semantics=("arbitrary",),
                                             vmem_limit_bytes=BIG_VMEM_LIMIT),
        name="s5_scan",
    )(u, bm, cm, a_re, a_im, d_skip.astype(F32).reshape(1, w), w_glu.astype(BF16),
      b_glu.astype(F32).reshape(1, 2 * w))


def _compress_kernel(r_ref, pe_top_ref, pe_bot_ref, w_top_ref, w_bot_ref, w2_ref, k_ref, v_ref):
    r = r_ref[...]
    top = (r + pe_top_ref[...]).astype(BF16)
    nxt = pltpu.roll(r, r.shape[0] - 1, 0)
    bot = (nxt + pe_bot_ref[...]).astype(BF16)
    hid = jax.nn.gelu(_dot(top, w_top_ref[...]) + _dot(bot, w_bot_ref[...]))
    k2, v2 = _shared_kv_variants(_dot(hid.astype(BF16), w2_ref[...]))
    k_ref[...] = k2.astype(BF16)
    v_ref[...] = v2.astype(BF16)


def _compress(kvc, pe, w1, w2):
    b, s_len, _ = kvc.shape
    nr = s_len // CMP_STRIDE
    per = CMP_BLOCK // CMP_STRIDE
    assert per == 2
    hid = w1.shape[-1]
    r = kvc.reshape(b, nr, CMP_STRIDE * 2 * HEAD_DIM)
    w1r = w1.astype(F32).reshape(2, per, CMP_STRIDE, HEAD_DIM, hid)
    per_r = pe.astype(F32).reshape(2, per, CMP_STRIDE, HEAD_DIM)

    def expand(j):
        wk = jnp.pad(w1r[0, j], ((0, 0), (0, HEAD_DIM), (0, hid)))
        wv = jnp.pad(w1r[1, j], ((0, 0), (HEAD_DIM, 0), (hid, 0)))
        return (wk + wv).reshape(CMP_STRIDE * 2 * HEAD_DIM, 2 * hid).astype(BF16)

    pe_rows = [jnp.concatenate([per_r[0, j], per_r[1, j]], axis=-1).reshape(1, -1) for j in range(per)]
    z = jnp.zeros((hid, HEAD_DIM), F32)
    w2f = w2.astype(F32)
    w2x = jnp.concatenate([jnp.concatenate([w2f[0], z], axis=1),
                           jnp.concatenate([z, w2f[1]], axis=1)], axis=0).astype(BF16)
    w_top, w_bot = expand(0), expand(1)
    blk = pl.BlockSpec((None, nr, r.shape[-1]), lambda i: (i, 0, 0))
    out = pl.BlockSpec((None, nr, BRANCH_WIDTH), lambda i: (i, 0, 0))
    return pl.pallas_call(
        _compress_kernel,
        grid=(b,),
        in_specs=[blk, _const_spec(pe_rows[0].shape), _const_spec(pe_rows[1].shape),
                  _const_spec(w_top.shape), _const_spec(w_bot.shape), _const_spec(w2x.shape)],
        out_specs=[out, out],
        out_shape=[jax.ShapeDtypeStruct((b, nr, BRANCH_WIDTH), BF16)] * 2,
        compiler_params=pltpu.CompilerParams(dimension_semantics=("arbitrary",)),
        name="nsa_compress",
    )(r, pe_rows[0], pe_rows[1], w_top, w_bot, w2x)


def _nsa_kernel(dq_ref, dqr_ref, dg_ref, kc_ref, vc_ref, ks_ref, vs_ref, kw_ref, vw_ref,
                ovt_ref, exp_ref, gsel_ref, wbias_ref, o_ref, *scratch, tq, n_cmp, n_slc, n_sel):
    qi = pl.program_id(1)
    t0 = qi * tq
    nr = kc_ref.shape[0]
    low = _low_half()
    q_cols = lambda h: slice((h // 2) * LANES, (h // 2 + 1) * LANES)
    kv_col = lambda h: (h % 2) * LANES

    dq = dq_ref[...]
    tpos = t0 + lax.broadcasted_iota(jnp.int32, (tq, nr), 0)
    ci = lax.broadcasted_iota(jnp.int32, (tq, nr), 1)
    cmask = (ci * CMP_STRIDE + (CMP_BLOCK - 1) <= tpos) & (ci < n_cmp)
    p_sum = jnp.zeros((tq, nr), F32)
    o_heads = []
    for h in range(N_HEADS):
        s = jnp.where(cmask, _dot_nt(dq[:, q_cols(h)], kc_ref[:, kv_col(h):kv_col(h) + LANES]), NEG_INF)
        m = jnp.max(s, axis=-1, keepdims=True)
        e = jnp.where(cmask, jnp.exp(s - m), 0.0)
        p = e / jnp.maximum(jnp.sum(e, axis=-1, keepdims=True), 1e-30)
        p_sum = p_sum + p
        o_heads.append(_dot(p.astype(BF16), vc_ref[:, kv_col(h):kv_col(h) + LANES]))
    o_cmp = jnp.concatenate([jnp.where(low, o_heads[0], o_heads[1]), jnp.where(low, o_heads[2], o_heads[3])],
                            axis=1)

    p_hi, p_lo = _split_hi_lo(p_sum)
    ovt = ovt_ref[...]
    imp_t = _dot_nt(ovt, p_hi) + _dot_nt(ovt, p_lo)
    blk = lax.broadcasted_iota(jnp.int32, (n_slc, tq), 0)
    qblk = lax.shift_right_arithmetic(t0 + lax.broadcasted_iota(jnp.int32, (n_slc, tq), 1),
                                      int(math.log2(SLC_BLOCK)))
    forced = jnp.where(blk == 0, 1, jnp.where(blk == qblk, 1, jnp.where(blk == qblk - 1, 1, 0)))
    score = jnp.where(blk <= qblk, jnp.where(forced > 0, FORCE_SCORE, imp_t), NEG_INF)
    rank = jnp.zeros((n_slc, tq), F32)
    for i in range(n_slc):
        si = score[i:i + 1, :]
        tie = jnp.where(blk > i, 1.0, 0.0)
        rank = rank + jnp.where(si > score, 1.0, jnp.where(si == score, tie, 0.0))
    sel_bias_t = jnp.where(rank < n_sel, jnp.where(score > 0.5 * NEG_INF, 0.0, NEG_INF), NEG_INF)
    pad_rows = exp_ref.shape[1] - n_slc
    sel_bias = jnp.concatenate([sel_bias_t, jnp.full((pad_rows, tq), NEG_INF, F32)], axis=0).T.astype(BF16)

    dqr = dqr_ref[...]
    pairs = lambda acc: jnp.concatenate([_normalise_pair(acc[0], acc[1]), _normalise_pair(acc[2], acc[3])], axis=1)
    chains = [(dqr[:, q_cols(h)], (ks_ref, kv_col(h)), (vs_ref, kv_col(h))) for h in range(N_HEADS)]
    o_slc = pairs(_strip_attention(chains, tq, 0, qi + 1, lambda kb: _dot(sel_bias, exp_ref[kb]),
                                   _causal_bias(tq), scratch))
    chains = [(dqr[:, q_cols(h)], (kw_ref, kv_col(h)), (vw_ref, kv_col(h))) for h in range(N_HEADS)]
    win_lo = jnp.maximum(qi - (wbias_ref.shape[0] - 1), 0)
    o_win = pairs(_strip_attention(chains, tq, win_lo, qi + 1, lambda kb: wbias_ref[qi - kb], None, scratch))

    gates = _dot_hilo(jax.nn.sigmoid(dg_ref[...]), gsel_ref[...])
    w = BRANCH_WIDTH
    o_ref[...] = gates[:, :w] * o_cmp + gates[:, w:2 * w] * o_slc + gates[:, 2 * w:] * o_win


def _nsa_constants(tq, s_len):
    n_cmp = (s_len - CMP_BLOCK) // CMP_STRIDE + 1
    n_slc = s_len // SLC_BLOCK
    nr = s_len // CMP_STRIDE
    c0 = np.arange(n_cmp)[:, None] * CMP_STRIDE
    s0 = np.arange(n_slc)[None, :] * SLC_BLOCK
    overlap = np.clip(np.minimum(c0 + CMP_BLOCK, s0 + SLC_BLOCK) - np.maximum(c0, s0), 0, None) / CMP_STRIDE
    ovt = np.zeros((n_slc, nr), np.float32)
    ovt[:, :n_cmp] = overlap.T
    rows = -(-n_slc // 128) * 128
    expand = np.zeros((s_len // tq, rows, tq), np.float32)
    tok = np.arange(s_len)
    expand[tok // tq, tok // SLC_BLOCK, tok % tq] = 1.0
    gsel = np.zeros((128, 3 * BRANCH_WIDTH), np.float32)
    for h in range(N_HEADS):
        for j in range(3):
            gsel[h * 3 + j, j * BRANCH_WIDTH + h * HEAD_DIM:j * BRANCH_WIDTH + (h + 1) * HEAD_DIM] = 1.0
    nwin = -(-WINDOW // tq) + 1
    d = np.arange(nwin)[:, None, None] * tq + np.arange(tq)[None, :, None] - np.arange(tq)[None, None, :]
    wbias = np.where((d >= 0) & (d < WINDOW), 0.0, NEG_INF).astype(np.float32)
    return n_cmp, n_slc, ovt, expand, gsel, wbias


def _nsa_attention(dq, dqr, dg, k_cmp, v_cmp, ks, vs, kw, vw):
    b, s_len, w = dq.shape
    tq = ATT_TILE
    n_cmp, n_slc, ovt, expand, gsel, wbias = _nsa_constants(tq, s_len)
    q_spec, kv_spec, _ = _att_specs(tq, s_len, w, w)
    nr = k_cmp.shape[1]
    cmp_spec = pl.BlockSpec((None, nr, w), lambda bi, i: (bi, 0, 0))
    g_spec = pl.BlockSpec((None, tq, dg.shape[-1]), lambda bi, i: (bi, i, 0))
    consts = [jnp.asarray(ovt, BF16), jnp.asarray(expand, BF16), jnp.asarray(gsel, BF16), jnp.asarray(wbias)]
    return pl.pallas_call(
        functools.partial(_nsa_kernel, tq=tq, n_cmp=n_cmp, n_slc=n_slc, n_sel=min(N_SELECT, n_slc)),
        grid=(b, s_len // tq),
        in_specs=[q_spec, q_spec, g_spec, cmp_spec, cmp_spec, kv_spec, kv_spec, kv_spec, kv_spec]
        + [_const_spec(c.shape) for c in consts],
        out_specs=q_spec,
        out_shape=jax.ShapeDtypeStruct((b, s_len, w), F32),
        scratch_shapes=_strip_scratch(N_HEADS, tq, s_len),
        compiler_params=pltpu.CompilerParams(dimension_semantics=("arbitrary", "arbitrary"),
                                             vmem_limit_bytes=BIG_VMEM_LIMIT),
        name="nsa_attention",
    )(dq, dqr, dg, k_cmp, v_cmp, ks, vs, kw, vw, *consts)


def _mem_attn_kernel(q_ref, m_ref, g_ref, w_ref, o_ref):
    q = q_ref[...]
    kv = _dot(_rms(m_ref[...], g_ref[...]).astype(BF16), w_ref[...])
    k = kv[:, :BRANCH_WIDTH].astype(BF16)
    v = kv[:, BRANCH_WIDTH:].astype(BF16)
    o = jnp.zeros(o_ref.shape, F32)
    for h in range(N_HEADS):
        s = _dot_nt(q * _head_mask(h, BF16), k)
        e = jnp.exp(s - jnp.max(s, axis=-1, keepdims=True))
        oh = _dot(e.astype(BF16), v) / jnp.sum(e, axis=-1, keepdims=True)
        o = jnp.where(_head_mask(h, F32) > 0.5, oh, o)
    o_ref[...] = o


def _mem_attention(q, mem, g, w_kv):
    b, s_len, w = q.shape
    m_len, d = mem.shape[1:]
    tq = min(MEM_ATT_TILE, s_len)
    q_spec = pl.BlockSpec((None, tq, w), lambda bi, i: (bi, i, 0))
    mem_spec = pl.BlockSpec((None, m_len, d), lambda bi, i: (bi, 0, 0))
    return pl.pallas_call(
        _mem_attn_kernel,
        grid=(b, s_len // tq),
        in_specs=[q_spec, mem_spec, _const_spec((1, d)), _const_spec(w_kv.shape)],
        out_specs=q_spec,
        out_shape=jax.ShapeDtypeStruct((b, s_len, w), F32),
        compiler_params=pltpu.CompilerParams(dimension_semantics=("arbitrary", "arbitrary")),
        name="mem_attention",
    )(q, mem, g.astype(F32).reshape(1, d), w_kv.astype(BF16))


def _merge_kernel(x_ref, g_ref, oa_ref, ob_ref, oc_ref, od_ref, oe_ref, zs_ref, wm_ref, bm_ref, wb_ref,
                  wo_ref, fg_ref, o_ref, *, final_norm):
    x = x_ref[...]
    d = x.shape[-1]
    w = BRANCH_WIDTH
    h = _rms(x, g_ref[...]).astype(BF16)
    mixed = jnp.zeros(x.shape, F32)
    for n, br_ref in enumerate((oa_ref, ob_ref, oc_ref, od_ref, oe_ref)):
        br = (br_ref[...] * zs_ref[:, n * w:(n + 1) * w]).astype(BF16)
        y = _dot(br, wb_ref[n].astype(BF16))
        gate = jax.nn.sigmoid(_dot(h, wm_ref[:, n * d:(n + 1) * d].astype(BF16)) + bm_ref[:, n * d:(n + 1) * d])
        mixed = mixed + gate * y
    out = x + _dot(mixed.astype(BF16), wo_ref[...].astype(BF16))
    if final_norm:
        out = _rms(out, fg_ref[...])
    o_ref[...] = out


def _merge(x2, g, branches, zs, w_merge, b_merge, w_branch, w_out, layer, final_g, final_norm):
    t, d = x2.shape
    tm = MERGE_ROW_TILE
    row = lambda wd: pl.BlockSpec((tm, wd), lambda i: (i, 0))

    def of_layer(a):
        nd = a.ndim - 1
        return pl.BlockSpec((None,) + a.shape[1:], lambda i: (layer,) + (0,) * nd, pipeline_mode=pl.Buffered(1))

    b_merge = b_merge.astype(F32).reshape(b_merge.shape[0], 1, -1)
    return pl.pallas_call(
        functools.partial(_merge_kernel, final_norm=final_norm),
        grid=(t // tm,),
        in_specs=[row(d), _const_spec((1, d))] + [row(BRANCH_WIDTH)] * N_BRANCHES + [row(zs.shape[1])]
        + [of_layer(w_merge), of_layer(b_merge), of_layer(w_branch), of_layer(w_out), _const_spec((1, d))],
        out_specs=row(d),
        out_shape=jax.ShapeDtypeStruct((t, d), F32),
        compiler_params=pltpu.CompilerParams(dimension_semantics=("arbitrary",),
                                             vmem_limit_bytes=BIG_VMEM_LIMIT),
        name="merge",
    )(x2, g.astype(F32).reshape(1, d), *branches, zs, w_merge.astype(F32), b_merge, w_branch.astype(F32),
      w_out.astype(F32), final_g.astype(F32).reshape(1, d))


def kernel(x, mem, norm_g, w_in, diff_lambda, diff_subln_g, s5_lambda_re, s5_lambda_im, s5_log_dt,
           s5_b_re, s5_b_im, s5_c_re, s5_c_im, s5_d, w_glu, b_glu, nsa_pe, nsa_w1, nsa_w2, mem_norm_g,
           w_mem_kv, w_merge, b_merge, w_branch, w_out, final_g):
    bsz, s_len, d = x.shape
    depth = w_in.shape[0]
    t = bsz * s_len
    w = BRANCH_WIDTH
    tables = _rope_tables(s_len)
    x2 = x.astype(F32).reshape(t, d)
    for l in range(depth):
        proj = dict(zip([n for n, _, _ in _IN_OUTS],
                        _in_proj(x2, norm_g[l].astype(F32), w_in.astype(F32), l, tables, s_len)))
        seq = lambda name: proj[name].reshape(bsz, s_len, -1)

        dl = diff_lambda[l].astype(F32)
        lam_init = 0.8 - 0.6 * math.exp(-0.3 * l)
        lam = jnp.exp(jnp.sum(dl[0] * dl[1])) - jnp.exp(jnp.sum(dl[2] * dl[3])) + lam_init
        o_a = _diff_attention(seq("qa"), seq("ka"), seq("va"), lam, diff_subln_g[l], lam_init)

        o_b = _dilated_attention(seq("qb"), seq("kb"), seq("vb"))

        s5p = _s5_params(s5_lambda_re[l], s5_lambda_im[l], s5_log_dt[l], s5_b_re[l], s5_b_im[l],
                         s5_c_re[l], s5_c_im[l])
        o_c = _s5_branch(seq("cu"), s5p, s5_d[l], w_glu[l], b_glu[l])

        k_cmp, v_cmp = _compress(seq("kvc"), nsa_pe[l], nsa_w1[l], nsa_w2[l])
        o_d = _nsa_attention(seq("dq"), seq("dqr"), seq("dg"), k_cmp, v_cmp,
                             seq("ks"), seq("vs"), seq("kw"), seq("vw"))

        o_e = _mem_attention(seq("eq"), mem.astype(F32), mem_norm_g[l], w_mem_kv[l])

        branches = [o.reshape(t, w) for o in (o_a, o_b, o_c, o_d, o_e)]
        x2 = _merge(x2, norm_g[l], branches, proj["zs"], w_merge, b_merge, w_branch, w_out, l,
                    final_g, final_norm=(l == depth - 1))
    return x2.reshape(bsz, s_len, d).astype(x.dtype)
```
